```python
import jax, jax.numpy as jnp
from jax import lax
import numpy as np

D_MODEL = 2048
BATCH = 2
SEQ = 8192
DEPTH = 1

CHUNK = 64
GDN_HEADS = 16
GDN_DK = 128
GDN_DV = 128
GDN_CONV = 4
QK_W = GDN_HEADS * GDN_DK
V_W = GDN_HEADS * GDN_DV
QKV_W = 2 * QK_W + V_W
POOL_WINDOWS = (2, 4, 8, 16)
POOL_WIDTH = D_MODEL // 2
POOL_GROUP = POOL_WIDTH // 4
XA_HEADS = 4
XA_HEAD_DIM = D_MODEL // XA_HEADS
MEM_LEN = 256
D_FF = 5504
FFN_CONV = 3
EPS = 1e-6
IN_SIZES = (QKV_W, V_W, GDN_HEADS, GDN_HEADS, POOL_WIDTH, D_MODEL, D_MODEL)
D_IN = QKV_W + V_W + 2 * GDN_HEADS + POOL_WIDTH + 2 * D_MODEL

kernel_name = "hybrid_gdn_pool_xattn_convffn_block"


def rms_norm(x, w):
    xf = x.astype(jnp.float32)
    y = xf * lax.rsqrt(jnp.mean(xf * xf, axis=-1, keepdims=True) + EPS)
    return (y * w.astype(jnp.float32)).astype(x.dtype)


def l2_norm(x):
    return x * lax.rsqrt(jnp.sum(x * x, axis=-1, keepdims=True) + EPS)


def causal_dwconv(x, w):
    K = w.shape[0]
    S = x.shape[1]
    xp = jnp.pad(x, ((0, 0), (K - 1, 0), (0, 0)))
    y = xp[:, K - 1:K - 1 + S] * w[K - 1]
    for j in range(K - 1):
        y = y + xp[:, j:j + S] * w[j]
    return y


def gated_delta_rule(q, k, v, g, beta):
    B, S, H, DK = q.shape
    DV = v.shape[-1]
    N = S // CHUNK

    def blocks(t):
        return t.reshape(B, N, CHUNK, H, -1).transpose(0, 3, 1, 2, 4)

    q, k, v = blocks(q), blocks(k), blocks(v)
    g = blocks(g[..., None])[..., 0]
    beta = blocks(beta[..., None])[..., 0]
    G = jnp.cumsum(g, axis=-1)
    pos = jnp.arange(CHUNK)
    incl = pos[:, None] >= pos[None, :]
    strict = pos[:, None] > pos[None, :]
    gap = G[..., :, None] - G[..., None, :]
    decay = jnp.where(incl, jnp.exp(jnp.where(incl, gap, 0.0)), 0.0)
    kk = jnp.einsum('bhnid,bhnjd->bhnij', k, k)
    lower = jnp.where(strict, beta[..., :, None] * decay * kk, 0.0)
    system = lower + jnp.eye(CHUNK, dtype=lower.dtype)
    rhs = jnp.concatenate([beta[..., None] * v, (beta * jnp.exp(G))[..., None] * k], axis=-1)
    sol = lax.linalg.triangular_solve(system, rhs, left_side=True, lower=True,
                                      unit_diagonal=True)
    u_v, w_k = sol[..., :DV], sol[..., DV:]
    attn = decay * jnp.einsum('bhnid,bhnjd->bhnij', q, k)
    q_dec = q * jnp.exp(G)[..., None]
    k_dec = k * jnp.exp(G[..., -1:] - G)[..., None]
    chunk_decay = jnp.exp(G[..., -1])

    def step(state, xs):
        u_c, w_c, a_c, qd_c, kd_c, cd_c = xs
        u = u_c - jnp.einsum('bhck,bhkv->bhcv', w_c, state)
        o = jnp.einsum('bhck,bhkv->bhcv', qd_c, state) + jnp.einsum('bhij,bhjv->bhiv', a_c, u)
        state = cd_c[..., None, None] * state + jnp.einsum('bhck,bhcv->bhkv', kd_c, u)
        return state, o

    xs = tuple(jnp.moveaxis(t, 2, 0) for t in (u_v, w_k, attn, q_dec, k_dec, chunk_decay))
    state0 = jnp.zeros((B, H, DK, DV), jnp.float32)
    _, o = lax.scan(step, state0, xs)
    return o.transpose(1, 0, 3, 2, 4).reshape(B, S, H, DV)


def multi_scale_pool(p, pool_w, pool_scale):
    B, S, _ = p.shape
    pf = p.astype(jnp.float32)
    csum = jnp.pad(jnp.cumsum(pf, axis=1), ((0, 0), (1, 0), (0, 0)))
    xs = jnp.split(pf, len(POOL_WINDOWS), axis=-1)
    cs = jnp.split(csum, len(POOL_WINDOWS), axis=-1)
    t = jnp.arange(S)
    outs = []
    for gi, win in enumerate(POOL_WINDOWS):
        lo = jnp.maximum(t + 1 - win, 0)
        cnt = jnp.minimum(t + 1, win).astype(jnp.float32)
        mean = (cs[gi][:, 1:] - cs[gi][:, lo]) / cnt[None, :, None]
        outs.append(mean - xs[gi])
    y = jnp.stack(outs, axis=2)
    y = jnp.einsum('bsgc,gcd->bsgd', y, pool_w.astype(jnp.float32)).reshape(B, S, POOL_WIDTH)
    return (y * pool_scale.astype(jnp.float32)).astype(p.dtype)


def hybrid_mixer(h, w_in, conv_qkv, a_log, dt_bias, gdn_norm, pool_w, pool_scale,
                 w_branch_a, w_branch_b, w_mix_out):
    B, S, _ = h.shape
    splits = [int(i) for i in np.cumsum(IN_SIZES)[:-1]]
    qkv, z, b_raw, a_raw, p, gate_a, gate_b = jnp.split(h @ w_in, splits, axis=-1)
    qkv = jax.nn.silu(causal_dwconv(qkv, conv_qkv))
    q, k, v = jnp.split(qkv, [QK_W, 2 * QK_W], axis=-1)
    q = l2_norm(q.reshape(B, S, GDN_HEADS, GDN_DK).astype(jnp.float32)) * (GDN_DK ** -0.5)
    k = l2_norm(k.reshape(B, S, GDN_HEADS, GDN_DK).astype(jnp.float32))
    v = v.reshape(B, S, GDN_HEADS, GDN_DV).astype(jnp.float32)
    beta = jax.nn.sigmoid(b_raw.astype(jnp.float32))
    g = -jnp.exp(a_log.astype(jnp.float32)) * jax.nn.softplus(
        a_raw.astype(jnp.float32) + dt_bias.astype(jnp.float32))
    o = gated_delta_rule(q, k, v, g, beta)
    o = rms_norm(o, gdn_norm) * jax.nn.silu(z.reshape(B, S, GDN_HEADS, GDN_DV).astype(jnp.float32))
    y_a = o.reshape(B, S, V_W).astype(h.dtype) @ w_branch_a
    y_b = multi_scale_pool(p, pool_w, pool_scale) @ w_branch_b
    merged = jax.nn.sigmoid(gate_a) * y_a + jax.nn.sigmoid(gate_b) * y_b
    return merged @ w_mix_out


def memory_cross_attention(h, m, w_xq, w_xkv, w_xo):
    B, S, _ = h.shape
    M = m.shape[1]
    q = (h @ w_xq).reshape(B, S, XA_HEADS, XA_HEAD_DIM)
    k, v = jnp.split(m @ w_xkv, 2, axis=-1)
    k = k.reshape(B, M, XA_HEADS, XA_HEAD_DIM)
    v = v.reshape(B, M, XA_HEADS, XA_HEAD_DIM)
    s = jnp.einsum('bshd,bmhd->bhsm', q, k).astype(jnp.float32) * (XA_HEAD_DIM ** -0.5)
    pr = jax.nn.softmax(s, axis=-1).astype(v.dtype)
    o = jnp.einsum('bhsm,bmhd->bshd', pr, v).reshape(B, S, D_MODEL)
    return o @ w_xo


def conv_glu_ffn(h, w_up, ffn_conv_w, ffn_conv_b, w_down):
    u = causal_dwconv(h @ w_up, ffn_conv_w) + ffn_conv_b
    a, b = jnp.split(u, 2, axis=-1)
    return (jax.nn.silu(a) * b) @ w_down


def setup_inputs(seed: int = 0) -> dict:
    key = jax.random.key(seed)
    ks = iter(jax.random.split(key, 32))

    def dense(shape, fan_in):
        return jax.random.normal(next(ks), shape, jnp.float32) * (fan_in ** -0.5)

    def gain(shape):
        return 1.0 + 0.05 * jax.random.normal(next(ks), shape, jnp.float32)

    L = DEPTH
    x = jax.random.normal(next(ks), (BATCH, SEQ, D_MODEL), jnp.float32)
    mem = jax.random.normal(next(ks), (BATCH, MEM_LEN, D_MODEL), jnp.float32)
    a_log = jnp.log(jax.random.uniform(next(ks), (L, GDN_HEADS), jnp.float32, 1.0, 16.0))
    dt = jnp.exp(jax.random.uniform(next(ks), (L, GDN_HEADS), jnp.float32,
                                    np.log(1e-3), np.log(1e-1)))
    dt_bias = dt + jnp.log(-jnp.expm1(-dt))
    return {
        "x": x,
        "mem": mem,
        "mix_pre_norm": gain((L, D_MODEL)),
        "w_in": dense((L, D_MODEL, D_IN), D_MODEL),
        "conv_qkv": dense((L, GDN_CONV, QKV_W), GDN_CONV),
        "a_log": a_log,
        "dt_bias": dt_bias,
        "gdn_norm": gain((L, GDN_DV)),
        "pool_w": dense((L, len(POOL_WINDOWS), POOL_GROUP, POOL_GROUP), POOL_GROUP),
        "pool_scale": gain((L, POOL_WIDTH)),
        "w_branch_a": dense((L, V_W, D_MODEL), V_W),
        "w_branch_b": dense((L, POOL_WIDTH, D_MODEL), POOL_WIDTH),
        "w_mix_out": dense((L, D_MODEL, D_MODEL), D_MODEL),
        "mix_post_norm": gain((L, D_MODEL)),
        "xa_pre_norm": gain((L, D_MODEL)),
        "mem_norm": gain((L, D_MODEL)),
        "w_xq": dense((L, D_MODEL, D_MODEL), D_MODEL),
        "w_xkv": dense((L, D_MODEL, 2 * D_MODEL), D_MODEL),
        "w_xo": dense((L, D_MODEL, D_MODEL), D_MODEL),
        "xa_post_norm": gain((L, D_MODEL)),
        "ffn_pre_norm": gain((L, D_MODEL)),
        "w_up": dense((L, D_MODEL, 2 * D_FF), D_MODEL),
        "ffn_conv_w": dense((L, FFN_CONV, 2 * D_FF), FFN_CONV),
        "ffn_conv_b": 0.01 * jax.random.normal(next(ks), (L, 2 * D_FF), jnp.float32),
        "w_down": dense((L, D_FF, D_MODEL), D_FF),
        "ffn_post_norm": gain((L, D_MODEL)),
    }


def reference(x, mem, mix_pre_norm, w_in, conv_qkv, a_log, dt_bias, gdn_norm, pool_w,
              pool_scale, w_branch_a, w_branch_b, w_mix_out, mix_post_norm, xa_pre_norm,
              mem_norm, w_xq, w_xkv, w_xo, xa_post_norm, ffn_pre_norm, w_up, ffn_conv_w,
              ffn_conv_b, w_down, ffn_post_norm):
    for l in range(DEPTH):
        h = rms_norm(x, mix_pre_norm[l])
        y = hybrid_mixer(h, w_in[l], conv_qkv[l], a_log[l], dt_bias[l], gdn_norm[l], pool_w[l],
                         pool_scale[l], w_branch_a[l], w_branch_b[l], w_mix_out[l])
        x = x + rms_norm(y, mix_post_norm[l])
        h = rms_norm(x, xa_pre_norm[l])
        m = rms_norm(mem, mem_norm[l])
        y = memory_cross_attention(h, m, w_xq[l], w_xkv[l], w_xo[l])
        x = x + rms_norm(y, xa_post_norm[l])
        h = rms_norm(x, ffn_pre_norm[l])
        y = conv_glu_ffn(h, w_up[l], ffn_conv_w[l], ffn_conv_b[l], w_down[l])
        x = x + rms_norm(y, ffn_post_norm[l])
    return x
```

```python
import functools

import jax
import jax.numpy as jnp
from jax import lax
from jax.experimental import pallas as pl
from jax.experimental.pallas import tpu as pltpu

F32 = jnp.float32
BF16 = jnp.bfloat16

D_MODEL = 2048
CHUNK = 64
GDN_HEADS = 16
GDN_DK = 128
GDN_DV = 128
GDN_CONV = 4
QK_W = GDN_HEADS * GDN_DK
V_W = GDN_HEADS * GDN_DV
QKV_W = 2 * QK_W + V_W
POOL_WINDOWS = (2, 4, 8, 16)
POOL_WIDTH = D_MODEL // 2
POOL_GROUP = POOL_WIDTH // 4
XA_HEADS = 4
XA_HEAD_DIM = D_MODEL // XA_HEADS
D_FF = 5504
FFN_CONV = 3
EPS = 1e-6

LANES = 128
SUBLANES = 8
VMEM_LIMIT = 56 * 1024 * 1024

COL_Z = QKV_W
COL_GA = COL_Z + V_W
COL_GB = COL_GA + D_MODEL
COL_P = COL_GB + D_MODEL
COL_BA = COL_P + POOL_WIDTH
D_IN_PAD = COL_BA + LANES

HEADS_PER_GROUP = 4
HEAD_GROUPS = GDN_HEADS // HEADS_PER_GROUP
GROUP_W = HEADS_PER_GROUP * GDN_DK
GDN_ROWS = 256
CHUNKS_PER_STEP = GDN_ROWS // CHUNK
LOG2_CHUNK = 6

D_FF_PAD = 5632
FFN_TN = 512
DOWN_TK = 1408


def _cparams(sem):
    return pltpu.CompilerParams(dimension_semantics=sem, vmem_limit_bytes=VMEM_LIMIT)


def _rms(xf, w):
    ms = jnp.mean(xf * xf, axis=-1, keepdims=True)
    return xf * lax.rsqrt(ms + EPS) * w


def _silu(x):
    return x * jax.nn.sigmoid(x)


def _bdot(a, b):
    return jnp.dot(a.astype(BF16), b.astype(BF16), preferred_element_type=F32)


def _norm_mm_kernel(x_ref, nw_ref, w_ref, o_ref, h_ref):
    @pl.when(pl.program_id(1) == 0)
    def _():
        h_ref[...] = _rms(x_ref[...].astype(F32), nw_ref[...]).astype(BF16)

    o_ref[...] = jnp.dot(h_ref[...], w_ref[...], preferred_element_type=F32).astype(o_ref.dtype)


def _norm_mm(x, nw, w, *, tm, tn, out_dtype):
    m, k = x.shape
    n = w.shape[1]
    return pl.pallas_call(
        _norm_mm_kernel,
        grid=(m // tm, n // tn),
        in_specs=[
            pl.BlockSpec((tm, k), lambda i, j: (i, 0)),
            pl.BlockSpec((1, k), lambda i, j: (0, 0)),
            pl.BlockSpec((k, tn), lambda i, j: (0, j)),
        ],
        out_specs=pl.BlockSpec((tm, tn), lambda i, j: (i, j)),
        out_shape=jax.ShapeDtypeStruct((m, n), out_dtype),
        scratch_shapes=[pltpu.VMEM((tm, k), BF16)],
        compiler_params=_cparams(("parallel", "arbitrary")),
        name="norm_mm",
    )(x, nw.reshape(1, k), w)


def _mm_kernel(x_ref, w_ref, o_ref):
    o_ref[...] = jnp.dot(x_ref[...], w_ref[...], preferred_element_type=F32).astype(o_ref.dtype)


def _mm(x, w, *, tm, tn, out_dtype):
    m, k = x.shape
    n = w.shape[1]
    return pl.pallas_call(
        _mm_kernel,
        grid=(m // tm, n // tn),
        in_specs=[
            pl.BlockSpec((tm, k), lambda i, j: (i, 0)),
            pl.BlockSpec((k, tn), lambda i, j: (0, j)),
        ],
        out_specs=pl.BlockSpec((tm, tn), lambda i, j: (i, j)),
        out_shape=jax.ShapeDtypeStruct((m, n), out_dtype),
        compiler_params=_cparams(("parallel", "arbitrary")),
        name="mm",
    )(x, w)


def _mm_norm_res_kernel(a_ref, w_ref, res_ref, pw_ref, nw_ref, x_ref, h_ref, acc_ref):
    kk = pl.program_id(1)

    @pl.when(kk == 0)
    def _():
        acc_ref[...] = jnp.zeros_like(acc_ref)

    acc_ref[...] += jnp.dot(a_ref[...], w_ref[...], preferred_element_type=F32)

    @pl.when(kk == pl.num_programs(1) - 1)
    def _():
        xn = res_ref[...] + _rms(acc_ref[...], pw_ref[...])
        x_ref[...] = xn
        h_ref[...] = _rms(xn, nw_ref[...]).astype(BF16)


def _mm_norm_res(a, w, res, post_w, next_w, *, tm, tk):
    m, k = a.shape
    n = w.shape[1]
    return pl.pallas_call(
        _mm_norm_res_kernel,
        grid=(m // tm, k // tk),
        in_specs=[
            pl.BlockSpec((tm, tk), lambda i, j: (i, j)),
            pl.BlockSpec((tk, n), lambda i, j: (j, 0)),
            pl.BlockSpec((tm, n), lambda i, j: (i, 0)),
            pl.BlockSpec((1, n), lambda i, j: (0, 0)),
            pl.BlockSpec((1, n), lambda i, j: (0, 0)),
        ],
        out_specs=[
            pl.BlockSpec((tm, n), lambda i, j: (i, 0)),
            pl.BlockSpec((tm, n), lambda i, j: (i, 0)),
        ],
        out_shape=[
            jax.ShapeDtypeStruct((m, n), F32),
            jax.ShapeDtypeStruct((m, n), BF16),
        ],
        scratch_shapes=[pltpu.VMEM((tm, n), F32)],
        compiler_params=_cparams(("parallel", "arbitrary")),
        name="mm_norm_res",
    )(a, w, res, post_w.reshape(1, n), next_w.reshape(1, n))


GATE_ROWS = 512


def _gates_kernel(ba_ref, alog_ref, dtb_ref, o_ref):
    ba = ba_ref[...]
    beta = jax.nn.sigmoid(ba)
    xs = ba + dtb_ref[...]
    softplus = jnp.maximum(xs, 0.0) + jnp.log1p(jnp.exp(-jnp.abs(xs)))
    g = -jnp.exp(alog_ref[...]) * softplus
    ri = lax.broadcasted_iota(jnp.int32, (GATE_ROWS, GATE_ROWS), 0)
    ci = lax.broadcasted_iota(jnp.int32, (GATE_ROWS, GATE_ROWS), 1)
    same = (ri >> LOG2_CHUNK) == (ci >> LOG2_CHUNK)
    tri = jnp.where(same & (ri >= ci), 1.0, 0.0).astype(F32)
    ones = jnp.where(same, 1.0, 0.0).astype(F32)
    gc = jnp.dot(tri, g, preferred_element_type=F32, precision=lax.Precision.HIGHEST)
    gl = jnp.dot(ones, g, preferred_element_type=F32, precision=lax.Precision.HIGHEST)
    lane = lax.broadcasted_iota(jnp.int32, ba.shape, 1)
    h = GDN_HEADS
    out = jnp.where(lane < h, beta, gc)
    out = jnp.where(lane < 2 * h, out, pltpu.roll(gl - gc, h, axis=1))
    out = jnp.where(lane < 3 * h, out, pltpu.roll(jnp.exp(gl), 2 * h, axis=1))
    o_ref[...] = out


def _gates(proj, alog_l, dtb_l):
    m = proj.shape[0]
    return pl.pallas_call(
        _gates_kernel,
        grid=(m // GATE_ROWS,),
        in_specs=[
            pl.BlockSpec((GATE_ROWS, LANES), lambda i: (i, COL_BA // LANES)),
            pl.BlockSpec((1, LANES), lambda i: (0, 0)),
            pl.BlockSpec((1, LANES), lambda i: (0, 0)),
        ],
        out_specs=pl.BlockSpec((GATE_ROWS, LANES), lambda i: (i, 0)),
        out_shape=jax.ShapeDtypeStruct((m, LANES), F32),
        compiler_params=_cparams(("parallel",)),
        name="gdn_gates",
    )(proj, alog_l, dtb_l)


def _gdn_kernel(cd_ref, q_ref, k_ref, v_ref, z_ref, wq_ref, wk_ref, wv_ref, colg_ref, rowg_ref,
                gn_ref, o_ref, cbuf, nbuf, lv_ref, s_ref, ubuf, obuf):
    b = pl.program_id(0)
    hg = pl.program_id(1)
    si = pl.program_id(2)
    R = GDN_ROWS

    @pl.when(si == 0)
    def _():
        s_ref[...] = jnp.zeros_like(s_ref)
        cbuf[:, 0:SUBLANES, :] = jnp.zeros((3, SUBLANES, GROUP_W), F32)
        ri = lax.broadcasted_iota(jnp.int32, (R, R), 0)
        ci = lax.broadcasted_iota(jnp.int32, (R, R), 1)
        x = ri ^ ci
        lv = jnp.full((R, R), -1, jnp.int32)
        for t in range(LOG2_CHUNK):
            lv = lv + jnp.where(x >= (1 << t), 1, 0)
        ok = ((ri >> LOG2_CHUNK) == (ci >> LOG2_CHUNK)) & (ri >= ci)
        lv_ref[...] = jnp.where(ok, lv, -2)

    for t, (x_ref, w_ref) in enumerate(((q_ref, wq_ref), (k_ref, wk_ref), (v_ref, wv_ref))):
        cbuf[t, SUBLANES:SUBLANES + R, :] = x_ref[...]
        y = x_ref[...] * w_ref[GDN_CONV - 1:GDN_CONV, :]
        for j in range(GDN_CONV - 1):
            off = SUBLANES - (GDN_CONV - 1) + j
            y = y + cbuf[t, off:off + R, :] * w_ref[j:j + 1, :]
        cbuf[t, 0:SUBLANES, :] = cbuf[t, R:R + SUBLANES, :]
        y = _silu(y)
        if t < 2:
            scale = GDN_DK ** -0.5 if t == 0 else 1.0
            for hl in range(HEADS_PER_GROUP):
                sl = slice(hl * GDN_DK, (hl + 1) * GDN_DK)
                yh = y[:, sl]
                ss = jnp.sum(yh * yh, axis=-1, keepdims=True)
                nbuf[t, :, sl] = yh * (lax.rsqrt(ss + EPS) * scale)
        else:
            nbuf[t] = y

    lv = lv_ref[...]
    incl = lv >= -1
    eye = jnp.where(lv == -1, 1.0, 0.0).astype(F32)
    colg = colg_ref[...]
    for hl in range(HEADS_PER_GROUP):
        sl = slice(hl * GDN_DK, (hl + 1) * GDN_DK)
        qh = nbuf[0, :, sl]
        kh = nbuf[1, :, sl]
        vh = nbuf[2, :, sl]
        beta = colg[:, hl:hl + 1]
        gcol = colg[:, 4 + hl:5 + hl]
        gtail = colg[:, 8 + hl:9 + hl]
        grow = rowg_ref[hl:hl + 1, :]
        exp_g = jnp.exp(gcol)
        decay = jnp.where(incl, jnp.exp(jnp.where(incl, gcol - grow, 0.0)), 0.0)
        kb = kh.astype(BF16)
        qk_kk = lax.dot_general(jnp.concatenate([qh.astype(BF16), kb], axis=0), kb,
                                (((1,), (1,)), ((), ())), preferred_element_type=F32)
        attn = decay * qk_kk[0:R]
        low = jnp.where(lv >= 0, beta * decay * qk_kk[R:2 * R], 0.0)
        xinv = eye - jnp.where(lv == 0, low, 0.0)
        for level in range(1, LOG2_CHUNK):
            lc = jnp.where(lv == level, low, 0.0)
            xinv = xinv - _bdot(xinv, _bdot(lc, xinv))
        rhs = jnp.concatenate([beta * vh, (beta * exp_g) * kh], axis=1)
        sol = _bdot(xinv, rhs)
        u_v = sol[:, 0:GDN_DV]
        w_k = sol[:, GDN_DV:]
        q_dec = qh * exp_g
        k_dec = kh * jnp.exp(gtail)
        wq = jnp.concatenate([w_k, q_dec], axis=1).astype(BF16)
        for c in range(CHUNKS_PER_STEP):
            rs = slice(c * CHUNK, (c + 1) * CHUNK)
            state = s_ref[hl]
            sb = state.astype(BF16)
            ws = jnp.dot(wq[rs, 0:GDN_DK], sb, preferred_element_type=F32)
            qs = jnp.dot(wq[rs, GDN_DK:], sb, preferred_element_type=F32)
            u = u_v[rs] - ws
            ubuf[rs, :] = u
            obuf[rs, :] = qs
            cd = cd_ref[b * GDN_HEADS + hg * HEADS_PER_GROUP + hl, si * CHUNKS_PER_STEP + c]
            upd = lax.dot_general(k_dec[rs].astype(BF16), u.astype(BF16),
                                  (((0,), (0,)), ((), ())), preferred_element_type=F32)
            s_ref[hl] = cd * state + upd
        o = obuf[...] + _bdot(attn, ubuf[...])
        on = _rms(o, gn_ref[...]) * _silu(z_ref[:, sl])
        o_ref[:, sl] = on.astype(o_ref.dtype)


def _gdn(proj, conv_w, colg4, rowg4, cdecay, gdn_norm, batch, seq):
    m = proj.shape[0]
    ns = seq // GDN_ROWS
    gw = GROUP_W

    def xspec(col0):
        return pl.BlockSpec((GDN_ROWS, gw), lambda b, h, s: (b * ns + s, col0 // gw + h))

    def wspec(col0):
        return pl.BlockSpec((GDN_CONV, gw), lambda b, h, s: (0, col0 // gw + h))

    return pl.pallas_call(
        _gdn_kernel,
        grid=(batch, HEAD_GROUPS, ns),
        in_specs=[
            pl.BlockSpec(memory_space=pltpu.SMEM),
            xspec(0), xspec(QK_W), xspec(2 * QK_W), xspec(COL_Z),
            wspec(0), wspec(QK_W), wspec(2 * QK_W),
            pl.BlockSpec((None, GDN_ROWS, LANES), lambda b, h, s: (h, b * ns + s, 0)),
            pl.BlockSpec((None, None, HEADS_PER_GROUP, GDN_ROWS), lambda b, h, s: (b, h, 0, s)),
            pl.BlockSpec((1, GDN_DV), lambda b, h, s: (0, 0)),
        ],
        out_specs=pl.BlockSpec((GDN_ROWS, gw), lambda b, h, s: (b * ns + s, h)),
        out_shape=jax.ShapeDtypeStruct((m, V_W), BF16),
        scratch_shapes=[
            pltpu.VMEM((3, GDN_ROWS + SUBLANES, gw), F32),
            pltpu.VMEM((3, GDN_ROWS, gw), F32),
            pltpu.VMEM((GDN_ROWS, GDN_ROWS), jnp.int32),
            pltpu.VMEM((HEADS_PER_GROUP, GDN_DK, GDN_DV), F32),
            pltpu.VMEM((GDN_ROWS, GDN_DV), F32),
            pltpu.VMEM((GDN_ROWS, GDN_DV), F32),
        ],
        compiler_params=_cparams(("parallel", "parallel", "arbitrary")),
        name="gdn",
    )(cdecay, proj, proj, proj, proj, conv_w, conv_w, conv_w, colg4, rowg4,
      gdn_norm.reshape(1, GDN_DV))


POOL_ROWS = 512
POOL_HALO = 16


def _pool_kernel(p_ref, w_ref, sc_ref, o_ref, buf):
    si = pl.program_id(1)
    R = POOL_ROWS

    @pl.when(si == 0)
    def _():
        buf[0:POOL_HALO, :] = jnp.zeros((POOL_HALO, POOL_WIDTH), F32)

    buf[POOL_HALO:POOL_HALO + R, :] = p_ref[...]
    t = si * R + lax.broadcasted_iota(jnp.int32, (R, 1), 0)
    for gi, win in enumerate(POOL_WINDOWS):
        sl = slice(gi * POOL_GROUP, (gi + 1) * POOL_GROUP)
        x = p_ref[:, sl]
        acc = x
        for d in range(1, win):
            acc = acc + buf[POOL_HALO - d:POOL_HALO - d + R, sl]
        cnt = jnp.minimum(t + 1, win).astype(F32)
        y = acc / cnt - x
        o = _bdot(y, w_ref[gi]) * sc_ref[:, sl]
        o_ref[:, sl] = o.astype(o_ref.dtype)
    buf[0:POOL_HALO, :] = buf[R:R + POOL_HALO, :]


def _pool(proj, pool_w, pool_scale, batch, seq):
    m = proj.shape[0]
    ns = seq // POOL_ROWS
    return pl.pallas_call(
        _pool_kernel,
        grid=(batch, ns),
        in_specs=[
            pl.BlockSpec((POOL_ROWS, POOL_WIDTH), lambda b, s: (b * ns + s, COL_P // POOL_WIDTH)),
            pl.BlockSpec((len(POOL_WINDOWS), POOL_GROUP, POOL_GROUP), lambda b, s: (0, 0, 0)),
            pl.BlockSpec((1, POOL_WIDTH), lambda b, s: (0, 0)),
        ],
        out_specs=pl.BlockSpec((POOL_ROWS, POOL_WIDTH), lambda b, s: (b * ns + s, 0)),
        out_shape=jax.ShapeDtypeStruct((m, POOL_WIDTH), BF16),
        scratch_shapes=[pltpu.VMEM((POOL_ROWS + POOL_HALO, POOL_WIDTH), F32)],
        compiler_params=_cparams(("parallel", "arbitrary")),
        name="pool",
    )(proj, pool_w, pool_scale.reshape(1, POOL_WIDTH))


def _merge_kernel(a_ref, p_ref, wa_ref, wb_ref, ga_ref, gb_ref, o_ref):
    ya = jnp.dot(a_ref[...], wa_ref[...], preferred_element_type=F32)
    yb = jnp.dot(p_ref[...], wb_ref[...], preferred_element_type=F32)
    merged = jax.nn.sigmoid(ga_ref[...]) * ya + jax.nn.sigmoid(gb_ref[...]) * yb
    o_ref[...] = merged.astype(o_ref.dtype)


def _merge(og, pooled, wa, wb, proj, *, tm, tn):
    m = og.shape[0]
    n = wa.shape[1]
    return pl.pallas_call(
        _merge_kernel,
        grid=(m // tm, n // tn),
        in_specs=[
            pl.BlockSpec((tm, V_W), lambda i, j: (i, 0)),
            pl.BlockSpec((tm, POOL_WIDTH), lambda i, j: (i, 0)),
            pl.BlockSpec((V_W, tn), lambda i, j: (0, j)),
            pl.BlockSpec((POOL_WIDTH, tn), lambda i, j: (0, j)),
            pl.BlockSpec((tm, tn), lambda i, j: (i, COL_GA // tn + j)),
            pl.BlockSpec((tm, tn), lambda i, j: (i, COL_GB // tn + j)),
        ],
        out_specs=pl.BlockSpec((tm, tn), lambda i, j: (i, j)),
        out_shape=jax.ShapeDtypeStruct((m, n), BF16),
        compiler_params=_cparams(("parallel", "arbitrary")),
        name="merge",
    )(og, pooled, wa, wb, proj, proj)


def _xattn_kernel(q_ref, k_ref, v_ref, o_ref):
    s = lax.dot_general(q_ref[...], k_ref[...], (((1,), (1,)), ((), ())),
                        preferred_element_type=F32) * (XA_HEAD_DIM ** -0.5)
    s = s - jnp.max(s, axis=-1, keepdims=True)
    e = jnp.exp(s)
    p = e / jnp.sum(e, axis=-1, keepdims=True)
    o_ref[...] = jnp.dot(p.astype(BF16), v_ref[...], preferred_element_type=F32).astype(o_ref.dtype)


def _xattn(q, kv, batch, seq, mem_len, *, ts):
    m = q.shape[0]
    ns = seq // ts
    return pl.pallas_call(
        _xattn_kernel,
        grid=(batch, ns, XA_HEADS),
        in_specs=[
            pl.BlockSpec((ts, XA_HEAD_DIM), lambda b, s, h: (b * ns + s, h)),
            pl.BlockSpec((mem_len, XA_HEAD_DIM), lambda b, s, h: (b, h)),
            pl.BlockSpec((mem_len, XA_HEAD_DIM), lambda b, s, h: (b, XA_HEADS + h)),
        ],
        out_specs=pl.BlockSpec((ts, XA_HEAD_DIM), lambda b, s, h: (b * ns + s, h)),
        out_shape=jax.ShapeDtypeStruct((m, D_MODEL), BF16),
        compiler_params=_cparams(("parallel", "parallel", "arbitrary")),
        name="xattn",
    )(q, kv, kv)


def _ffn_up_kernel(h_ref, wa_ref, wb_ref, cwa_ref, cwb_ref, ba_ref, bb_ref, o_ref, ubuf, carry,
                   *, tiles_per_seq):
    i = pl.program_id(0)
    j = pl.program_id(1)
    tm = h_ref.shape[0]
    halves = []
    for t, (w_ref, cw_ref, b_ref) in enumerate(((wa_ref, cwa_ref, ba_ref), (wb_ref, cwb_ref, bb_ref))):
        u = jnp.dot(h_ref[...], w_ref[...], preferred_element_type=F32)
        prev = carry[t, j]
        ubuf[t, 0:SUBLANES, :] = jnp.where(i % tiles_per_seq == 0, 0.0, prev)
        ubuf[t, SUBLANES:SUBLANES + tm, :] = u
        carry[t, j] = u[tm - SUBLANES:tm, :]
        y = u * cw_ref[FFN_CONV - 1:FFN_CONV, :] + b_ref[...]
        for d in range(FFN_CONV - 1):
            off = SUBLANES - (FFN_CONV - 1) + d
            y = y + ubuf[t, off:off + tm, :] * cw_ref[d:d + 1, :]
        halves.append(y)
    o_ref[...] = (_silu(halves[0]) * halves[1]).astype(o_ref.dtype)


def _ffn_up(h, wa, wb, cwa, cwb, ba, bb, seq, *, tm):
    m, k = h.shape
    n = wa.shape[1]
    wspec = pl.BlockSpec((k, FFN_TN), lambda i, j: (0, j))
    cspec = pl.BlockSpec((FFN_CONV, FFN_TN), lambda i, j: (0, j))
    bspec = pl.BlockSpec((1, FFN_TN), lambda i, j: (0, j))
    return pl.pallas_call(
        functools.partial(_ffn_up_kernel, tiles_per_seq=seq // tm),
        grid=(m // tm, n // FFN_TN),
        in_specs=[pl.BlockSpec((tm, k), lambda i, j: (i, 0)), wspec, wspec, cspec, cspec, bspec, bspec],
        out_specs=pl.BlockSpec((tm, FFN_TN), lambda i, j: (i, j)),
        out_shape=jax.ShapeDtypeStruct((m, n), BF16),
        scratch_shapes=[
            pltpu.VMEM((2, tm + SUBLANES, FFN_TN), F32),
            pltpu.VMEM((2, n // FFN_TN, SUBLANES, FFN_TN), F32),
        ],
        compiler_params=_cparams(("arbitrary", "arbitrary")),
        name="ffn_up",
    )(h, wa, wb, cwa, cwb, ba, bb)


def _pad_cols(a, n):
    return jnp.pad(a, ((0, 0), (0, n - a.shape[1])))


def _layer(x2d, mem2d, batch, seq, mem_len, p):
    m = x2d.shape[0]
    h = GDN_HEADS
    w_in = p["w_in"]
    o_z = QKV_W
    o_b = o_z + V_W
    o_a = o_b + h
    o_p = o_a + h
    o_ga = o_p + POOL_WIDTH
    o_gb = o_ga + D_MODEL
    w_in_r = jnp.concatenate(
        [w_in[:, :o_z], w_in[:, o_z:o_b], w_in[:, o_ga:o_gb], w_in[:, o_gb:], w_in[:, o_p:o_ga],
         w_in[:, o_b:o_p], jnp.zeros((D_MODEL, LANES - 2 * h), F32)], axis=1).astype(BF16)
    alog_l = jnp.zeros((1, LANES), F32).at[0, h:2 * h].set(p["a_log"])
    dtb_l = jnp.zeros((1, LANES), F32).at[0, h:2 * h].set(p["dt_bias"])
    w_up = p["w_up"]
    wa_up = _pad_cols(w_up[:, :D_FF], D_FF_PAD).astype(BF16)
    wb_up = _pad_cols(w_up[:, D_FF:], D_FF_PAD).astype(BF16)
    cwa = _pad_cols(p["ffn_conv_w"][:, :D_FF], D_FF_PAD)
    cwb = _pad_cols(p["ffn_conv_w"][:, D_FF:], D_FF_PAD)
    fba = _pad_cols(p["ffn_conv_b"][None, :D_FF], D_FF_PAD)
    fbb = _pad_cols(p["ffn_conv_b"][None, D_FF:], D_FF_PAD)
    w_down = jnp.pad(p["w_down"], ((0, D_FF_PAD - D_FF), (0, 0))).astype(BF16)

    proj = _norm_mm(x2d, p["mix_pre_norm"], w_in_r, tm=1024, tn=896, out_dtype=F32)
    gates = _gates(proj, alog_l, dtb_l)
    g4 = gates[:, :4 * h].reshape(m, 4, HEAD_GROUPS, HEADS_PER_GROUP)
    colg4 = jnp.pad(g4[:, :3].transpose(2, 0, 1, 3).reshape(HEAD_GROUPS, m, 3 * HEADS_PER_GROUP),
                    ((0, 0), (0, 0), (0, LANES - 3 * HEADS_PER_GROUP)))
    rowg4 = gates[:, h:2 * h].reshape(batch, seq, HEAD_GROUPS, HEADS_PER_GROUP).transpose(0, 2, 3, 1)
    cdecay = gates[CHUNK - 1::CHUNK, 3 * h:4 * h].reshape(batch, seq // CHUNK, h)
    cdecay = cdecay.transpose(0, 2, 1).reshape(batch * h, seq // CHUNK)
    og = _gdn(proj, p["conv_qkv"], colg4, rowg4, cdecay, p["gdn_norm"], batch, seq)
    pooled = _pool(proj, p["pool_w"].astype(BF16), p["pool_scale"], batch, seq)
    merged = _merge(og, pooled, p["w_branch_a"].astype(BF16), p["w_branch_b"].astype(BF16), proj,
                    tm=1024, tn=512)
    x1, h2 = _mm_norm_res(merged, p["w_mix_out"].astype(BF16), x2d, p["mix_post_norm"],
                          p["xa_pre_norm"], tm=512, tk=D_MODEL)
    kv = _norm_mm(mem2d, p["mem_norm"], p["w_xkv"].astype(BF16), tm=mem2d.shape[0], tn=1024,
                  out_dtype=BF16)
    q = _mm(h2, p["w_xq"].astype(BF16), tm=1024, tn=1024, out_dtype=BF16)
    xo = _xattn(q, kv, batch, seq, mem_len, ts=1024)
    x2, h3 = _mm_norm_res(xo, p["w_xo"].astype(BF16), x1, p["xa_post_norm"], p["ffn_pre_norm"],
                          tm=512, tk=D_MODEL)
    act = _ffn_up(h3, wa_up, wb_up, cwa, cwb, fba, fbb, seq, tm=1024)
    x3, _ = _mm_norm_res(act, w_down, x2, p["ffn_post_norm"], p["ffn_post_norm"], tm=512, tk=DOWN_TK)
    return x3


def kernel(x, mem, mix_pre_norm, w_in, conv_qkv, a_log, dt_bias, gdn_norm, pool_w, pool_scale,
           w_branch_a, w_branch_b, w_mix_out, mix_post_norm, xa_pre_norm, mem_norm, w_xq, w_xkv,
           w_xo, xa_post_norm, ffn_pre_norm, w_up, ffn_conv_w, ffn_conv_b, w_down, ffn_post_norm):
    params = dict(mix_pre_norm=mix_pre_norm, w_in=w_in, conv_qkv=conv_qkv, a_log=a_log,
                  dt_bias=dt_bias, gdn_norm=gdn_norm, pool_w=pool_w, pool_scale=pool_scale,
                  w_branch_a=w_branch_a, w_branch_b=w_branch_b, w_mix_out=w_mix_out,
                  mix_post_norm=mix_post_norm, xa_pre_norm=xa_pre_norm, mem_norm=mem_norm,
                  w_xq=w_xq, w_xkv=w_xkv, w_xo=w_xo, xa_post_norm=xa_post_norm,
                  ffn_pre_norm=ffn_pre_norm, w_up=w_up, ffn_conv_w=ffn_conv_w,
                  ffn_conv_b=ffn_conv_b, w_down=w_down, ffn_post_norm=ffn_post_norm)
    batch, seq, d = x.shape
    mem_len = mem.shape[1]
    x2d = x.reshape(batch * seq, d)
    mem2d = mem.reshape(batch * mem_len, d)
    for l in range(w_in.shape[0]):
        x2d = _layer(x2d, mem2d, batch, seq, mem_len, {k: v[l] for k, v in params.items()})
    return x2d.reshape(batch, seq, d)
```

```python
import functools

import jax
import jax.numpy as jnp
from jax import lax
from jax.experimental import pallas as pl
from jax.experimental.pallas import tpu as pltpu

F32 = jnp.float32
BF16 = jnp.bfloat16

D_MODEL = 2048
CHUNK = 64
GDN_HEADS = 16
GDN_DK = 128
GDN_DV = 128
GDN_CONV = 4
QK_W = GDN_HEADS * GDN_DK
V_W = GDN_HEADS * GDN_DV
QKV_W = 2 * QK_W + V_W
POOL_WINDOWS = (2, 4, 8, 16)
POOL_WIDTH = D_MODEL // 2
POOL_GROUP = POOL_WIDTH // 4
XA_HEADS = 4
XA_HEAD_DIM = D_MODEL // XA_HEADS
D_FF = 5504
FFN_CONV = 3
EPS = 1e-6

LANES = 128
SUBLANES = 8
VMEM_LIMIT = 56 * 1024 * 1024

GCOL_Z = 0
GCOL_GA = V_W
GCOL_GB = GCOL_GA + D_MODEL
GCOL_P = GCOL_GB + D_MODEL
GATE_W = GCOL_P + POOL_WIDTH

HEADS_PER_GROUP = 4
HEAD_GROUPS = GDN_HEADS // HEADS_PER_GROUP
GROUP_W = HEADS_PER_GROUP * GDN_DK
GDN_ROWS = 256
CHUNKS_PER_STEP = GDN_ROWS // CHUNK
LOG2_CHUNK = 6

D_FF_PAD = 5632
PROJ_SUB = 256

TILES = dict(
    proj_tm=1024, proj_tn=1024,
    gate_rows=512,
    pool_rows=512,
    merge_tm=1024, merge_tn=512,
    res_tm=512,
    q_tm=1024, q_tn=1024,
    xattn_ts=1024,
    ffn_tm=1024, ffn_tn=512,
    down_tk=1408,
)


def _cparams(sem):
    return pltpu.CompilerParams(dimension_semantics=sem, vmem_limit_bytes=VMEM_LIMIT)


def _rms(xf, w):
    ms = jnp.mean(xf * xf, axis=-1, keepdims=True)
    return xf * lax.rsqrt(ms + EPS) * w


def _silu(x):
    return x * jax.nn.sigmoid(x)


def _bdot(a, b):
    return jnp.dot(a.astype(BF16), b.astype(BF16), preferred_element_type=F32)


def _causal_conv(u, prev, ubuf, cw):
    tm = u.shape[0]
    width = cw.shape[0]
    ubuf[0:SUBLANES, :] = prev
    ubuf[SUBLANES:SUBLANES + tm, :] = u
    y = u * cw[width - 1:width, :]
    for d in range(width - 1):
        off = SUBLANES - (width - 1) + d
        y = y + ubuf[off:off + tm, :] * cw[d:d + 1, :]
    return y


def _qkv_proj_kernel(x_ref, nw_ref, w_ref, cw_ref, o_ref, h_ref, ubuf, carry, *, tiles_per_seq,
                     tiles_per_part):
    i = pl.program_id(0)
    j = pl.program_id(1)
    tm, tn = o_ref.shape

    @pl.when(j == 0)
    def _():
        h_ref[...] = _rms(x_ref[...], nw_ref[...]).astype(BF16)

    is_qk = j < 2 * tiles_per_part
    scale = jnp.where(j < tiles_per_part, GDN_DK ** -0.5, 1.0)
    first = i % tiles_per_seq == 0
    for c in range(tn // PROJ_SUB):
        cs = slice(c * PROJ_SUB, (c + 1) * PROJ_SUB)
        u = jnp.dot(h_ref[...], w_ref[:, cs], preferred_element_type=F32)
        prev = jnp.where(first, 0.0, carry[j, :, cs])
        carry[j, :, cs] = u[tm - SUBLANES:tm, :]
        y = _silu(_causal_conv(u, prev, ubuf.at[:, cs], cw_ref[:, cs]))
        for hb in range(PROJ_SUB // GDN_DK):
            yh = y[:, hb * GDN_DK:(hb + 1) * GDN_DK]
            ss = jnp.sum(yh * yh, axis=-1, keepdims=True)
            f = jnp.where(is_qk, lax.rsqrt(ss + EPS) * scale, 1.0)
            o_ref[:, c * PROJ_SUB + hb * GDN_DK:c * PROJ_SUB + (hb + 1) * GDN_DK] = (
                yh * f).astype(o_ref.dtype)


def _qkv_proj(x, nw, w, conv_w, seq):
    m, k = x.shape
    n = w.shape[1]
    tm, tn = TILES["proj_tm"], TILES["proj_tn"]
    return pl.pallas_call(
        functools.partial(_qkv_proj_kernel, tiles_per_seq=seq // tm, tiles_per_part=QK_W // tn),
        grid=(m // tm, n // tn),
        in_specs=[
            pl.BlockSpec((tm, k), lambda i, j: (i, 0)),
            pl.BlockSpec((1, k), lambda i, j: (0, 0)),
            pl.BlockSpec((k, tn), lambda i, j: (0, j)),
            pl.BlockSpec((GDN_CONV, tn), lambda i, j: (0, j)),
        ],
        out_specs=pl.BlockSpec((tm, tn), lambda i, j: (i, j)),
        out_shape=jax.ShapeDtypeStruct((m, n), BF16),
        scratch_shapes=[
            pltpu.VMEM((tm, k), BF16),
            pltpu.VMEM((tm + SUBLANES, tn), F32),
            pltpu.VMEM((n // tn, SUBLANES, tn), F32),
        ],
        compiler_params=_cparams(("arbitrary", "arbitrary")),
        name="qkv_proj",
    )(x, nw.reshape(1, k), w, conv_w)


def _gate_proj_kernel(x_ref, nw_ref, w_ref, o_ref, h_ref, *, z_tiles, sig_tiles):
    j = pl.program_id(1)

    @pl.when(j == 0)
    def _():
        h_ref[...] = _rms(x_ref[...], nw_ref[...]).astype(BF16)

    is_z = j < z_tiles
    is_gate = j < z_tiles + sig_tiles
    for c in range(o_ref.shape[1] // PROJ_SUB):
        cs = slice(c * PROJ_SUB, (c + 1) * PROJ_SUB)
        u = jnp.dot(h_ref[...], w_ref[:, cs], preferred_element_type=F32)
        sg = jax.nn.sigmoid(u)
        out = jnp.where(is_z, u * sg, jnp.where(is_gate, sg, u))
        o_ref[:, cs] = out.astype(o_ref.dtype)


def _gate_proj(x, nw, w):
    m, k = x.shape
    n = w.shape[1]
    tm, tn = TILES["proj_tm"], TILES["proj_tn"]
    return pl.pallas_call(
        functools.partial(_gate_proj_kernel, z_tiles=V_W // tn, sig_tiles=2 * D_MODEL // tn),
        grid=(m // tm, n // tn),
        in_specs=[
            pl.BlockSpec((tm, k), lambda i, j: (i, 0)),
            pl.BlockSpec((1, k), lambda i, j: (0, 0)),
            pl.BlockSpec((k, tn), lambda i, j: (0, j)),
        ],
        out_specs=pl.BlockSpec((tm, tn), lambda i, j: (i, j)),
        out_shape=jax.ShapeDtypeStruct((m, n), BF16),
        scratch_shapes=[pltpu.VMEM((tm, k), BF16)],
        compiler_params=_cparams(("parallel", "arbitrary")),
        name="gate_proj",
    )(x, nw.reshape(1, k), w)


def _norm_mm_kernel(x_ref, nw_ref, w_ref, o_ref, h_ref):
    @pl.when(pl.program_id(1) == 0)
    def _():
        h_ref[...] = _rms(x_ref[...], nw_ref[...]).astype(BF16)

    o_ref[...] = jnp.dot(h_ref[...], w_ref[...], preferred_element_type=F32).astype(o_ref.dtype)


def _norm_mm(x, nw, w, *, tm, tn, out_dtype):
    m, k = x.shape
    n = w.shape[1]
    return pl.pallas_call(
        _norm_mm_kernel,
        grid=(m // tm, n // tn),
        in_specs=[
            pl.BlockSpec((tm, k), lambda i, j: (i, 0)),
            pl.BlockSpec((1, k), lambda i, j: (0, 0)),
            pl.BlockSpec((k, tn), lambda i, j: (0, j)),
        ],
        out_specs=pl.BlockSpec((tm, tn), lambda i, j: (i, j)),
        out_shape=jax.ShapeDtypeStruct((m, n), out_dtype),
        scratch_shapes=[pltpu.VMEM((tm, k), BF16)],
        compiler_params=_cparams(("parallel", "arbitrary")),
        name="norm_mm",
    )(x, nw.reshape(1, k), w)


def _mm_kernel(x_ref, w_ref, o_ref):
    o_ref[...] = jnp.dot(x_ref[...], w_ref[...], preferred_element_type=F32).astype(o_ref.dtype)


def _mm(x, w, *, tm, tn, out_dtype):
    m, k = x.shape
    n = w.shape[1]
    return pl.pallas_call(
        _mm_kernel,
        grid=(m // tm, n // tn),
        in_specs=[
            pl.BlockSpec((tm, k), lambda i, j: (i, 0)),
            pl.BlockSpec((k, tn), lambda i, j: (0, j)),
        ],
        out_specs=pl.BlockSpec((tm, tn), lambda i, j: (i, j)),
        out_shape=jax.ShapeDtypeStruct((m, n), out_dtype),
        compiler_params=_cparams(("parallel", "arbitrary")),
        name="mm",
    )(x, w)


def _mm_norm_res_kernel(a_ref, w_ref, res_ref, pw_ref, *rest, with_next):
    if with_next:
        nw_ref, x_ref, h_ref, acc_ref = rest
    else:
        x_ref, acc_ref = rest
    kk = pl.program_id(1)

    @pl.when(kk == 0)
    def _():
        acc_ref[...] = jnp.zeros_like(acc_ref)

    acc_ref[...] += jnp.dot(a_ref[...], w_ref[...], preferred_element_type=F32)

    @pl.when(kk == pl.num_programs(1) - 1)
    def _():
        xn = res_ref[...] + _rms(acc_ref[...], pw_ref[...])
        x_ref[...] = xn
        if with_next:
            h_ref[...] = _rms(xn, nw_ref[...]).astype(BF16)


def _mm_norm_res(a, w, res, post_w, next_w, *, tk):
    m, k = a.shape
    n = w.shape[1]
    tm = TILES["res_tm"]
    with_next = next_w is not None
    row = pl.BlockSpec((tm, n), lambda i, j: (i, 0))
    vec = pl.BlockSpec((1, n), lambda i, j: (0, 0))
    in_specs = [pl.BlockSpec((tm, tk), lambda i, j: (i, j)), pl.BlockSpec((tk, n), lambda i, j: (j, 0)),
                row, vec]
    args = [a, w, res, post_w.reshape(1, n)]
    out_specs = [row]
    out_shape = [jax.ShapeDtypeStruct((m, n), F32)]
    if with_next:
        in_specs.append(vec)
        args.append(next_w.reshape(1, n))
        out_specs.append(row)
        out_shape.append(jax.ShapeDtypeStruct((m, n), BF16))
    outs = pl.pallas_call(
        functools.partial(_mm_norm_res_kernel, with_next=with_next),
        grid=(m // tm, k // tk),
        in_specs=in_specs,
        out_specs=out_specs,
        out_shape=out_shape,
        scratch_shapes=[pltpu.VMEM((tm, n), F32)],
        compiler_params=_cparams(("parallel", "arbitrary")),
        name="mm_norm_res",
    )(*args)
    return outs if with_next else (outs[0], None)


def _gates_kernel(x_ref, nw_ref, w_ref, alog_ref, dtb_ref, o_ref):
    rows = x_ref.shape[0]
    hn = _rms(x_ref[...], nw_ref[...]).astype(BF16)
    ba = jnp.dot(hn, w_ref[...], preferred_element_type=F32)
    beta = jax.nn.sigmoid(ba)
    xs = ba + dtb_ref[...]
    softplus = jnp.maximum(xs, 0.0) + jnp.log1p(jnp.exp(-jnp.abs(xs)))
    g = -jnp.exp(alog_ref[...]) * softplus
    ri = lax.broadcasted_iota(jnp.int32, (rows, rows), 0)
    ci = lax.broadcasted_iota(jnp.int32, (rows, rows), 1)
    same = (ri >> LOG2_CHUNK) == (ci >> LOG2_CHUNK)
    tri = jnp.where(same & (ri >= ci), 1.0, 0.0).astype(F32)
    ones = jnp.where(same, 1.0, 0.0).astype(F32)
    gc = jnp.dot(tri, g, preferred_element_type=F32, precision=lax.Precision.HIGHEST)
    gl = jnp.dot(ones, g, preferred_element_type=F32, precision=lax.Precision.HIGHEST)
    lane = lax.broadcasted_iota(jnp.int32, ba.shape, 1)
    h = GDN_HEADS
    out = jnp.where(lane < h, beta, gc)
    out = jnp.where(lane < 2 * h, out, pltpu.roll(gl - gc, h, axis=1))
    out = jnp.where(lane < 3 * h, out, pltpu.roll(jnp.exp(gl), 2 * h, axis=1))
    o_ref[...] = out


def _gates(x, nw, w_ba, alog_l, dtb_l):
    m, k = x.shape
    rows = TILES["gate_rows"]
    vec = pl.BlockSpec((1, LANES), lambda i: (0, 0))
    return pl.pallas_call(
        _gates_kernel,
        grid=(m // rows,),
        in_specs=[
            pl.BlockSpec((rows, k), lambda i: (i, 0)),
            pl.BlockSpec((1, k), lambda i: (0, 0)),
            pl.BlockSpec((k, LANES), lambda i: (0, 0)),
            vec, vec,
        ],
        out_specs=pl.BlockSpec((rows, LANES), lambda i: (i, 0)),
        out_shape=jax.ShapeDtypeStruct((m, LANES), F32),
        compiler_params=_cparams(("parallel",)),
        name="gdn_gates",
    )(x, nw.reshape(1, k), w_ba, alog_l, dtb_l)


def _gdn_kernel(cd_ref, q_ref, k_ref, v_ref, z_ref, colg_ref, rowg_ref, gn_ref, o_ref,
                lv_ref, s_ref, uv_buf, wq_buf, kd_buf, at_buf, ubuf, obuf, *, n_tiles):
    b = pl.program_id(0)
    hg = pl.program_id(1)
    si = pl.program_id(2)
    R = GDN_ROWS
    heads = range(HEADS_PER_GROUP)
    sls = [slice(hl * GDN_DK, (hl + 1) * GDN_DK) for hl in heads]
    wslot = si % 2
    rslot = 1 - wslot

    @pl.when(si == 0)
    def _():
        s_ref[...] = jnp.zeros_like(s_ref)
        uv_buf[1] = jnp.zeros(uv_buf.shape[1:], uv_buf.dtype)
        wq_buf[1] = jnp.zeros(wq_buf.shape[1:], wq_buf.dtype)
        kd_buf[1] = jnp.zeros(kd_buf.shape[1:], kd_buf.dtype)
        at_buf[1] = jnp.zeros(at_buf.shape[1:], at_buf.dtype)
        ri = lax.broadcasted_iota(jnp.int32, (R, R), 0)
        ci = lax.broadcasted_iota(jnp.int32, (R, R), 1)
        x = ri ^ ci
        lv = jnp.full((R, R), -1, jnp.int32)
        for t in range(LOG2_CHUNK):
            lv = lv + jnp.where(x >= (1 << t), 1, 0)
        ok = ((ri >> LOG2_CHUNK) == (ci >> LOG2_CHUNK)) & (ri >= ci)
        lv_ref[...] = jnp.where(ok, lv, -2)

    p1 = {}

    def p1_scores():
        p1["qk_kk"] = []
        for hl in heads:
            kb = k_ref[:, sls[hl]]
            p1["qk_kk"].append(lax.dot_general(jnp.concatenate([q_ref[:, sls[hl]], kb], axis=0), kb,
                                               (((1,), (1,)), ((), ())), preferred_element_type=F32))

    def p1_decay():
        lv = lv_ref[...]
        incl = lv >= -1
        colg = colg_ref[...]
        p1["beta"] = [colg[:, hl:hl + 1] for hl in heads]
        p1["exp_g"] = [jnp.exp(colg[:, 4 + hl:5 + hl]) for hl in heads]
        p1["low"], p1["xinv"] = [], []
        for hl in heads:
            gap = colg[:, 4 + hl:5 + hl] - rowg_ref[hl:hl + 1, :]
            decay = jnp.where(incl, jnp.exp(jnp.where(incl, gap, 0.0)), 0.0)
            at_buf[wslot, hl] = (decay * p1["qk_kk"][hl][0:R]).astype(BF16)
            lo = jnp.where(lv >= 0, p1["beta"][hl] * decay * p1["qk_kk"][hl][R:2 * R], 0.0)
            p1["low"].append(lo)
            p1["xinv"].append(jnp.where(lv == -1, 1.0, 0.0) - jnp.where(lv == 0, lo, 0.0))

    def p1_level_a(level):
        def run():
            lv = lv_ref[...]
            p1["ys"] = [_bdot(jnp.where(lv == level, p1["low"][hl], 0.0), p1["xinv"][hl]) for hl in heads]
        return run

    def p1_level_b():
        p1["xinv"] = [p1["xinv"][hl] - _bdot(p1["xinv"][hl], p1["ys"][hl]) for hl in heads]

    def p1_solve():
        p1["sol"] = []
        for hl in heads:
            bk = (p1["beta"][hl] * p1["exp_g"][hl]) * k_ref[:, sls[hl]].astype(F32)
            rhs = jnp.concatenate([p1["beta"][hl] * v_ref[:, sls[hl]].astype(F32), bk], axis=1)
            p1["sol"].append(_bdot(p1["xinv"][hl], rhs))

    def p1_store():
        colg = colg_ref[...]
        for hl in heads:
            q_dec = q_ref[:, sls[hl]].astype(F32) * p1["exp_g"][hl]
            uv_buf[wslot, hl] = p1["sol"][hl][:, 0:GDN_DV]
            wq_buf[wslot, hl, 0:R, :] = p1["sol"][hl][:, GDN_DV:].astype(BF16)
            wq_buf[wslot, hl, R:2 * R, :] = q_dec.astype(BF16)
            kd_buf[wslot, hl] = (k_ref[:, sls[hl]].astype(F32)
                                 * jnp.exp(colg[:, 8 + hl:9 + hl])).astype(BF16)

    p1_stages = [p1_scores, p1_decay]
    for level in range(1, LOG2_CHUNK):
        p1_stages += [p1_level_a(level), p1_level_b]
    p1_stages += [p1_solve, p1_store]

    p2 = {}
    tile = jnp.maximum(si - 1, 0)

    def p2_read(c):
        def run():
            p2["states"] = [s_ref[hl] for hl in heads]
            p2["wqs"] = []
            for hl in heads:
                lhs = jnp.concatenate([wq_buf[rslot, hl, c * CHUNK:(c + 1) * CHUNK, :],
                                       wq_buf[rslot, hl, R + c * CHUNK:R + (c + 1) * CHUNK, :]], axis=0)
                p2["wqs"].append(jnp.dot(lhs, p2["states"][hl].astype(BF16),
                                         preferred_element_type=F32))
        return run

    def p2_update(c):
        def run():
            rs = slice(c * CHUNK, (c + 1) * CHUNK)
            for hl in heads:
                u = (uv_buf[rslot, hl, rs, :] - p2["wqs"][hl][0:CHUNK]).astype(BF16)
                ubuf[hl, rs, :] = u
                obuf[hl, rs, :] = p2["wqs"][hl][CHUNK:]
                cd = cd_ref[b * GDN_HEADS + hg * HEADS_PER_GROUP + hl, tile * CHUNKS_PER_STEP + c]
                upd = lax.dot_general(kd_buf[rslot, hl, rs, :], u, (((0,), (0,)), ((), ())),
                                      preferred_element_type=F32)
                s_ref[hl] = cd * p2["states"][hl] + upd
        return run

    def p2_output():
        for hl in heads:
            o = obuf[hl] + jnp.dot(at_buf[rslot, hl], ubuf[hl], preferred_element_type=F32)
            on = _rms(o, gn_ref[...]) * z_ref[:, sls[hl]].astype(F32)
            o_ref[:, sls[hl]] = on.astype(o_ref.dtype)

    p2_stages = []
    for c in range(CHUNKS_PER_STEP):
        p2_stages += [p2_read(c), p2_update(c)]
    p2_stages.append(p2_output)

    n1, n2 = len(p1_stages), len(p2_stages)
    i2 = 0
    for i1, stage in enumerate(p1_stages):
        stage()
        while i2 < n2 and (i2 + 1) * n1 <= (i1 + 1) * n2:
            p2_stages[i2]()
            i2 += 1
    while i2 < n2:
        p2_stages[i2]()
        i2 += 1


def _gdn(qkv, gate, colg4, rowg4, cdecay, gdn_norm, batch, seq):
    m = qkv.shape[0]
    ns = seq // GDN_ROWS
    gw = GROUP_W
    R = GDN_ROWS

    def cur(s):
        return jnp.minimum(s, ns - 1)

    def prv(s):
        return jnp.maximum(s - 1, 0)

    def xspec(col0):
        return pl.BlockSpec((R, gw), lambda b, h, s: (b * ns + cur(s), col0 // gw + h))

    hp = HEADS_PER_GROUP
    return pl.pallas_call(
        functools.partial(_gdn_kernel, n_tiles=ns),
        grid=(batch, HEAD_GROUPS, ns + 1),
        in_specs=[
            pl.BlockSpec(memory_space=pltpu.SMEM),
            xspec(0), xspec(QK_W), xspec(2 * QK_W),
            pl.BlockSpec((R, gw), lambda b, h, s: (b * ns + prv(s), GCOL_Z // gw + h)),
            pl.BlockSpec((None, R, LANES), lambda b, h, s: (h, b * ns + cur(s), 0)),
            pl.BlockSpec((None, None, hp, R), lambda b, h, s: (b, h, 0, cur(s))),
            pl.BlockSpec((1, GDN_DV), lambda b, h, s: (0, 0)),
        ],
        out_specs=pl.BlockSpec((R, gw), lambda b, h, s: (b * ns + prv(s), h)),
        out_shape=jax.ShapeDtypeStruct((m, V_W), BF16),
        scratch_shapes=[
            pltpu.VMEM((R, R), jnp.int32),
            pltpu.VMEM((hp, GDN_DK, GDN_DV), F32),
            pltpu.VMEM((2, hp, R, GDN_DV), F32),
            pltpu.VMEM((2, hp, 2 * R, GDN_DK), BF16),
            pltpu.VMEM((2, hp, R, GDN_DK), BF16),
            pltpu.VMEM((2, hp, R, R), BF16),
            pltpu.VMEM((hp, R, GDN_DV), BF16),
            pltpu.VMEM((hp, R, GDN_DV), F32),
        ],
        compiler_params=_cparams(("parallel", "parallel", "arbitrary")),
        name="gdn",
    )(cdecay, qkv, qkv, qkv, gate, colg4, rowg4, gdn_norm.reshape(1, GDN_DV))


POOL_HALO = 16


def _pool_kernel(p_ref, w_ref, sc_ref, o_ref, buf):
    si = pl.program_id(1)
    R = p_ref.shape[0]

    @pl.when(si == 0)
    def _():
        buf[0:POOL_HALO, :] = jnp.zeros((POOL_HALO, POOL_WIDTH), F32)

    buf[POOL_HALO:POOL_HALO + R, :] = p_ref[...].astype(F32)
    t = si * R + lax.broadcasted_iota(jnp.int32, (R, 1), 0)
    for gi, win in enumerate(POOL_WINDOWS):
        sl = slice(gi * POOL_GROUP, (gi + 1) * POOL_GROUP)
        x = buf[POOL_HALO:POOL_HALO + R, sl]
        acc = x
        for d in range(1, win):
            acc = acc + buf[POOL_HALO - d:POOL_HALO - d + R, sl]
        cnt = jnp.minimum(t + 1, win).astype(F32)
        y = acc / cnt - x
        o = _bdot(y, w_ref[gi]) * sc_ref[:, sl]
        o_ref[:, sl] = o.astype(o_ref.dtype)
    buf[0:POOL_HALO, :] = buf[R:R + POOL_HALO, :]


def _pool(gate, pool_w, pool_scale, batch, seq):
    m = gate.shape[0]
    rows = TILES["pool_rows"]
    ns = seq // rows
    return pl.pallas_call(
        _pool_kernel,
        grid=(batch, ns),
        in_specs=[
            pl.BlockSpec((rows, POOL_WIDTH), lambda b, s: (b * ns + s, GCOL_P // POOL_WIDTH)),
            pl.BlockSpec((len(POOL_WINDOWS), POOL_GROUP, POOL_GROUP), lambda b, s: (0, 0, 0)),
            pl.BlockSpec((1, POOL_WIDTH), lambda b, s: (0, 0)),
        ],
        out_specs=pl.BlockSpec((rows, POOL_WIDTH), lambda b, s: (b * ns + s, 0)),
        out_shape=jax.ShapeDtypeStruct((m, POOL_WIDTH), BF16),
        scratch_shapes=[pltpu.VMEM((rows + POOL_HALO, POOL_WIDTH), F32)],
        compiler_params=_cparams(("parallel", "arbitrary")),
        name="pool",
    )(gate, pool_w, pool_scale.reshape(1, POOL_WIDTH))


def _merge_kernel(a_ref, p_ref, wa_ref, wb_ref, ga_ref, gb_ref, o_ref):
    ya = jnp.dot(a_ref[...], wa_ref[...], preferred_element_type=F32)
    yb = jnp.dot(p_ref[...], wb_ref[...], preferred_element_type=F32)
    merged = ga_ref[...].astype(F32) * ya + gb_ref[...].astype(F32) * yb
    o_ref[...] = merged.astype(o_ref.dtype)


def _merge(og, pooled, wa, wb, gate):
    m = og.shape[0]
    n = wa.shape[1]
    tm, tn = TILES["merge_tm"], TILES["merge_tn"]
    return pl.pallas_call(
        _merge_kernel,
        grid=(m // tm, n // tn),
        in_specs=[
            pl.BlockSpec((tm, V_W), lambda i, j: (i, 0)),
            pl.BlockSpec((tm, POOL_WIDTH), lambda i, j: (i, 0)),
            pl.BlockSpec((V_W, tn), lambda i, j: (0, j)),
            pl.BlockSpec((POOL_WIDTH, tn), lambda i, j: (0, j)),
            pl.BlockSpec((tm, tn), lambda i, j: (i, GCOL_GA // tn + j)),
            pl.BlockSpec((tm, tn), lambda i, j: (i, GCOL_GB // tn + j)),
        ],
        out_specs=pl.BlockSpec((tm, tn), lambda i, j: (i, j)),
        out_shape=jax.ShapeDtypeStruct((m, n), BF16),
        compiler_params=_cparams(("parallel", "arbitrary")),
        name="merge",
    )(og, pooled, wa, wb, gate, gate)


def _xattn_kernel(q_ref, k_ref, v_ref, o_ref):
    s = lax.dot_general(q_ref[...], k_ref[...], (((1,), (1,)), ((), ())),
                        preferred_element_type=F32) * (XA_HEAD_DIM ** -0.5)
    s = s - jnp.max(s, axis=-1, keepdims=True)
    e = jnp.exp(s)
    p = e / jnp.sum(e, axis=-1, keepdims=True)
    o_ref[...] = jnp.dot(p.astype(BF16), v_ref[...], preferred_element_type=F32).astype(o_ref.dtype)


def _xattn(q, kv, batch, seq, mem_len):
    m = q.shape[0]
    ts = TILES["xattn_ts"]
    ns = seq // ts
    return pl.pallas_call(
        _xattn_kernel,
        grid=(batch, ns, XA_HEADS),
        in_specs=[
            pl.BlockSpec((ts, XA_HEAD_DIM), lambda b, s, h: (b * ns + s, h)),
            pl.BlockSpec((mem_len, XA_HEAD_DIM), lambda b, s, h: (b, h)),
            pl.BlockSpec((mem_len, XA_HEAD_DIM), lambda b, s, h: (b, XA_HEADS + h)),
        ],
        out_specs=pl.BlockSpec((ts, XA_HEAD_DIM), lambda b, s, h: (b * ns + s, h)),
        out_shape=jax.ShapeDtypeStruct((m, D_MODEL), BF16),
        compiler_params=_cparams(("parallel", "parallel", "arbitrary")),
        name="xattn",
    )(q, kv, kv)


def _ffn_up_kernel(h_ref, wa_ref, wb_ref, cwa_ref, cwb_ref, ba_ref, bb_ref, o_ref, ubuf, carry,
                   *, tiles_per_seq):
    i = pl.program_id(0)
    j = pl.program_id(1)
    tm = h_ref.shape[0]
    halves = []
    for t, (w_ref, cw_ref, b_ref) in enumerate(((wa_ref, cwa_ref, ba_ref), (wb_ref, cwb_ref, bb_ref))):
        u = jnp.dot(h_ref[...], w_ref[...], preferred_element_type=F32)
        prev = jnp.where(i % tiles_per_seq == 0, 0.0, carry[t, j])
        carry[t, j] = u[tm - SUBLANES:tm, :]
        halves.append(_causal_conv(u, prev, ubuf.at[t], cw_ref[...]) + b_ref[...])
    o_ref[...] = (_silu(halves[0]) * halves[1]).astype(o_ref.dtype)


def _ffn_up(h, wa, wb, cwa, cwb, ba, bb, seq):
    m, k = h.shape
    n = wa.shape[1]
    tm, tn = TILES["ffn_tm"], TILES["ffn_tn"]
    wspec = pl.BlockSpec((k, tn), lambda i, j: (0, j))
    cspec = pl.BlockSpec((FFN_CONV, tn), lambda i, j: (0, j))
    bspec = pl.BlockSpec((1, tn), lambda i, j: (0, j))
    return pl.pallas_call(
        functools.partial(_ffn_up_kernel, tiles_per_seq=seq // tm),
        grid=(m // tm, n // tn),
        in_specs=[pl.BlockSpec((tm, k), lambda i, j: (i, 0)), wspec, wspec, cspec, cspec, bspec, bspec],
        out_specs=pl.BlockSpec((tm, tn), lambda i, j: (i, j)),
        out_shape=jax.ShapeDtypeStruct((m, n), BF16),
        scratch_shapes=[
            pltpu.VMEM((2, tm + SUBLANES, tn), F32),
            pltpu.VMEM((2, n // tn, SUBLANES, tn), F32),
        ],
        compiler_params=_cparams(("arbitrary", "arbitrary")),
        name="ffn_up",
    )(h, wa, wb, cwa, cwb, ba, bb)


def _pad_cols(a, n):
    return jnp.pad(a, ((0, 0), (0, n - a.shape[1])))


def _layer(x2d, mem2d, batch, seq, mem_len, p):
    m = x2d.shape[0]
    h = GDN_HEADS
    hp = HEADS_PER_GROUP
    w_in = p["w_in"]
    o_z = QKV_W
    o_b = o_z + V_W
    o_p = o_b + 2 * h
    o_ga = o_p + POOL_WIDTH
    o_gb = o_ga + D_MODEL
    w_qkv = w_in[:, :o_z].astype(BF16)
    w_gate = jnp.concatenate([w_in[:, o_z:o_b], w_in[:, o_ga:], w_in[:, o_p:o_ga]], axis=1).astype(BF16)
    w_ba = _pad_cols(w_in[:, o_b:o_p], LANES).astype(BF16)
    alog_l = jnp.zeros((1, LANES), F32).at[0, h:2 * h].set(p["a_log"])
    dtb_l = jnp.zeros((1, LANES), F32).at[0, h:2 * h].set(p["dt_bias"])
    w_up = p["w_up"]
    wa_up = _pad_cols(w_up[:, :D_FF], D_FF_PAD).astype(BF16)
    wb_up = _pad_cols(w_up[:, D_FF:], D_FF_PAD).astype(BF16)
    cwa = _pad_cols(p["ffn_conv_w"][:, :D_FF], D_FF_PAD)
    cwb = _pad_cols(p["ffn_conv_w"][:, D_FF:], D_FF_PAD)
    fba = _pad_cols(p["ffn_conv_b"][None, :D_FF], D_FF_PAD)
    fbb = _pad_cols(p["ffn_conv_b"][None, D_FF:], D_FF_PAD)
    w_down = jnp.pad(p["w_down"], ((0, D_FF_PAD - D_FF), (0, 0))).astype(BF16)

    qkv = _qkv_proj(x2d, p["mix_pre_norm"], w_qkv, p["conv_qkv"], seq)
    gate = _gate_proj(x2d, p["mix_pre_norm"], w_gate)
    gates = _gates(x2d, p["mix_pre_norm"], w_ba, alog_l, dtb_l)
    g4 = gates[:, :4 * h].reshape(m, 4, HEAD_GROUPS, hp)
    colg4 = jnp.pad(g4[:, :3].transpose(2, 0, 1, 3).reshape(HEAD_GROUPS, m, 3 * hp),
                    ((0, 0), (0, 0), (0, LANES - 3 * hp)))
    rowg4 = gates[:, h:2 * h].reshape(batch, seq, HEAD_GROUPS, hp).transpose(0, 2, 3, 1)
    cdecay = gates[CHUNK - 1::CHUNK, 3 * h:4 * h].reshape(batch, seq // CHUNK, h)
    cdecay = cdecay.transpose(0, 2, 1).reshape(batch * h, seq // CHUNK)
    og = _gdn(qkv, gate, colg4, rowg4, cdecay, p["gdn_norm"], batch, seq)
    pooled = _pool(gate, p["pool_w"].astype(BF16), p["pool_scale"], batch, seq)
    merged = _merge(og, pooled, p["w_branch_a"].astype(BF16), p["w_branch_b"].astype(BF16), gate)
    x1, h2 = _mm_norm_res(merged, p["w_mix_out"].astype(BF16), x2d, p["mix_post_norm"],
                          p["xa_pre_norm"], tk=D_MODEL)
    kv = _norm_mm(mem2d, p["mem_norm"], p["w_xkv"].astype(BF16), tm=mem2d.shape[0], tn=1024,
                  out_dtype=BF16)
    q = _mm(h2, p["w_xq"].astype(BF16), tm=TILES["q_tm"], tn=TILES["q_tn"], out_dtype=BF16)
    xo = _xattn(q, kv, batch, seq, mem_len)
    x2, h3 = _mm_norm_res(xo, p["w_xo"].astype(BF16), x1, p["xa_post_norm"], p["ffn_pre_norm"],
                          tk=D_MODEL)
    act = _ffn_up(h3, wa_up, wb_up, cwa, cwb, fba, fbb, seq)
    x3, _ = _mm_norm_res(act, w_down, x2, p["ffn_post_norm"], None, tk=TILES["down_tk"])
    return x3


def kernel(x, mem, mix_pre_norm, w_in, conv_qkv, a_log, dt_bias, gdn_norm, pool_w, pool_scale,
           w_branch_a, w_branch_b, w_mix_out, mix_post_norm, xa_pre_norm, mem_norm, w_xq, w_xkv,
           w_xo, xa_post_norm, ffn_pre_norm, w_up, ffn_conv_w, ffn_conv_b, w_down, ffn_post_norm):
    params = dict(mix_pre_norm=mix_pre_norm, w_in=w_in, conv_qkv=conv_qkv, a_log=a_log,
                  dt_bias=dt_bias, gdn_norm=gdn_norm, pool_w=pool_w, pool_scale=pool_scale,
                  w_branch_a=w_branch_a, w_branch_b=w_branch_b, w_mix_out=w_mix_out,
                  mix_post_norm=mix_post_norm, xa_pre_norm=xa_pre_norm, mem_norm=mem_norm,
                  w_xq=w_xq, w_xkv=w_xkv, w_xo=w_xo, xa_post_norm=xa_post_norm,
                  ffn_pre_norm=ffn_pre_norm, w_up=w_up, ffn_conv_w=ffn_conv_w,
                  ffn_conv_b=ffn_conv_b, w_down=w_down, ffn_post_norm=ffn_post_norm)
    batch, seq, d = x.shape
    mem_len = mem.shape[1]
    x2d = x.reshape(batch * seq, d)
    mem2d = mem.reshape(batch * mem_len, d)
    for l in range(w_in.shape[0]):
        x2d = _layer(x2d, mem2d, batch, seq, mem_len, {k: v[l] for k, v in params.items()})
    return x2d.reshape(batch, seq, d)
```

```python
import functools

import jax
import jax.numpy as jnp
from jax import lax
from jax.experimental import pallas as pl
from jax.experimental.pallas import tpu as pltpu

F32 = jnp.float32
BF16 = jnp.bfloat16

D_MODEL = 2048
CHUNK = 64
GDN_HEADS = 16
GDN_DK = 128
GDN_DV = 128
GDN_CONV = 4
QK_W = GDN_HEADS * GDN_DK
V_W = GDN_HEADS * GDN_DV
QKV_W = 2 * QK_W + V_W
POOL_WINDOWS = (2, 4, 8, 16)
POOL_WIDTH = D_MODEL // 2
POOL_GROUP = POOL_WIDTH // 4
XA_HEADS = 4
XA_HEAD_DIM = D_MODEL // XA_HEADS
D_FF = 5504
FFN_CONV = 3
EPS = 1e-6

LANES = 128
SUBLANES = 8
VMEM_LIMIT = 56 * 1024 * 1024

GCOL_Z = 0
GCOL_GA = V_W
GCOL_GB = GCOL_GA + D_MODEL
GCOL_P = GCOL_GB + D_MODEL
GATE_W = GCOL_P + POOL_WIDTH

HEADS_PER_GROUP = 4
HEAD_GROUPS = GDN_HEADS // HEADS_PER_GROUP
GROUP_W = HEADS_PER_GROUP * GDN_DK
GDN_ROWS = 256
CHUNKS_PER_STEP = GDN_ROWS // CHUNK
LOG2_CHUNK = 6

D_FF_PAD = 5632
PROJ_SUB = 256

TILES = dict(
    proj_tm=1024, proj_tn=1024,
    gate_rows=512,
    pool_rows=512,
    merge_tm=1024, merge_tn=512,
    res_tm=512,
    q_tm=1024, q_tn=1024,
    xattn_ts=1024,
    ffn_tm=1024, ffn_tn=512,
    down_tm=256,
)


def _cparams(sem):
    return pltpu.CompilerParams(dimension_semantics=sem, vmem_limit_bytes=VMEM_LIMIT)


def _rms(xf, w):
    ms = jnp.mean(xf * xf, axis=-1, keepdims=True)
    return xf * lax.rsqrt(ms + EPS) * w


def _silu(x):
    return x * jax.nn.sigmoid(x)


def _bdot(a, b):
    return jnp.dot(a.astype(BF16), b.astype(BF16), preferred_element_type=F32)


def _causal_conv(u, prev, ubuf, cw):
    tm = u.shape[0]
    width = cw.shape[0]
    ubuf[0:SUBLANES, :] = prev
    ubuf[SUBLANES:SUBLANES + tm, :] = u
    y = u * cw[width - 1:width, :]
    for d in range(width - 1):
        off = SUBLANES - (width - 1) + d
        y = y + ubuf[off:off + tm, :] * cw[d:d + 1, :]
    return y


def _qkv_proj_kernel(x_ref, nw_ref, w_ref, cw_ref, o_ref, h_ref, ubuf, carry, *, tiles_per_seq,
                     tiles_per_part):
    i = pl.program_id(0)
    j = pl.program_id(1)
    tm, tn = o_ref.shape

    @pl.when(j == 0)
    def _():
        h_ref[...] = _rms(x_ref[...], nw_ref[...]).astype(BF16)

    is_qk = j < 2 * tiles_per_part
    scale = jnp.where(j < tiles_per_part, GDN_DK ** -0.5, 1.0)
    first = i % tiles_per_seq == 0
    for c in range(tn // PROJ_SUB):
        cs = slice(c * PROJ_SUB, (c + 1) * PROJ_SUB)
        u = jnp.dot(h_ref[...], w_ref[:, cs], preferred_element_type=F32)
        prev = jnp.where(first, 0.0, carry[j, :, cs])
        carry[j, :, cs] = u[tm - SUBLANES:tm, :]
        y = _silu(_causal_conv(u, prev, ubuf.at[:, cs], cw_ref[:, cs]))
        for hb in range(PROJ_SUB // GDN_DK):
            yh = y[:, hb * GDN_DK:(hb + 1) * GDN_DK]
            ss = jnp.sum(yh * yh, axis=-1, keepdims=True)
            f = jnp.where(is_qk, lax.rsqrt(ss + EPS) * scale, 1.0)
            o_ref[:, c * PROJ_SUB + hb * GDN_DK:c * PROJ_SUB + (hb + 1) * GDN_DK] = (
                yh * f).astype(o_ref.dtype)


def _qkv_proj(x, nw, w, conv_w, seq):
    m, k = x.shape
    n = QKV_W
    tm, tn = TILES["proj_tm"], TILES["proj_tn"]
    return pl.pallas_call(
        functools.partial(_qkv_proj_kernel, tiles_per_seq=seq // tm, tiles_per_part=QK_W // tn),
        grid=(m // tm, n // tn),
        in_specs=[
            pl.BlockSpec((tm, k), lambda i, j: (i, 0)),
            pl.BlockSpec((1, k), lambda i, j: (0, 0)),
            pl.BlockSpec((k, tn), lambda i, j: (0, j)),
            pl.BlockSpec((GDN_CONV, tn), lambda i, j: (0, j)),
        ],
        out_specs=pl.BlockSpec((tm, tn), lambda i, j: (i, j)),
        out_shape=jax.ShapeDtypeStruct((m, n), BF16),
        scratch_shapes=[
            pltpu.VMEM((tm, k), BF16),
            pltpu.VMEM((tm + SUBLANES, tn), F32),
            pltpu.VMEM((n // tn, SUBLANES, tn), F32),
        ],
        compiler_params=_cparams(("arbitrary", "arbitrary")),
        name="qkv_proj",
    )(x, nw.reshape(1, k), w, conv_w)


def _gate_proj_kernel(x_ref, nw_ref, w_ref, o_ref, h_ref, *, z_tiles, sig_tiles):
    j = pl.program_id(1)

    @pl.when(j == 0)
    def _():
        h_ref[...] = _rms(x_ref[...], nw_ref[...]).astype(BF16)

    is_z = j < z_tiles
    is_gate = j < z_tiles + sig_tiles
    for c in range(o_ref.shape[1] // PROJ_SUB):
        cs = slice(c * PROJ_SUB, (c + 1) * PROJ_SUB)
        u = jnp.dot(h_ref[...], w_ref[:, cs], preferred_element_type=F32)
        sg = jax.nn.sigmoid(u)
        out = jnp.where(is_z, u * sg, jnp.where(is_gate, sg, u))
        o_ref[:, cs] = out.astype(o_ref.dtype)


def _gate_proj(x, nw, w):
    m, k = x.shape
    n = w.shape[1]
    tm, tn = TILES["proj_tm"], TILES["proj_tn"]
    return pl.pallas_call(
        functools.partial(_gate_proj_kernel, z_tiles=V_W // tn, sig_tiles=2 * D_MODEL // tn),
        grid=(m // tm, n // tn),
        in_specs=[
            pl.BlockSpec((tm, k), lambda i, j: (i, 0)),
            pl.BlockSpec((1, k), lambda i, j: (0, 0)),
            pl.BlockSpec((k, tn), lambda i, j: (0, j)),
        ],
        out_specs=pl.BlockSpec((tm, tn), lambda i, j: (i, j)),
        out_shape=jax.ShapeDtypeStruct((m, n), BF16),
        scratch_shapes=[pltpu.VMEM((tm, k), BF16)],
        compiler_params=_cparams(("parallel", "arbitrary")),
        name="gate_proj",
    )(x, nw.reshape(1, k), w)


def _norm_mm_kernel(x_ref, nw_ref, w_ref, o_ref, h_ref):
    @pl.when(pl.program_id(1) == 0)
    def _():
        h_ref[...] = _rms(x_ref[...], nw_ref[...]).astype(BF16)

    o_ref[...] = jnp.dot(h_ref[...], w_ref[...], preferred_element_type=F32).astype(o_ref.dtype)


def _norm_mm(x, nw, w, *, tm, tn, out_dtype):
    m, k = x.shape
    n = w.shape[1]
    return pl.pallas_call(
        _norm_mm_kernel,
        grid=(m // tm, n // tn),
        in_specs=[
            pl.BlockSpec((tm, k), lambda i, j: (i, 0)),
            pl.BlockSpec((1, k), lambda i, j: (0, 0)),
            pl.BlockSpec((k, tn), lambda i, j: (0, j)),
        ],
        out_specs=pl.BlockSpec((tm, tn), lambda i, j: (i, j)),
        out_shape=jax.ShapeDtypeStruct((m, n), out_dtype),
        scratch_shapes=[pltpu.VMEM((tm, k), BF16)],
        compiler_params=_cparams(("parallel", "arbitrary")),
        name="norm_mm",
    )(x, nw.reshape(1, k), w)


def _mm_kernel(x_ref, w_ref, o_ref):
    o_ref[...] = jnp.dot(x_ref[...], w_ref[...], preferred_element_type=F32).astype(o_ref.dtype)


def _mm(x, w, *, tm, tn, out_dtype):
    m, k = x.shape
    n = w.shape[1]
    return pl.pallas_call(
        _mm_kernel,
        grid=(m // tm, n // tn),
        in_specs=[
            pl.BlockSpec((tm, k), lambda i, j: (i, 0)),
            pl.BlockSpec((k, tn), lambda i, j: (0, j)),
        ],
        out_specs=pl.BlockSpec((tm, tn), lambda i, j: (i, j)),
        out_shape=jax.ShapeDtypeStruct((m, n), out_dtype),
        compiler_params=_cparams(("parallel", "arbitrary")),
        name="mm",
    )(x, w)


def _mm_norm_res_kernel(a_ref, w_ref, res_ref, pw_ref, *rest, with_next):
    if with_next:
        nw_ref, x_ref, h_ref = rest
    else:
        (x_ref,) = rest
    y = jnp.dot(a_ref[...], w_ref[...], preferred_element_type=F32)
    xn = res_ref[...] + _rms(y, pw_ref[...])
    x_ref[...] = xn
    if with_next:
        h_ref[...] = _rms(xn, nw_ref[...]).astype(BF16)


def _mm_norm_res(a, w, res, post_w, next_w, *, tm):
    m, k = a.shape
    n = w.shape[1]
    with_next = next_w is not None
    row = pl.BlockSpec((tm, n), lambda i: (i, 0))
    vec = pl.BlockSpec((1, n), lambda i: (0, 0))
    wspec = pl.BlockSpec((k, n), lambda i: (0, 0), pipeline_mode=pl.Buffered(1))
    in_specs = [pl.BlockSpec((tm, k), lambda i: (i, 0)), wspec, row, vec]
    args = [a, w, res, post_w.reshape(1, n)]
    out_specs = [row]
    out_shape = [jax.ShapeDtypeStruct((m, n), F32)]
    if with_next:
        in_specs.append(vec)
        args.append(next_w.reshape(1, n))
        out_specs.append(row)
        out_shape.append(jax.ShapeDtypeStruct((m, n), BF16))
    outs = pl.pallas_call(
        functools.partial(_mm_norm_res_kernel, with_next=with_next),
        grid=(m // tm,),
        in_specs=in_specs,
        out_specs=out_specs,
        out_shape=out_shape,
        compiler_params=_cparams(("parallel",)),
        name="mm_norm_res",
    )(*args)
    return outs if with_next else (outs[0], None)


def _gates_kernel(x_ref, nw_ref, w_ref, alog_ref, dtb_ref, o_ref):
    rows = x_ref.shape[0]
    hn = _rms(x_ref[...], nw_ref[...]).astype(BF16)
    ba = jnp.dot(hn, w_ref[...], preferred_element_type=F32)
    beta = jax.nn.sigmoid(ba)
    xs = ba + dtb_ref[...]
    softplus = jnp.maximum(xs, 0.0) + jnp.log1p(jnp.exp(-jnp.abs(xs)))
    g = -jnp.exp(alog_ref[...]) * softplus
    ri = lax.broadcasted_iota(jnp.int32, (rows, rows), 0)
    ci = lax.broadcasted_iota(jnp.int32, (rows, rows), 1)
    same = (ri >> LOG2_CHUNK) == (ci >> LOG2_CHUNK)
    tri = jnp.where(same & (ri >= ci), 1.0, 0.0).astype(F32)
    ones = jnp.where(same, 1.0, 0.0).astype(F32)
    gc = jnp.dot(tri, g, preferred_element_type=F32, precision=lax.Precision.HIGHEST)
    gl = jnp.dot(ones, g, preferred_element_type=F32, precision=lax.Precision.HIGHEST)
    lane = lax.broadcasted_iota(jnp.int32, ba.shape, 1)
    h = GDN_HEADS
    out = jnp.where(lane < h, beta, gc)
    out = jnp.where(lane < 2 * h, out, pltpu.roll(gl - gc, h, axis=1))
    out = jnp.where(lane < 3 * h, out, pltpu.roll(jnp.exp(gl), 2 * h, axis=1))
    o_ref[...] = out


def _gates(x, nw, w_ba, alog_l, dtb_l):
    m, k = x.shape
    rows = TILES["gate_rows"]
    vec = pl.BlockSpec((1, LANES), lambda i: (0, 0))
    return pl.pallas_call(
        _gates_kernel,
        grid=(m // rows,),
        in_specs=[
            pl.BlockSpec((rows, k), lambda i: (i, 0)),
            pl.BlockSpec((1, k), lambda i: (0, 0)),
            pl.BlockSpec((k, LANES), lambda i: (0, 0)),
            vec, vec,
        ],
        out_specs=pl.BlockSpec((rows, LANES), lambda i: (i, 0)),
        out_shape=jax.ShapeDtypeStruct((m, LANES), F32),
        compiler_params=_cparams(("parallel",)),
        name="gdn_gates",
    )(x, nw.reshape(1, k), w_ba, alog_l, dtb_l)


def _gdn_kernel(cd_ref, q_ref, k_ref, v_ref, z_ref, colg_ref, rowg_ref, gn_ref, o_ref,
                lv_ref, s_ref, uv_buf, wq_buf, kd_buf, at_buf, ubuf, obuf, *, n_tiles):
    b = pl.program_id(0)
    hg = pl.program_id(1)
    si = pl.program_id(2)
    R = GDN_ROWS
    heads = range(HEADS_PER_GROUP)
    sls = [slice(hl * GDN_DK, (hl + 1) * GDN_DK) for hl in heads]
    wslot = si % 2
    rslot = 1 - wslot

    @pl.when(si == 0)
    def _():
        s_ref[...] = jnp.zeros_like(s_ref)
        uv_buf[1] = jnp.zeros(uv_buf.shape[1:], uv_buf.dtype)
        wq_buf[1] = jnp.zeros(wq_buf.shape[1:], wq_buf.dtype)
        kd_buf[1] = jnp.zeros(kd_buf.shape[1:], kd_buf.dtype)
        at_buf[1] = jnp.zeros(at_buf.shape[1:], at_buf.dtype)
        ri = lax.broadcasted_iota(jnp.int32, (R, R), 0)
        ci = lax.broadcasted_iota(jnp.int32, (R, R), 1)
        x = ri ^ ci
        lv = jnp.full((R, R), -1, jnp.int32)
        for t in range(LOG2_CHUNK):
            lv = lv + jnp.where(x >= (1 << t), 1, 0)
        ok = ((ri >> LOG2_CHUNK) == (ci >> LOG2_CHUNK)) & (ri >= ci)
        lv_ref[...] = jnp.where(ok, lv, -2)

    p1 = {}

    def p1_scores():
        p1["qk_kk"] = []
        for hl in heads:
            kb = k_ref[:, sls[hl]]
            p1["qk_kk"].append(lax.dot_general(jnp.concatenate([q_ref[:, sls[hl]], kb], axis=0), kb,
                                               (((1,), (1,)), ((), ())), preferred_element_type=F32))

    def p1_decay():
        lv = lv_ref[...]
        incl = lv >= -1
        colg = colg_ref[...]
        p1["beta"] = [colg[:, hl:hl + 1] for hl in heads]
        p1["exp_g"] = [jnp.exp(colg[:, 4 + hl:5 + hl]) for hl in heads]
        p1["low"], p1["xinv"] = [], []
        for hl in heads:
            gap = colg[:, 4 + hl:5 + hl] - rowg_ref[hl:hl + 1, :]
            decay = jnp.where(incl, jnp.exp(jnp.where(incl, gap, 0.0)), 0.0)
            at_buf[wslot, hl] = (decay * p1["qk_kk"][hl][0:R]).astype(BF16)
            lo = jnp.where(lv >= 0, p1["beta"][hl] * decay * p1["qk_kk"][hl][R:2 * R], 0.0)
            p1["low"].append(lo)
            p1["xinv"].append(jnp.where(lv == -1, 1.0, 0.0) - jnp.where(lv == 0, lo, 0.0))

    def p1_level_a(level):
        def run():
            lv = lv_ref[...]
            p1["ys"] = [_bdot(jnp.where(lv == level, p1["low"][hl], 0.0), p1["xinv"][hl]) for hl in heads]
        return run

    def p1_level_b():
        p1["xinv"] = [p1["xinv"][hl] - _bdot(p1["xinv"][hl], p1["ys"][hl]) for hl in heads]

    def p1_solve():
        p1["sol"] = []
        for hl in heads:
            bk = (p1["beta"][hl] * p1["exp_g"][hl]) * k_ref[:, sls[hl]].astype(F32)
            rhs = jnp.concatenate([p1["beta"][hl] * v_ref[:, sls[hl]].astype(F32), bk], axis=1)
            p1["sol"].append(_bdot(p1["xinv"][hl], rhs))

    def p1_store():
        colg = colg_ref[...]
        for hl in heads:
            q_dec = q_ref[:, sls[hl]].astype(F32) * p1["exp_g"][hl]
            uv_buf[wslot, hl] = p1["sol"][hl][:, 0:GDN_DV]
            wq_buf[wslot, hl, 0:R, :] = p1["sol"][hl][:, GDN_DV:].astype(BF16)
            wq_buf[wslot, hl, R:2 * R, :] = q_dec.astype(BF16)
            kd_buf[wslot, hl] = (k_ref[:, sls[hl]].astype(F32)
                                 * jnp.exp(colg[:, 8 + hl:9 + hl])).astype(BF16)

    p1_stages = [p1_scores, p1_decay]
    for level in range(1, LOG2_CHUNK):
        p1_stages += [p1_level_a(level), p1_level_b]
    p1_stages += [p1_solve, p1_store]

    p2 = {}
    tile = jnp.maximum(si - 1, 0)

    def p2_read(c):
        def run():
            p2["states"] = [s_ref[hl] for hl in heads]
            p2["wqs"] = []
            for hl in heads:
                lhs = jnp.concatenate([wq_buf[rslot, hl, c * CHUNK:(c + 1) * CHUNK, :],
                                       wq_buf[rslot, hl, R + c * CHUNK:R + (c + 1) * CHUNK, :]], axis=0)
                p2["wqs"].append(jnp.dot(lhs, p2["states"][hl].astype(BF16),
                                         preferred_element_type=F32))
        return run

    def p2_update(c):
        def run():
            rs = slice(c * CHUNK, (c + 1) * CHUNK)
            for hl in heads:
                u = (uv_buf[rslot, hl, rs, :] - p2["wqs"][hl][0:CHUNK]).astype(BF16)
                ubuf[hl, rs, :] = u
                obuf[hl, rs, :] = p2["wqs"][hl][CHUNK:]
                cd = cd_ref[b * GDN_HEADS + hg * HEADS_PER_GROUP + hl, tile * CHUNKS_PER_STEP + c]
                upd = lax.dot_general(kd_buf[rslot, hl, rs, :], u, (((0,), (0,)), ((), ())),
                                      preferred_element_type=F32)
                s_ref[hl] = cd * p2["states"][hl] + upd
        return run

    def p2_output():
        for hl in heads:
            o = obuf[hl] + jnp.dot(at_buf[rslot, hl], ubuf[hl], preferred_element_type=F32)
            on = _rms(o, gn_ref[...]) * z_ref[:, sls[hl]].astype(F32)
            o_ref[:, sls[hl]] = on.astype(o_ref.dtype)

    p2_stages = []
    for c in range(CHUNKS_PER_STEP):
        p2_stages += [p2_read(c), p2_update(c)]
    p2_stages.append(p2_output)

    n1, n2 = len(p1_stages), len(p2_stages)
    i2 = 0
    for i1, stage in enumerate(p1_stages):
        stage()
        while i2 < n2 and (i2 + 1) * n1 <= (i1 + 1) * n2:
            p2_stages[i2]()
            i2 += 1
    while i2 < n2:
        p2_stages[i2]()
        i2 += 1


def _gdn(qkv, gate, colg4, rowg4, cdecay, gdn_norm, batch, seq):
    m = qkv.shape[0]
    ns = seq // GDN_ROWS
    gw = GROUP_W
    R = GDN_ROWS

    def cur(s):
        return jnp.minimum(s, ns - 1)

    def prv(s):
        return jnp.maximum(s - 1, 0)

    def xspec(col0):
        return pl.BlockSpec((R, gw), lambda b, h, s: (b * ns + cur(s), col0 // gw + h))

    hp = HEADS_PER_GROUP
    return pl.pallas_call(
        functools.partial(_gdn_kernel, n_tiles=ns),
        grid=(batch, HEAD_GROUPS, ns + 1),
        in_specs=[
            pl.BlockSpec(memory_space=pltpu.SMEM),
            xspec(0), xspec(QK_W), xspec(2 * QK_W),
            pl.BlockSpec((R, gw), lambda b, h, s: (b * ns + prv(s), GCOL_Z // gw + h)),
            pl.BlockSpec((None, R, LANES), lambda b, h, s: (h, b * ns + cur(s), 0)),
            pl.BlockSpec((None, None, hp, R), lambda b, h, s: (b, h, 0, cur(s))),
            pl.BlockSpec((1, GDN_DV), lambda b, h, s: (0, 0)),
        ],
        out_specs=pl.BlockSpec((R, gw), lambda b, h, s: (b * ns + prv(s), h)),
        out_shape=jax.ShapeDtypeStruct((m, V_W), BF16),
        scratch_shapes=[
            pltpu.VMEM((R, R), jnp.int32),
            pltpu.VMEM((hp, GDN_DK, GDN_DV), F32),
            pltpu.VMEM((2, hp, R, GDN_DV), F32),
            pltpu.VMEM((2, hp, 2 * R, GDN_DK), BF16),
            pltpu.VMEM((2, hp, R, GDN_DK), BF16),
            pltpu.VMEM((2, hp, R, R), BF16),
            pltpu.VMEM((hp, R, GDN_DV), BF16),
            pltpu.VMEM((hp, R, GDN_DV), F32),
        ],
        compiler_params=_cparams(("parallel", "parallel", "arbitrary")),
        name="gdn",
    )(cdecay, qkv, qkv, qkv, gate, colg4, rowg4, gdn_norm.reshape(1, GDN_DV))


POOL_HALO = 16


def _pool_kernel(p_ref, w_ref, sc_ref, o_ref, buf):
    si = pl.program_id(1)
    R = p_ref.shape[0]

    @pl.when(si == 0)
    def _():
        buf[0:POOL_HALO, :] = jnp.zeros((POOL_HALO, POOL_WIDTH), F32)

    buf[POOL_HALO:POOL_HALO + R, :] = p_ref[...].astype(F32)
    t = si * R + lax.broadcasted_iota(jnp.int32, (R, 1), 0)
    for gi, win in enumerate(POOL_WINDOWS):
        sl = slice(gi * POOL_GROUP, (gi + 1) * POOL_GROUP)
        x = buf[POOL_HALO:POOL_HALO + R, sl]
        acc = x
        for d in range(1, win):
            acc = acc + buf[POOL_HALO - d:POOL_HALO - d + R, sl]
        cnt = jnp.minimum(t + 1, win).astype(F32)
        y = acc / cnt - x
        o = _bdot(y, w_ref[gi]) * sc_ref[:, sl]
        o_ref[:, sl] = o.astype(o_ref.dtype)
    buf[0:POOL_HALO, :] = buf[R:R + POOL_HALO, :]


def _pool(gate, pool_w, pool_scale, batch, seq):
    m = gate.shape[0]
    rows = TILES["pool_rows"]
    ns = seq // rows
    return pl.pallas_call(
        _pool_kernel,
        grid=(batch, ns),
        in_specs=[
            pl.BlockSpec((rows, POOL_WIDTH), lambda b, s: (b * ns + s, GCOL_P // POOL_WIDTH)),
            pl.BlockSpec((len(POOL_WINDOWS), POOL_GROUP, POOL_GROUP), lambda b, s: (0, 0, 0)),
            pl.BlockSpec((1, POOL_WIDTH), lambda b, s: (0, 0)),
        ],
        out_specs=pl.BlockSpec((rows, POOL_WIDTH), lambda b, s: (b * ns + s, 0)),
        out_shape=jax.ShapeDtypeStruct((m, POOL_WIDTH), BF16),
        scratch_shapes=[pltpu.VMEM((rows + POOL_HALO, POOL_WIDTH), F32)],
        compiler_params=_cparams(("parallel", "arbitrary")),
        name="pool",
    )(gate, pool_w, pool_scale.reshape(1, POOL_WIDTH))


def _merge_kernel(a_ref, p_ref, wa_ref, wb_ref, ga_ref, gb_ref, o_ref):
    ya = jnp.dot(a_ref[...], wa_ref[...], preferred_element_type=F32)
    yb = jnp.dot(p_ref[...], wb_ref[...], preferred_element_type=F32)
    merged = ga_ref[...].astype(F32) * ya + gb_ref[...].astype(F32) * yb
    o_ref[...] = merged.astype(o_ref.dtype)


def _merge(og, pooled, wa, wb, gate):
    m = og.shape[0]
    n = wa.shape[1]
    tm, tn = TILES["merge_tm"], TILES["merge_tn"]
    return pl.pallas_call(
        _merge_kernel,
        grid=(m // tm, n // tn),
        in_specs=[
            pl.BlockSpec((tm, V_W), lambda i, j: (i, 0)),
            pl.BlockSpec((tm, POOL_WIDTH), lambda i, j: (i, 0)),
            pl.BlockSpec((V_W, tn), lambda i, j: (0, j)),
            pl.BlockSpec((POOL_WIDTH, tn), lambda i, j: (0, j)),
            pl.BlockSpec((tm, tn), lambda i, j: (i, GCOL_GA // tn + j)),
            pl.BlockSpec((tm, tn), lambda i, j: (i, GCOL_GB // tn + j)),
        ],
        out_specs=pl.BlockSpec((tm, tn), lambda i, j: (i, j)),
        out_shape=jax.ShapeDtypeStruct((m, n), BF16),
        compiler_params=_cparams(("parallel", "arbitrary")),
        name="merge",
    )(og, pooled, wa, wb, gate, gate)


def _xattn_kernel(q_ref, k_ref, v_ref, o_ref):
    s = lax.dot_general(q_ref[...], k_ref[...], (((1,), (1,)), ((), ())),
                        preferred_element_type=F32) * (XA_HEAD_DIM ** -0.5)
    s = s - jnp.max(s, axis=-1, keepdims=True)
    e = jnp.exp(s)
    p = e / jnp.sum(e, axis=-1, keepdims=True)
    o_ref[...] = jnp.dot(p.astype(BF16), v_ref[...], preferred_element_type=F32).astype(o_ref.dtype)


def _xattn(q, kv, batch, seq, mem_len):
    m = q.shape[0]
    ts = TILES["xattn_ts"]
    ns = seq // ts
    return pl.pallas_call(
        _xattn_kernel,
        grid=(batch, ns, XA_HEADS),
        in_specs=[
            pl.BlockSpec((ts, XA_HEAD_DIM), lambda b, s, h: (b * ns + s, h)),
            pl.BlockSpec((mem_len, XA_HEAD_DIM), lambda b, s, h: (b, h)),
            pl.BlockSpec((mem_len, XA_HEAD_DIM), lambda b, s, h: (b, XA_HEADS + h)),
        ],
        out_specs=pl.BlockSpec((ts, XA_HEAD_DIM), lambda b, s, h: (b * ns + s, h)),
        out_shape=jax.ShapeDtypeStruct((m, D_MODEL), BF16),
        compiler_params=_cparams(("parallel", "parallel", "arbitrary")),
        name="xattn",
    )(q, kv, kv)


def _ffn_up_kernel(h_ref, wa_ref, wb_ref, cwa_ref, cwb_ref, ba_ref, bb_ref, o_ref, ubuf, carry,
                   *, tiles_per_seq):
    i = pl.program_id(0)
    j = pl.program_id(1)
    tm = h_ref.shape[0]
    halves = []
    for t, (w_ref, cw_ref, b_ref) in enumerate(((wa_ref, cwa_ref, ba_ref), (wb_ref, cwb_ref, bb_ref))):
        u = jnp.dot(h_ref[...], w_ref[...], preferred_element_type=F32)
        prev = jnp.where(i % tiles_per_seq == 0, 0.0, carry[t, j])
        carry[t, j] = u[tm - SUBLANES:tm, :]
        halves.append(_causal_conv(u, prev, ubuf.at[t], cw_ref[...]) + b_ref[...])
    o_ref[...] = (_silu(halves[0]) * halves[1]).astype(o_ref.dtype)


def _ffn_up(h, w, cw, bias, seq):
    m, k = h.shape
    n = w.shape[1] // 2
    tm, tn = TILES["ffn_tm"], TILES["ffn_tn"]
    nt = n // tn

    def halves(rows):
        return [pl.BlockSpec((rows, tn), lambda i, j: (0, j)),
                pl.BlockSpec((rows, tn), lambda i, j: (0, nt + j))]

    (wa, wb), (cwa, cwb), (ba, bb) = [(a, a) for a in (w, cw, bias)]
    return pl.pallas_call(
        functools.partial(_ffn_up_kernel, tiles_per_seq=seq // tm),
        grid=(m // tm, n // tn),
        in_specs=[pl.BlockSpec((tm, k), lambda i, j: (i, 0))] + halves(k) + halves(FFN_CONV) + halves(1),
        out_specs=pl.BlockSpec((tm, tn), lambda i, j: (i, j)),
        out_shape=jax.ShapeDtypeStruct((m, n), BF16),
        scratch_shapes=[
            pltpu.VMEM((2, tm + SUBLANES, tn), F32),
            pltpu.VMEM((2, n // tn, SUBLANES, tn), F32),
        ],
        compiler_params=_cparams(("arbitrary", "arbitrary")),
        name="ffn_up",
    )(h, wa, wb, cwa, cwb, ba, bb)


def _pad_cols(a, n):
    return jnp.pad(a, ((0, 0), (0, n - a.shape[1])))


def _layer(x2d, mem2d, batch, seq, mem_len, p):
    m = x2d.shape[0]
    h = GDN_HEADS
    hp = HEADS_PER_GROUP
    w_in = p["w_in"]
    o_z = QKV_W
    o_b = o_z + V_W
    o_p = o_b + 2 * h
    o_ga = o_p + POOL_WIDTH
    o_gb = o_ga + D_MODEL
    w_bf = w_in.astype(BF16)
    w_gate = jnp.concatenate([w_bf[:, o_z:o_b], w_bf[:, o_ga:], w_bf[:, o_p:o_ga]], axis=1)
    w_ba = _pad_cols(w_bf[:, o_b:o_p], LANES)
    alog_l = jnp.zeros((1, LANES), F32).at[0, h:2 * h].set(p["a_log"])
    dtb_l = jnp.zeros((1, LANES), F32).at[0, h:2 * h].set(p["dt_bias"])

    def glu_halves(a):
        r = a.shape[0]
        a = jnp.pad(a.reshape(r, 2, D_FF), ((0, 0), (0, 0), (0, D_FF_PAD - D_FF)))
        return a.reshape(r, 2 * D_FF_PAD)

    w_up = glu_halves(p["w_up"]).astype(BF16)
    cw_up = glu_halves(p["ffn_conv_w"])
    b_up = glu_halves(p["ffn_conv_b"][None, :])
    w_down = jnp.pad(p["w_down"], ((0, D_FF_PAD - D_FF), (0, 0))).astype(BF16)

    qkv = _qkv_proj(x2d, p["mix_pre_norm"], w_bf, p["conv_qkv"], seq)
    gate = _gate_proj(x2d, p["mix_pre_norm"], w_gate)
    gates = _gates(x2d, p["mix_pre_norm"], w_ba, alog_l, dtb_l)
    g4 = gates[:, :4 * h].reshape(m, 4, HEAD_GROUPS, hp)
    colg4 = jnp.pad(g4[:, :3].transpose(2, 0, 1, 3).reshape(HEAD_GROUPS, m, 3 * hp),
                    ((0, 0), (0, 0), (0, LANES - 3 * hp)))
    rowg4 = gates[:, h:2 * h].reshape(batch, seq, HEAD_GROUPS, hp).transpose(0, 2, 3, 1)
    cdecay = gates[CHUNK - 1::CHUNK, 3 * h:4 * h].reshape(batch, seq // CHUNK, h)
    cdecay = cdecay.transpose(0, 2, 1).reshape(batch * h, seq // CHUNK)
    og = _gdn(qkv, gate, colg4, rowg4, cdecay, p["gdn_norm"], batch, seq)
    pooled = _pool(gate, p["pool_w"].astype(BF16), p["pool_scale"], batch, seq)
    merged = _merge(og, pooled, p["w_branch_a"].astype(BF16), p["w_branch_b"].astype(BF16), gate)
    x1, h2 = _mm_norm_res(merged, p["w_mix_out"].astype(BF16), x2d, p["mix_post_norm"],
                          p["xa_pre_norm"], tm=TILES["res_tm"])
    kv = _norm_mm(mem2d, p["mem_norm"], p["w_xkv"].astype(BF16), tm=mem2d.shape[0], tn=1024,
                  out_dtype=BF16)
    q = _mm(h2, p["w_xq"].astype(BF16), tm=TILES["q_tm"], tn=TILES["q_tn"], out_dtype=BF16)
    xo = _xattn(q, kv, batch, seq, mem_len)
    x2, h3 = _mm_norm_res(xo, p["w_xo"].astype(BF16), x1, p["xa_post_norm"], p["ffn_pre_norm"],
                          tm=TILES["res_tm"])
    act = _ffn_up(h3, w_up, cw_up, b_up, seq)
    x3, _ = _mm_norm_res(act, w_down, x2, p["ffn_post_norm"], None, tm=TILES["down_tm"])
    return x3


def kernel(x, mem, mix_pre_norm, w_in, conv_qkv, a_log, dt_bias, gdn_norm, pool_w, pool_scale,
           w_branch_a, w_branch_b, w_mix_out, mix_post_norm, xa_pre_norm, mem_norm, w_xq, w_xkv,
           w_xo, xa_post_norm, ffn_pre_norm, w_up, ffn_conv_w, ffn_conv_b, w_down, ffn_post_norm):
    params = dict(mix_pre_norm=mix_pre_norm, w_in=w_in, conv_qkv=conv_qkv, a_log=a_log,
                  dt_bias=dt_bias, gdn_norm=gdn_norm, pool_w=pool_w, pool_scale=pool_scale,
                  w_branch_a=w_branch_a, w_branch_b=w_branch_b, w_mix_out=w_mix_out,
                  mix_post_norm=mix_post_norm, xa_pre_norm=xa_pre_norm, mem_norm=mem_norm,
                  w_xq=w_xq, w_xkv=w_xkv, w_xo=w_xo, xa_post_norm=xa_post_norm,
                  ffn_pre_norm=ffn_pre_norm, w_up=w_up, ffn_conv_w=ffn_conv_w,
                  ffn_conv_b=ffn_conv_b, w_down=w_down, ffn_post_norm=ffn_post_norm)
    batch, seq, d = x.shape
    mem_len = mem.shape[1]
    x2d = x.reshape(batch * seq, d)
    mem2d = mem.reshape(batch * mem_len, d)
    for l in range(w_in.shape[0]):
        x2d = _layer(x2d, mem2d, batch, seq, mem_len, {k: v[l] for k, v in params.items()})
    return x2d.reshape(batch, seq, d)
```

```python
import functools

import jax
import jax.numpy as jnp
from jax import lax
from jax.experimental import pallas as pl
from jax.experimental.pallas import tpu as pltpu

F32 = jnp.float32
BF16 = jnp.bfloat16

D_MODEL = 2048
CHUNK = 64
GDN_HEADS = 16
GDN_DK = 128
GDN_DV = 128
GDN_CONV = 4
QK_W = GDN_HEADS * GDN_DK
V_W = GDN_HEADS * GDN_DV
QKV_W = 2 * QK_W + V_W
POOL_WINDOWS = (2, 4, 8, 16)
POOL_WIDTH = D_MODEL // 2
POOL_GROUP = POOL_WIDTH // 4
XA_HEADS = 4
XA_HEAD_DIM = D_MODEL // XA_HEADS
D_FF = 5504
FFN_CONV = 3
EPS = 1e-6

LANES = 128
SUBLANES = 8
VMEM_LIMIT = 56 * 1024 * 1024

COL_Z = QKV_W
QKVZ_W = QKV_W + V_W
GCOL_P = 0
GCOL_GA = POOL_WIDTH
GCOL_GB = GCOL_GA + D_MODEL

HEADS_PER_GROUP = 4
HEAD_GROUPS = GDN_HEADS // HEADS_PER_GROUP
GROUP_W = HEADS_PER_GROUP * GDN_DK
GDN_ROWS = 256
CHUNKS_PER_STEP = GDN_ROWS // CHUNK
LOG2_CHUNK = 6

D_FF_PAD = 5632
PROJ_SUB = 256

TILES = dict(
    proj_tm=1024, proj_tn=1024,
    gate_rows=512,
    pool_rows=512,
    merge_tm=1024, merge_tn=512,
    res_tm=512,
    q_tm=1024, q_tn=1024,
    xattn_ts=1024,
    ffn_tm=1024, ffn_tn=512,
    down_tm=256,
)


def _cparams(sem):
    return pltpu.CompilerParams(dimension_semantics=sem, vmem_limit_bytes=VMEM_LIMIT)


def _rms(xf, w):
    ms = jnp.mean(xf * xf, axis=-1, keepdims=True)
    return xf * lax.rsqrt(ms + EPS) * w


def _silu(x):
    return x * jax.nn.sigmoid(x)


def _bdot(a, b):
    return jnp.dot(a.astype(BF16), b.astype(BF16), preferred_element_type=F32)


def _causal_conv(u, prev, ubuf, cw):
    tm = u.shape[0]
    width = cw.shape[0]
    ubuf[0:SUBLANES, :] = prev
    ubuf[SUBLANES:SUBLANES + tm, :] = u
    y = u * cw[width - 1:width, :]
    for d in range(width - 1):
        off = SUBLANES - (width - 1) + d
        y = y + ubuf[off:off + tm, :] * cw[d:d + 1, :]
    return y


def _qkv_proj_kernel(x_ref, nw_ref, w_ref, cw_ref, wba_ref, o_ref, ba_ref, h_ref, ubuf, carry, *,
                     tiles_per_seq, tiles_per_part):
    i = pl.program_id(0)
    j = pl.program_id(1)
    tm, tn = o_ref.shape

    @pl.when(j == 0)
    def _():
        h_ref[...] = _rms(x_ref[...], nw_ref[...]).astype(BF16)
        ba_ref[...] = jnp.dot(h_ref[...], wba_ref[...], preferred_element_type=F32)

    is_qk = j < 2 * tiles_per_part
    scale = jnp.where(j < tiles_per_part, GDN_DK ** -0.5, 1.0)
    first = i % tiles_per_seq == 0

    @pl.when(j < 3 * tiles_per_part)
    def _():
        for c in range(tn // PROJ_SUB):
            cs = slice(c * PROJ_SUB, (c + 1) * PROJ_SUB)
            u = jnp.dot(h_ref[...], w_ref[:, cs], preferred_element_type=F32)
            prev = jnp.where(first, 0.0, carry[j, :, cs])
            carry[j, :, cs] = u[tm - SUBLANES:tm, :]
            y = _silu(_causal_conv(u, prev, ubuf.at[:, cs], cw_ref[:, cs]))
            for hb in range(PROJ_SUB // GDN_DK):
                yh = y[:, hb * GDN_DK:(hb + 1) * GDN_DK]
                ss = jnp.sum(yh * yh, axis=-1, keepdims=True)
                f = jnp.where(is_qk, lax.rsqrt(ss + EPS) * scale, 1.0)
                o_ref[:, c * PROJ_SUB + hb * GDN_DK:c * PROJ_SUB + (hb + 1) * GDN_DK] = (
                    yh * f).astype(o_ref.dtype)

    @pl.when(j >= 3 * tiles_per_part)
    def _():
        for c in range(tn // PROJ_SUB):
            cs = slice(c * PROJ_SUB, (c + 1) * PROJ_SUB)
            u = jnp.dot(h_ref[...], w_ref[:, cs], preferred_element_type=F32)
            o_ref[:, cs] = _silu(u).astype(o_ref.dtype)


def _qkv_proj(x, nw, w, conv_w, w_ba, seq):
    m, k = x.shape
    n = w.shape[1]
    tm, tn = TILES["proj_tm"], TILES["proj_tn"]
    return pl.pallas_call(
        functools.partial(_qkv_proj_kernel, tiles_per_seq=seq // tm, tiles_per_part=QK_W // tn),
        grid=(m // tm, n // tn),
        in_specs=[
            pl.BlockSpec((tm, k), lambda i, j: (i, 0)),
            pl.BlockSpec((1, k), lambda i, j: (0, 0)),
            pl.BlockSpec((k, tn), lambda i, j: (0, j)),
            pl.BlockSpec((GDN_CONV, tn), lambda i, j: (0, jnp.minimum(j, QKV_W // tn - 1))),
            pl.BlockSpec((k, LANES), lambda i, j: (0, 0)),
        ],
        out_specs=[pl.BlockSpec((tm, tn), lambda i, j: (i, j)),
                   pl.BlockSpec((tm, LANES), lambda i, j: (i, 0))],
        out_shape=[jax.ShapeDtypeStruct((m, n), BF16), jax.ShapeDtypeStruct((m, LANES), F32)],
        scratch_shapes=[
            pltpu.VMEM((tm, k), BF16),
            pltpu.VMEM((tm + SUBLANES, tn), F32),
            pltpu.VMEM((n // tn, SUBLANES, tn), F32),
        ],
        compiler_params=_cparams(("arbitrary", "arbitrary")),
        name="qkv_proj",
    )(x, nw.reshape(1, k), w, conv_w, w_ba)


def _gate_proj_kernel(x_ref, nw_ref, w_ref, o_ref, h_ref, *, pool_tiles):
    j = pl.program_id(1)

    @pl.when(j == 0)
    def _():
        h_ref[...] = _rms(x_ref[...], nw_ref[...]).astype(BF16)

    is_pool = j < pool_tiles
    for c in range(o_ref.shape[1] // PROJ_SUB):
        cs = slice(c * PROJ_SUB, (c + 1) * PROJ_SUB)
        u = jnp.dot(h_ref[...], w_ref[:, cs], preferred_element_type=F32)
        out = jnp.where(is_pool, u, jax.nn.sigmoid(u))
        o_ref[:, cs] = out.astype(o_ref.dtype)


def _gate_proj(x, nw, w):
    m, k = x.shape
    n = w.shape[1]
    tm, tn = TILES["proj_tm"], TILES["proj_tn"]
    return pl.pallas_call(
        functools.partial(_gate_proj_kernel, pool_tiles=POOL_WIDTH // tn),
        grid=(m // tm, n // tn),
        in_specs=[
            pl.BlockSpec((tm, k), lambda i, j: (i, 0)),
            pl.BlockSpec((1, k), lambda i, j: (0, 0)),
            pl.BlockSpec((k, tn), lambda i, j: (0, j)),
        ],
        out_specs=pl.BlockSpec((tm, tn), lambda i, j: (i, j)),
        out_shape=jax.ShapeDtypeStruct((m, n), BF16),
        scratch_shapes=[pltpu.VMEM((tm, k), BF16)],
        compiler_params=_cparams(("parallel", "arbitrary")),
        name="gate_proj",
    )(x, nw.reshape(1, k), w)


def _norm_mm_kernel(x_ref, nw_ref, w_ref, o_ref, h_ref):
    @pl.when(pl.program_id(1) == 0)
    def _():
        h_ref[...] = _rms(x_ref[...], nw_ref[...]).astype(BF16)

    o_ref[...] = jnp.dot(h_ref[...], w_ref[...], preferred_element_type=F32).astype(o_ref.dtype)


def _norm_mm(x, nw, w, *, tm, tn, out_dtype):
    m, k = x.shape
    n = w.shape[1]
    return pl.pallas_call(
        _norm_mm_kernel,
        grid=(m // tm, n // tn),
        in_specs=[
            pl.BlockSpec((tm, k), lambda i, j: (i, 0)),
            pl.BlockSpec((1, k), lambda i, j: (0, 0)),
            pl.BlockSpec((k, tn), lambda i, j: (0, j)),
        ],
        out_specs=pl.BlockSpec((tm, tn), lambda i, j: (i, j)),
        out_shape=jax.ShapeDtypeStruct((m, n), out_dtype),
        scratch_shapes=[pltpu.VMEM((tm, k), BF16)],
        compiler_params=_cparams(("parallel", "arbitrary")),
        name="norm_mm",
    )(x, nw.reshape(1, k), w)


def _mm_kernel(x_ref, w_ref, o_ref):
    o_ref[...] = jnp.dot(x_ref[...], w_ref[...], preferred_element_type=F32).astype(o_ref.dtype)


def _mm(x, w, *, tm, tn, out_dtype):
    m, k = x.shape
    n = w.shape[1]
    return pl.pallas_call(
        _mm_kernel,
        grid=(m // tm, n // tn),
        in_specs=[
            pl.BlockSpec((tm, k), lambda i, j: (i, 0)),
            pl.BlockSpec((k, tn), lambda i, j: (0, j)),
        ],
        out_specs=pl.BlockSpec((tm, tn), lambda i, j: (i, j)),
        out_shape=jax.ShapeDtypeStruct((m, n), out_dtype),
        compiler_params=_cparams(("parallel", "arbitrary")),
        name="mm",
    )(x, w)


def _mm_norm_res_kernel(a_ref, w_ref, res_ref, pw_ref, *rest, with_next):
    if with_next:
        nw_ref, x_ref, h_ref = rest
    else:
        (x_ref,) = rest
    y = jnp.dot(a_ref[...], w_ref[...], preferred_element_type=F32)
    xn = res_ref[...] + _rms(y, pw_ref[...])
    x_ref[...] = xn
    if with_next:
        h_ref[...] = _rms(xn, nw_ref[...]).astype(BF16)


def _mm_norm_res(a, w, res, post_w, next_w, *, tm):
    m, k = a.shape
    n = w.shape[1]
    with_next = next_w is not None
    row = pl.BlockSpec((tm, n), lambda i: (i, 0))
    vec = pl.BlockSpec((1, n), lambda i: (0, 0))
    wspec = pl.BlockSpec((k, n), lambda i: (0, 0), pipeline_mode=pl.Buffered(1))
    in_specs = [pl.BlockSpec((tm, k), lambda i: (i, 0)), wspec, row, vec]
    args = [a, w, res, post_w.reshape(1, n)]
    out_specs = [row]
    out_shape = [jax.ShapeDtypeStruct((m, n), F32)]
    if with_next:
        in_specs.append(vec)
        args.append(next_w.reshape(1, n))
        out_specs.append(row)
        out_shape.append(jax.ShapeDtypeStruct((m, n), BF16))
    outs = pl.pallas_call(
        functools.partial(_mm_norm_res_kernel, with_next=with_next),
        grid=(m // tm,),
        in_specs=in_specs,
        out_specs=out_specs,
        out_shape=out_shape,
        compiler_params=_cparams(("parallel",)),
        name="mm_norm_res",
    )(*args)
    return outs if with_next else (outs[0], None)


def _gates_kernel(ba_ref, alog_ref, dtb_ref, o_ref):
    rows = ba_ref.shape[0]
    ba = ba_ref[...]
    beta = jax.nn.sigmoid(ba)
    xs = ba + dtb_ref[...]
    softplus = jnp.maximum(xs, 0.0) + jnp.log1p(jnp.exp(-jnp.abs(xs)))
    g = -jnp.exp(alog_ref[...]) * softplus
    ri = lax.broadcasted_iota(jnp.int32, (rows, rows), 0)
    ci = lax.broadcasted_iota(jnp.int32, (rows, rows), 1)
    same = (ri >> LOG2_CHUNK) == (ci >> LOG2_CHUNK)
    tri = jnp.where(same & (ri >= ci), 1.0, 0.0).astype(F32)
    ones = jnp.where(same, 1.0, 0.0).astype(F32)
    gc = jnp.dot(tri, g, preferred_element_type=F32, precision=lax.Precision.HIGHEST)
    gl = jnp.dot(ones, g, preferred_element_type=F32, precision=lax.Precision.HIGHEST)
    lane = lax.broadcasted_iota(jnp.int32, ba.shape, 1)
    h = GDN_HEADS
    out = jnp.where(lane < h, beta, gc)
    out = jnp.where(lane < 2 * h, out, pltpu.roll(gl - gc, h, axis=1))
    out = jnp.where(lane < 3 * h, out, pltpu.roll(jnp.exp(gl), 2 * h, axis=1))
    o_ref[...] = out


def _gates(ba, alog_l, dtb_l):
    m = ba.shape[0]
    rows = TILES["gate_rows"]
    vec = pl.BlockSpec((1, LANES), lambda i: (0, 0))
    return pl.pallas_call(
        _gates_kernel,
        grid=(m // rows,),
        in_specs=[pl.BlockSpec((rows, LANES), lambda i: (i, 0)), vec, vec],
        out_specs=pl.BlockSpec((rows, LANES), lambda i: (i, 0)),
        out_shape=jax.ShapeDtypeStruct((m, LANES), F32),
        compiler_params=_cparams(("parallel",)),
        name="gdn_gates",
    )(ba, alog_l, dtb_l)


def _gdn_kernel(cd_ref, q_ref, k_ref, v_ref, z_ref, colg_ref, rowg_ref, gn_ref, o_ref,
                lv_ref, s_ref, uv_buf, wq_buf, kd_buf, at_buf, ubuf, obuf, *, n_tiles):
    b = pl.program_id(0)
    hg = pl.program_id(1)
    si = pl.program_id(2)
    R = GDN_ROWS
    heads = range(HEADS_PER_GROUP)
    sls = [slice(hl * GDN_DK, (hl + 1) * GDN_DK) for hl in heads]
    wslot = si % 2
    rslot = 1 - wslot

    @pl.when(si == 0)
    def _():
        s_ref[...] = jnp.zeros_like(s_ref)
        uv_buf[1] = jnp.zeros(uv_buf.shape[1:], uv_buf.dtype)
        wq_buf[1] = jnp.zeros(wq_buf.shape[1:], wq_buf.dtype)
        kd_buf[1] = jnp.zeros(kd_buf.shape[1:], kd_buf.dtype)
        at_buf[1] = jnp.zeros(at_buf.shape[1:], at_buf.dtype)
        ri = lax.broadcasted_iota(jnp.int32, (R, R), 0)
        ci = lax.broadcasted_iota(jnp.int32, (R, R), 1)
        x = ri ^ ci
        lv = jnp.full((R, R), -1, jnp.int32)
        for t in range(LOG2_CHUNK):
            lv = lv + jnp.where(x >= (1 << t), 1, 0)
        ok = ((ri >> LOG2_CHUNK) == (ci >> LOG2_CHUNK)) & (ri >= ci)
        lv_ref[...] = jnp.where(ok, lv, -2)

    p1 = {}

    def p1_scores():
        p1["qk_kk"] = []
        for hl in heads:
            kb = k_ref[:, sls[hl]]
            p1["qk_kk"].append(lax.dot_general(jnp.concatenate([q_ref[:, sls[hl]], kb], axis=0), kb,
                                               (((1,), (1,)), ((), ())), preferred_element_type=F32))

    def p1_decay():
        lv = lv_ref[...]
        incl = lv >= -1
        colg = colg_ref[...]
        p1["beta"] = [colg[:, hl:hl + 1] for hl in heads]
        p1["exp_g"] = [jnp.exp(colg[:, 4 + hl:5 + hl]) for hl in heads]
        p1["low"], p1["xinv"] = [], []
        for hl in heads:
            gap = colg[:, 4 + hl:5 + hl] - rowg_ref[hl:hl + 1, :]
            decay = jnp.where(incl, jnp.exp(jnp.where(incl, gap, 0.0)), 0.0)
            at_buf[wslot, hl] = (decay * p1["qk_kk"][hl][0:R]).astype(BF16)
            lo = jnp.where(lv >= 0, p1["beta"][hl] * decay * p1["qk_kk"][hl][R:2 * R], 0.0)
            p1["low"].append(lo)
            p1["xinv"].append(jnp.where(lv == -1, 1.0, 0.0) - jnp.where(lv == 0, lo, 0.0))

    def p1_level_a(level):
        def run():
            lv = lv_ref[...]
            p1["ys"] = [_bdot(jnp.where(lv == level, p1["low"][hl], 0.0), p1["xinv"][hl]) for hl in heads]
        return run

    def p1_level_b():
        p1["xinv"] = [p1["xinv"][hl] - _bdot(p1["xinv"][hl], p1["ys"][hl]) for hl in heads]

    def p1_solve():
        p1["sol"] = []
        for hl in heads:
            bk = (p1["beta"][hl] * p1["exp_g"][hl]) * k_ref[:, sls[hl]].astype(F32)
            rhs = jnp.concatenate([p1["beta"][hl] * v_ref[:, sls[hl]].astype(F32), bk], axis=1)
            p1["sol"].append(_bdot(p1["xinv"][hl], rhs))

    def p1_store():
        colg = colg_ref[...]
        for hl in heads:
            q_dec = q_ref[:, sls[hl]].astype(F32) * p1["exp_g"][hl]
            uv_buf[wslot, hl] = p1["sol"][hl][:, 0:GDN_DV]
            wq_buf[wslot, hl, 0:R, :] = p1["sol"][hl][:, GDN_DV:].astype(BF16)
            wq_buf[wslot, hl, R:2 * R, :] = q_dec.astype(BF16)
            kd_buf[wslot, hl] = (k_ref[:, sls[hl]].astype(F32)
                                 * jnp.exp(colg[:, 8 + hl:9 + hl])).astype(BF16)

    p1_stages = [p1_scores, p1_decay]
    for level in range(1, LOG2_CHUNK):
        p1_stages += [p1_level_a(level), p1_level_b]
    p1_stages += [p1_solve, p1_store]

    p2 = {}
    tile = jnp.maximum(si - 1, 0)

    def p2_read(c):
        def run():
            p2["states"] = [s_ref[hl] for hl in heads]
            p2["wqs"] = []
            for hl in heads:
                lhs = jnp.concatenate([wq_buf[rslot, hl, c * CHUNK:(c + 1) * CHUNK, :],
                                       wq_buf[rslot, hl, R + c * CHUNK:R + (c + 1) * CHUNK, :]], axis=0)
                p2["wqs"].append(jnp.dot(lhs, p2["states"][hl].astype(BF16),
                                         preferred_element_type=F32))
        return run

    def p2_update(c):
        def run():
            rs = slice(c * CHUNK, (c + 1) * CHUNK)
            for hl in heads:
                u = (uv_buf[rslot, hl, rs, :] - p2["wqs"][hl][0:CHUNK]).astype(BF16)
                ubuf[hl, rs, :] = u
                obuf[hl, rs, :] = p2["wqs"][hl][CHUNK:]
                cd = cd_ref[b * GDN_HEADS + hg * HEADS_PER_GROUP + hl, tile * CHUNKS_PER_STEP + c]
                upd = lax.dot_general(kd_buf[rslot, hl, rs, :], u, (((0,), (0,)), ((), ())),
                                      preferred_element_type=F32)
                s_ref[hl] = cd * p2["states"][hl] + upd
        return run

    def p2_output():
        for hl in heads:
            o = obuf[hl] + jnp.dot(at_buf[rslot, hl], ubuf[hl], preferred_element_type=F32)
            on = _rms(o, gn_ref[...]) * z_ref[:, sls[hl]].astype(F32)
            o_ref[:, sls[hl]] = on.astype(o_ref.dtype)

    p2_stages = []
    for c in range(CHUNKS_PER_STEP):
        p2_stages += [p2_read(c), p2_update(c)]
    p2_stages.append(p2_output)

    n1, n2 = len(p1_stages), len(p2_stages)
    i2 = 0
    for i1, stage in enumerate(p1_stages):
        stage()
        while i2 < n2 and (i2 + 1) * n1 <= (i1 + 1) * n2:
            p2_stages[i2]()
            i2 += 1
    while i2 < n2:
        p2_stages[i2]()
        i2 += 1


def _gdn(qkvz, colg4, rowg4, cdecay, gdn_norm, batch, seq):
    m = qkvz.shape[0]
    ns = seq // GDN_ROWS
    gw = GROUP_W
    R = GDN_ROWS

    def cur(s):
        return jnp.minimum(s, ns - 1)

    def prv(s):
        return jnp.maximum(s - 1, 0)

    def xspec(col0):
        return pl.BlockSpec((R, gw), lambda b, h, s: (b * ns + cur(s), col0 // gw + h))

    hp = HEADS_PER_GROUP
    return pl.pallas_call(
        functools.partial(_gdn_kernel, n_tiles=ns),
        grid=(batch, HEAD_GROUPS, ns + 1),
        in_specs=[
            pl.BlockSpec(memory_space=pltpu.SMEM),
            xspec(0), xspec(QK_W), xspec(2 * QK_W),
            pl.BlockSpec((R, gw), lambda b, h, s: (b * ns + prv(s), COL_Z // gw + h)),
            pl.BlockSpec((None, R, LANES), lambda b, h, s: (h, b * ns + cur(s), 0)),
            pl.BlockSpec((None, None, hp, R), lambda b, h, s: (b, h, 0, cur(s))),
            pl.BlockSpec((1, GDN_DV), lambda b, h, s: (0, 0)),
        ],
        out_specs=pl.BlockSpec((R, gw), lambda b, h, s: (b * ns + prv(s), h)),
        out_shape=jax.ShapeDtypeStruct((m, V_W), BF16),
        scratch_shapes=[
            pltpu.VMEM((R, R), jnp.int32),
            pltpu.VMEM((hp, GDN_DK, GDN_DV), F32),
            pltpu.VMEM((2, hp, R, GDN_DV), F32),
            pltpu.VMEM((2, hp, 2 * R, GDN_DK), BF16),
            pltpu.VMEM((2, hp, R, GDN_DK), BF16),
            pltpu.VMEM((2, hp, R, R), BF16),
            pltpu.VMEM((hp, R, GDN_DV), BF16),
            pltpu.VMEM((hp, R, GDN_DV), F32),
        ],
        compiler_params=_cparams(("parallel", "parallel", "arbitrary")),
        name="gdn",
    )(cdecay, qkvz, qkvz, qkvz, qkvz, colg4, rowg4, gdn_norm.reshape(1, GDN_DV))


POOL_HALO = 16


def _pool_kernel(p_ref, w_ref, sc_ref, o_ref, buf):
    si = pl.program_id(1)
    R = p_ref.shape[0]

    @pl.when(si == 0)
    def _():
        buf[0:POOL_HALO, :] = jnp.zeros((POOL_HALO, POOL_WIDTH), F32)

    buf[POOL_HALO:POOL_HALO + R, :] = p_ref[...].astype(F32)
    t = si * R + lax.broadcasted_iota(jnp.int32, (R, 1), 0)
    for gi, win in enumerate(POOL_WINDOWS):
        sl = slice(gi * POOL_GROUP, (gi + 1) * POOL_GROUP)
        x = buf[POOL_HALO:POOL_HALO + R, sl]
        acc = x
        for d in range(1, win):
            acc = acc + buf[POOL_HALO - d:POOL_HALO - d + R, sl]
        cnt = jnp.minimum(t + 1, win).astype(F32)
        y = acc / cnt - x
        o = _bdot(y, w_ref[gi]) * sc_ref[:, sl]
        o_ref[:, sl] = o.astype(o_ref.dtype)
    buf[0:POOL_HALO, :] = buf[R:R + POOL_HALO, :]


def _pool(gate, pool_w, pool_scale, batch, seq):
    m = gate.shape[0]
    rows = TILES["pool_rows"]
    ns = seq // rows
    return pl.pallas_call(
        _pool_kernel,
        grid=(batch, ns),
        in_specs=[
            pl.BlockSpec((rows, POOL_WIDTH), lambda b, s: (b * ns + s, GCOL_P // POOL_WIDTH)),
            pl.BlockSpec((len(POOL_WINDOWS), POOL_GROUP, POOL_GROUP), lambda b, s: (0, 0, 0)),
            pl.BlockSpec((1, POOL_WIDTH), lambda b, s: (0, 0)),
        ],
        out_specs=pl.BlockSpec((rows, POOL_WIDTH), lambda b, s: (b * ns + s, 0)),
        out_shape=jax.ShapeDtypeStruct((m, POOL_WIDTH), BF16),
        scratch_shapes=[pltpu.VMEM((rows + POOL_HALO, POOL_WIDTH), F32)],
        compiler_params=_cparams(("parallel", "arbitrary")),
        name="pool",
    )(gate, pool_w, pool_scale.reshape(1, POOL_WIDTH))


def _merge_kernel(a_ref, p_ref, wa_ref, wb_ref, ga_ref, gb_ref, o_ref):
    ya = jnp.dot(a_ref[...], wa_ref[...], preferred_element_type=F32)
    yb = jnp.dot(p_ref[...], wb_ref[...], preferred_element_type=F32)
    merged = ga_ref[...].astype(F32) * ya + gb_ref[...].astype(F32) * yb
    o_ref[...] = merged.astype(o_ref.dtype)


def _merge(og, pooled, wa, wb, gate):
    m = og.shape[0]
    n = wa.shape[1]
    tm, tn = TILES["merge_tm"], TILES["merge_tn"]
    return pl.pallas_call(
        _merge_kernel,
        grid=(m // tm, n // tn),
        in_specs=[
            pl.BlockSpec((tm, V_W), lambda i, j: (i, 0)),
            pl.BlockSpec((tm, POOL_WIDTH), lambda i, j: (i, 0)),
            pl.BlockSpec((V_W, tn), lambda i, j: (0, j)),
            pl.BlockSpec((POOL_WIDTH, tn), lambda i, j: (0, j)),
            pl.BlockSpec((tm, tn), lambda i, j: (i, GCOL_GA // tn + j)),
            pl.BlockSpec((tm, tn), lambda i, j: (i, GCOL_GB // tn + j)),
        ],
        out_specs=pl.BlockSpec((tm, tn), lambda i, j: (i, j)),
        out_shape=jax.ShapeDtypeStruct((m, n), BF16),
        compiler_params=_cparams(("parallel", "arbitrary")),
        name="merge",
    )(og, pooled, wa, wb, gate, gate)


def _xattn_kernel(q_ref, k_ref, v_ref, o_ref):
    s = lax.dot_general(q_ref[...], k_ref[...], (((1,), (1,)), ((), ())),
                        preferred_element_type=F32) * (XA_HEAD_DIM ** -0.5)
    s = s - jnp.max(s, axis=-1, keepdims=True)
    e = jnp.exp(s)
    p = e / jnp.sum(e, axis=-1, keepdims=True)
    o_ref[...] = jnp.dot(p.astype(BF16), v_ref[...], preferred_element_type=F32).astype(o_ref.dtype)


def _xattn(q, kv, batch, seq, mem_len):
    m = q.shape[0]
    ts = TILES["xattn_ts"]
    ns = seq // ts
    return pl.pallas_call(
        _xattn_kernel,
        grid=(batch, ns, XA_HEADS),
        in_specs=[
            pl.BlockSpec((ts, XA_HEAD_DIM), lambda b, s, h: (b * ns + s, h)),
            pl.BlockSpec((mem_len, XA_HEAD_DIM), lambda b, s, h: (b, h)),
            pl.BlockSpec((mem_len, XA_HEAD_DIM), lambda b, s, h: (b, XA_HEADS + h)),
        ],
        out_specs=pl.BlockSpec((ts, XA_HEAD_DIM), lambda b, s, h: (b * ns + s, h)),
        out_shape=jax.ShapeDtypeStruct((m, D_MODEL), BF16),
        compiler_params=_cparams(("parallel", "parallel", "arbitrary")),
        name="xattn",
    )(q, kv, kv)


def _ffn_up_kernel(h_ref, wa_ref, wb_ref, cwa_ref, cwb_ref, ba_ref, bb_ref, o_ref, ubuf, carry,
                   *, tiles_per_seq):
    i = pl.program_id(0)
    j = pl.program_id(1)
    tm = h_ref.shape[0]
    halves = []
    for t, (w_ref, cw_ref, b_ref) in enumerate(((wa_ref, cwa_ref, ba_ref), (wb_ref, cwb_ref, bb_ref))):
        u = jnp.dot(h_ref[...], w_ref[...], preferred_element_type=F32)
        prev = jnp.where(i % tiles_per_seq == 0, 0.0, carry[t, j])
        carry[t, j] = u[tm - SUBLANES:tm, :]
        halves.append(_causal_conv(u, prev, ubuf.at[t], cw_ref[...]) + b_ref[...])
    o_ref[...] = (_silu(halves[0]) * halves[1]).astype(o_ref.dtype)


def _ffn_up(h, wa, wb, cwa, cwb, ba, bb, seq):
    m, k = h.shape
    n = wa.shape[1]
    tm, tn = TILES["ffn_tm"], TILES["ffn_tn"]
    wspec = pl.BlockSpec((k, tn), lambda i, j: (0, j))
    cspec = pl.BlockSpec((FFN_CONV, tn), lambda i, j: (0, j))
    bspec = pl.BlockSpec((1, tn), lambda i, j: (0, j))
    return pl.pallas_call(
        functools.partial(_ffn_up_kernel, tiles_per_seq=seq // tm),
        grid=(m // tm, n // tn),
        in_specs=[pl.BlockSpec((tm, k), lambda i, j: (i, 0)), wspec, wspec, cspec, cspec, bspec, bspec],
        out_specs=pl.BlockSpec((tm, tn), lambda i, j: (i, j)),
        out_shape=jax.ShapeDtypeStruct((m, n), BF16),
        scratch_shapes=[
            pltpu.VMEM((2, tm + SUBLANES, tn), F32),
            pltpu.VMEM((2, n // tn, SUBLANES, tn), F32),
        ],
        compiler_params=_cparams(("arbitrary", "arbitrary")),
        name="ffn_up",
    )(h, wa, wb, cwa, cwb, ba, bb)


def _pad_cols(a, n):
    return jnp.pad(a, ((0, 0), (0, n - a.shape[1])))


def _layer(x2d, mem2d, batch, seq, mem_len, p):
    m = x2d.shape[0]
    h = GDN_HEADS
    hp = HEADS_PER_GROUP
    w_in = p["w_in"]
    o_z = QKV_W
    o_b = o_z + V_W
    o_p = o_b + 2 * h
    w_qkvz = w_in[:, :o_b].astype(BF16)
    w_gate = w_in[:, o_p:].astype(BF16)
    w_ba = _pad_cols(w_in[:, o_b:o_p], LANES).astype(BF16)
    alog_l = jnp.zeros((1, LANES), F32).at[0, h:2 * h].set(p["a_log"])
    dtb_l = jnp.zeros((1, LANES), F32).at[0, h:2 * h].set(p["dt_bias"])
    w_up = p["w_up"]
    wa_up = _pad_cols(w_up[:, :D_FF], D_FF_PAD).astype(BF16)
    wb_up = _pad_cols(w_up[:, D_FF:], D_FF_PAD).astype(BF16)
    cwa = _pad_cols(p["ffn_conv_w"][:, :D_FF], D_FF_PAD)
    cwb = _pad_cols(p["ffn_conv_w"][:, D_FF:], D_FF_PAD)
    fba = _pad_cols(p["ffn_conv_b"][None, :D_FF], D_FF_PAD)
    fbb = _pad_cols(p["ffn_conv_b"][None, D_FF:], D_FF_PAD)
    w_down = jnp.pad(p["w_down"], ((0, D_FF_PAD - D_FF), (0, 0))).astype(BF16)

    qkvz, ba = _qkv_proj(x2d, p["mix_pre_norm"], w_qkvz, p["conv_qkv"], w_ba, seq)
    gate = _gate_proj(x2d, p["mix_pre_norm"], w_gate)
    gates = _gates(ba, alog_l, dtb_l)
    g4 = gates[:, :4 * h].reshape(m, 4, HEAD_GROUPS, hp)
    colg4 = jnp.pad(g4[:, :3].transpose(2, 0, 1, 3).reshape(HEAD_GROUPS, m, 3 * hp),
                    ((0, 0), (0, 0), (0, LANES - 3 * hp)))
    rowg4 = gates[:, h:2 * h].reshape(batch, seq, HEAD_GROUPS, hp).transpose(0, 2, 3, 1)
    cdecay = gates[CHUNK - 1::CHUNK, 3 * h:4 * h].reshape(batch, seq // CHUNK, h)
    cdecay = cdecay.transpose(0, 2, 1).reshape(batch * h, seq // CHUNK)
    og = _gdn(qkvz, colg4, rowg4, cdecay, p["gdn_norm"], batch, seq)
    pooled = _pool(gate, p["pool_w"].astype(BF16), p["pool_scale"], batch, seq)
    merged = _merge(og, pooled, p["w_branch_a"].astype(BF16), p["w_branch_b"].astype(BF16), gate)
    x1, h2 = _mm_norm_res(merged, p["w_mix_out"].astype(BF16), x2d, p["mix_post_norm"],
                          p["xa_pre_norm"], tm=TILES["res_tm"])
    kv = _norm_mm(mem2d, p["mem_norm"], p["w_xkv"].astype(BF16), tm=mem2d.shape[0], tn=1024,
                  out_dtype=BF16)
    q = _mm(h2, p["w_xq"].astype(BF16), tm=TILES["q_tm"], tn=TILES["q_tn"], out_dtype=BF16)
    xo = _xattn(q, kv, batch, seq, mem_len)
    x2, h3 = _mm_norm_res(xo, p["w_xo"].astype(BF16), x1, p["xa_post_norm"], p["ffn_pre_norm"],
                          tm=TILES["res_tm"])
    act = _ffn_up(h3, wa_up, wb_up, cwa, cwb, fba, fbb, seq)
    x3, _ = _mm_norm_res(act, w_down, x2, p["ffn_post_norm"], None, tm=TILES["down_tm"])
    return x3


def kernel(x, mem, mix_pre_norm, w_in, conv_qkv, a_log, dt_bias, gdn_norm, pool_w, pool_scale,
           w_branch_a, w_branch_b, w_mix_out, mix_post_norm, xa_pre_norm, mem_norm, w_xq, w_xkv,
           w_xo, xa_post_norm, ffn_pre_norm, w_up, ffn_conv_w, ffn_conv_b, w_down, ffn_post_norm):
    params = dict(mix_pre_norm=mix_pre_norm, w_in=w_in, conv_qkv=conv_qkv, a_log=a_log,
                  dt_bias=dt_bias, gdn_norm=gdn_norm, pool_w=pool_w, pool_scale=pool_scale,
                  w_branch_a=w_branch_a, w_branch_b=w_branch_b, w_mix_out=w_mix_out,
                  mix_post_norm=mix_post_norm, xa_pre_norm=xa_pre_norm, mem_norm=mem_norm,
                  w_xq=w_xq, w_xkv=w_xkv, w_xo=w_xo, xa_post_norm=xa_post_norm,
                  ffn_pre_norm=ffn_pre_norm, w_up=w_up, ffn_conv_w=ffn_conv_w,
                  ffn_conv_b=ffn_conv_b, w_down=w_down, ffn_post_norm=ffn_post_norm)
    batch, seq, d = x.shape
    mem_len = mem.shape[1]
    x2d = x.reshape(batch * seq, d)
    mem2d = mem.reshape(batch * mem_len, d)
    for l in range(w_in.shape[0]):
        x2d = _layer(x2d, mem2d, batch, seq, mem_len, {k: v[l] for k, v in params.items()})
    return x2d.reshape(batch, seq, d)
```

```python
import functools

import jax
import jax.numpy as jnp
from jax import lax
from jax.experimental import pallas as pl
from jax.experimental.pallas import tpu as pltpu

F32 = jnp.float32
BF16 = jnp.bfloat16

D_MODEL = 2048
CHUNK = 64
GDN_HEADS = 16
GDN_DK = 128
GDN_DV = 128
GDN_CONV = 4
QK_W = GDN_HEADS * GDN_DK
V_W = GDN_HEADS * GDN_DV
QKV_W = 2 * QK_W + V_W
POOL_WINDOWS = (2, 4, 8, 16)
POOL_WIDTH = D_MODEL // 2
POOL_GROUP = POOL_WIDTH // 4
XA_HEADS = 4
XA_HEAD_DIM = D_MODEL // XA_HEADS
D_FF = 5504
FFN_CONV = 3
EPS = 1e-6

LANES = 128
SUBLANES = 8
VMEM_LIMIT = 56 * 1024 * 1024

COL_Z = QKV_W
QKVZ_W = QKV_W + V_W
GCOL_P = 0
GCOL_GA = POOL_WIDTH
GCOL_GB = GCOL_GA + D_MODEL

HEADS_PER_GROUP = 16
HEAD_GROUPS = GDN_HEADS // HEADS_PER_GROUP
GROUP_W = HEADS_PER_GROUP * GDN_DK
GDN_ROWS = 128
CHUNKS_PER_STEP = GDN_ROWS // CHUNK
LOG2_CHUNK = 6

D_FF_PAD = 5632
PROJ_SUB = 256

TILES = dict(
    proj_tm=1024, proj_tn=1024,
    gate_rows=512,
    pool_rows=512,
    merge_tm=1024, merge_tn=512,
    res_tm=512,
    q_tm=1024, q_tn=1024,
    xattn_ts=1024,
    ffn_tm=1024, ffn_tn=512,
    down_tm=256,
)


def _cparams(sem):
    return pltpu.CompilerParams(dimension_semantics=sem, vmem_limit_bytes=VMEM_LIMIT)


def _rms(xf, w):
    ms = jnp.mean(xf * xf, axis=-1, keepdims=True)
    return xf * lax.rsqrt(ms + EPS) * w


def _silu(x):
    return x * jax.nn.sigmoid(x)


def _bdot(a, b):
    return jnp.dot(a.astype(BF16), b.astype(BF16), preferred_element_type=F32)


def _causal_conv(u, prev, ubuf, cw):
    tm = u.shape[0]
    width = cw.shape[0]
    ubuf[0:SUBLANES, :] = prev
    ubuf[SUBLANES:SUBLANES + tm, :] = u
    y = u * cw[width - 1:width, :]
    for d in range(width - 1):
        off = SUBLANES - (width - 1) + d
        y = y + ubuf[off:off + tm, :] * cw[d:d + 1, :]
    return y


def _qkv_proj_kernel(x_ref, nw_ref, w_ref, cw_ref, wba_ref, o_ref, ba_ref, h_ref, ubuf, carry, *,
                     tiles_per_seq, tiles_per_part):
    i = pl.program_id(0)
    j = pl.program_id(1)
    tm, tn = o_ref.shape

    @pl.when(j == 0)
    def _():
        h_ref[...] = _rms(x_ref[...], nw_ref[...]).astype(BF16)
        ba_ref[...] = jnp.dot(h_ref[...], wba_ref[...], preferred_element_type=F32)

    is_qk = j < 2 * tiles_per_part
    scale = jnp.where(j < tiles_per_part, GDN_DK ** -0.5, 1.0)
    first = i % tiles_per_seq == 0

    @pl.when(j < 3 * tiles_per_part)
    def _():
        for c in range(tn // PROJ_SUB):
            cs = slice(c * PROJ_SUB, (c + 1) * PROJ_SUB)
            u = jnp.dot(h_ref[...], w_ref[:, cs], preferred_element_type=F32)
            prev = jnp.where(first, 0.0, carry[j, :, cs])
            carry[j, :, cs] = u[tm - SUBLANES:tm, :]
            y = _silu(_causal_conv(u, prev, ubuf.at[:, cs], cw_ref[:, cs]))
            for hb in range(PROJ_SUB // GDN_DK):
                yh = y[:, hb * GDN_DK:(hb + 1) * GDN_DK]
                ss = jnp.sum(yh * yh, axis=-1, keepdims=True)
                f = jnp.where(is_qk, lax.rsqrt(ss + EPS) * scale, 1.0)
                o_ref[:, c * PROJ_SUB + hb * GDN_DK:c * PROJ_SUB + (hb + 1) * GDN_DK] = (
                    yh * f).astype(o_ref.dtype)

    @pl.when(j >= 3 * tiles_per_part)
    def _():
        for c in range(tn // PROJ_SUB):
            cs = slice(c * PROJ_SUB, (c + 1) * PROJ_SUB)
            u = jnp.dot(h_ref[...], w_ref[:, cs], preferred_element_type=F32)
            o_ref[:, cs] = _silu(u).astype(o_ref.dtype)


def _qkv_proj(x, nw, w, conv_w, w_ba, seq):
    m, k = x.shape
    n = w.shape[1]
    tm, tn = TILES["proj_tm"], TILES["proj_tn"]
    return pl.pallas_call(
        functools.partial(_qkv_proj_kernel, tiles_per_seq=seq // tm, tiles_per_part=QK_W // tn),
        grid=(m // tm, n // tn),
        in_specs=[
            pl.BlockSpec((tm, k), lambda i, j: (i, 0)),
            pl.BlockSpec((1, k), lambda i, j: (0, 0)),
            pl.BlockSpec((k, tn), lambda i, j: (0, j)),
            pl.BlockSpec((GDN_CONV, tn), lambda i, j: (0, jnp.minimum(j, QKV_W // tn - 1))),
            pl.BlockSpec((k, LANES), lambda i, j: (0, 0)),
        ],
        out_specs=[pl.BlockSpec((tm, tn), lambda i, j: (i, j)),
                   pl.BlockSpec((tm, LANES), lambda i, j: (i, 0))],
        out_shape=[jax.ShapeDtypeStruct((m, n), BF16), jax.ShapeDtypeStruct((m, LANES), F32)],
        scratch_shapes=[
            pltpu.VMEM((tm, k), BF16),
            pltpu.VMEM((tm + SUBLANES, tn), F32),
            pltpu.VMEM((n // tn, SUBLANES, tn), F32),
        ],
        compiler_params=_cparams(("arbitrary", "arbitrary")),
        name="qkv_proj",
    )(x, nw.reshape(1, k), w, conv_w, w_ba)


def _gate_proj_kernel(x_ref, nw_ref, w_ref, o_ref, h_ref, *, pool_tiles):
    j = pl.program_id(1)

    @pl.when(j == 0)
    def _():
        h_ref[...] = _rms(x_ref[...], nw_ref[...]).astype(BF16)

    is_pool = j < pool_tiles
    for c in range(o_ref.shape[1] // PROJ_SUB):
        cs = slice(c * PROJ_SUB, (c + 1) * PROJ_SUB)
        u = jnp.dot(h_ref[...], w_ref[:, cs], preferred_element_type=F32)
        out = jnp.where(is_pool, u, jax.nn.sigmoid(u))
        o_ref[:, cs] = out.astype(o_ref.dtype)


def _gate_proj(x, nw, w):
    m, k = x.shape
    n = w.shape[1]
    tm, tn = TILES["proj_tm"], TILES["proj_tn"]
    return pl.pallas_call(
        functools.partial(_gate_proj_kernel, pool_tiles=POOL_WIDTH // tn),
        grid=(m // tm, n // tn),
        in_specs=[
            pl.BlockSpec((tm, k), lambda i, j: (i, 0)),
            pl.BlockSpec((1, k), lambda i, j: (0, 0)),
            pl.BlockSpec((k, tn), lambda i, j: (0, j)),
        ],
        out_specs=pl.BlockSpec((tm, tn), lambda i, j: (i, j)),
        out_shape=jax.ShapeDtypeStruct((m, n), BF16),
        scratch_shapes=[pltpu.VMEM((tm, k), BF16)],
        compiler_params=_cparams(("parallel", "arbitrary")),
        name="gate_proj",
    )(x, nw.reshape(1, k), w)


def _norm_mm_kernel(x_ref, nw_ref, w_ref, o_ref, h_ref):
    @pl.when(pl.program_id(1) == 0)
    def _():
        h_ref[...] = _rms(x_ref[...], nw_ref[...]).astype(BF16)

    o_ref[...] = jnp.dot(h_ref[...], w_ref[...], preferred_element_type=F32).astype(o_ref.dtype)


def _norm_mm(x, nw, w, *, tm, tn, out_dtype):
    m, k = x.shape
    n = w.shape[1]
    return pl.pallas_call(
        _norm_mm_kernel,
        grid=(m // tm, n // tn),
        in_specs=[
            pl.BlockSpec((tm, k), lambda i, j: (i, 0)),
            pl.BlockSpec((1, k), lambda i, j: (0, 0)),
            pl.BlockSpec((k, tn), lambda i, j: (0, j)),
        ],
        out_specs=pl.BlockSpec((tm, tn), lambda i, j: (i, j)),
        out_shape=jax.ShapeDtypeStruct((m, n), out_dtype),
        scratch_shapes=[pltpu.VMEM((tm, k), BF16)],
        compiler_params=_cparams(("parallel", "arbitrary")),
        name="norm_mm",
    )(x, nw.reshape(1, k), w)


def _mm_kernel(x_ref, w_ref, o_ref):
    o_ref[...] = jnp.dot(x_ref[...], w_ref[...], preferred_element_type=F32).astype(o_ref.dtype)


def _mm(x, w, *, tm, tn, out_dtype):
    m, k = x.shape
    n = w.shape[1]
    return pl.pallas_call(
        _mm_kernel,
        grid=(m // tm, n // tn),
        in_specs=[
            pl.BlockSpec((tm, k), lambda i, j: (i, 0)),
            pl.BlockSpec((k, tn), lambda i, j: (0, j)),
        ],
        out_specs=pl.BlockSpec((tm, tn), lambda i, j: (i, j)),
        out_shape=jax.ShapeDtypeStruct((m, n), out_dtype),
        compiler_params=_cparams(("parallel", "arbitrary")),
        name="mm",
    )(x, w)


def _mm_norm_res_kernel(a_ref, w_ref, res_ref, pw_ref, *rest, with_next):
    if with_next:
        nw_ref, x_ref, h_ref = rest
    else:
        (x_ref,) = rest
    y = jnp.dot(a_ref[...], w_ref[...], preferred_element_type=F32)
    xn = res_ref[...] + _rms(y, pw_ref[...])
    x_ref[...] = xn
    if with_next:
        h_ref[...] = _rms(xn, nw_ref[...]).astype(BF16)


def _mm_norm_res(a, w, res, post_w, next_w, *, tm):
    m, k = a.shape
    n = w.shape[1]
    with_next = next_w is not None
    row = pl.BlockSpec((tm, n), lambda i: (i, 0))
    vec = pl.BlockSpec((1, n), lambda i: (0, 0))
    wspec = pl.BlockSpec((k, n), lambda i: (0, 0), pipeline_mode=pl.Buffered(1))
    in_specs = [pl.BlockSpec((tm, k), lambda i: (i, 0)), wspec, row, vec]
    args = [a, w, res, post_w.reshape(1, n)]
    out_specs = [row]
    out_shape = [jax.ShapeDtypeStruct((m, n), F32)]
    if with_next:
        in_specs.append(vec)
        args.append(next_w.reshape(1, n))
        out_specs.append(row)
        out_shape.append(jax.ShapeDtypeStruct((m, n), BF16))
    outs = pl.pallas_call(
        functools.partial(_mm_norm_res_kernel, with_next=with_next),
        grid=(m // tm,),
        in_specs=in_specs,
        out_specs=out_specs,
        out_shape=out_shape,
        compiler_params=_cparams(("parallel",)),
        name="mm_norm_res",
    )(*args)
    return outs if with_next else (outs[0], None)


def _gates_kernel(ba_ref, alog_ref, dtb_ref, o_ref):
    rows = ba_ref.shape[0]
    ba = ba_ref[...]
    beta = jax.nn.sigmoid(ba)
    xs = ba + dtb_ref[...]
    softplus = jnp.maximum(xs, 0.0) + jnp.log1p(jnp.exp(-jnp.abs(xs)))
    g = -jnp.exp(alog_ref[...]) * softplus
    ri = lax.broadcasted_iota(jnp.int32, (rows, rows), 0)
    ci = lax.broadcasted_iota(jnp.int32, (rows, rows), 1)
    same = (ri >> LOG2_CHUNK) == (ci >> LOG2_CHUNK)
    tri = jnp.where(same & (ri >= ci), 1.0, 0.0).astype(F32)
    ones = jnp.where(same, 1.0, 0.0).astype(F32)
    gc = jnp.dot(tri, g, preferred_element_type=F32, precision=lax.Precision.HIGHEST)
    gl = jnp.dot(ones, g, preferred_element_type=F32, precision=lax.Precision.HIGHEST)
    lane = lax.broadcasted_iota(jnp.int32, ba.shape, 1)
    h = GDN_HEADS
    out = jnp.where(lane < h, beta, gc)
    out = jnp.where(lane < 2 * h, out, pltpu.roll(gl - gc, h, axis=1))
    out = jnp.where(lane < 3 * h, out, pltpu.roll(jnp.exp(gl), 2 * h, axis=1))
    o_ref[...] = out


def _gates(ba, alog_l, dtb_l):
    m = ba.shape[0]
    rows = TILES["gate_rows"]
    vec = pl.BlockSpec((1, LANES), lambda i: (0, 0))
    return pl.pallas_call(
        _gates_kernel,
        grid=(m // rows,),
        in_specs=[pl.BlockSpec((rows, LANES), lambda i: (i, 0)), vec, vec],
        out_specs=pl.BlockSpec((rows, LANES), lambda i: (i, 0)),
        out_shape=jax.ShapeDtypeStruct((m, LANES), F32),
        compiler_params=_cparams(("parallel",)),
        name="gdn_gates",
    )(ba, alog_l, dtb_l)


def _gdn_kernel(cd_ref, q_ref, k_ref, v_ref, z_ref, colg_ref, rowg_ref, gn_ref, o_ref,
                lv_ref, s_ref, uv_buf, wq_buf, kd_buf, at_buf, ubuf, obuf, *, n_tiles):
    b = pl.program_id(0)
    hg = pl.program_id(1)
    si = pl.program_id(2)
    R = GDN_ROWS
    hp = HEADS_PER_GROUP
    heads = range(hp)
    sls = [slice(hl * GDN_DK, (hl + 1) * GDN_DK) for hl in heads]
    wslot = si % 2
    rslot = 1 - wslot

    @pl.when(si == 0)
    def _():
        s_ref[...] = jnp.zeros_like(s_ref)
        uv_buf[1] = jnp.zeros(uv_buf.shape[1:], uv_buf.dtype)
        wq_buf[1] = jnp.zeros(wq_buf.shape[1:], wq_buf.dtype)
        kd_buf[1] = jnp.zeros(kd_buf.shape[1:], kd_buf.dtype)
        at_buf[1] = jnp.zeros(at_buf.shape[1:], at_buf.dtype)
        ri = lax.broadcasted_iota(jnp.int32, (R, R), 0)
        ci = lax.broadcasted_iota(jnp.int32, (R, R), 1)
        x = ri ^ ci
        lv = jnp.full((R, R), -1, jnp.int32)
        for t in range(LOG2_CHUNK):
            lv = lv + jnp.where(x >= (1 << t), 1, 0)
        ok = ((ri >> LOG2_CHUNK) == (ci >> LOG2_CHUNK)) & (ri >= ci)
        lv_ref[...] = jnp.where(ok, lv, -2)

    p1 = {}

    def p1_scores():
        p1["qk_kk"] = []
        for hl in heads:
            kb = k_ref[:, sls[hl]]
            p1["qk_kk"].append(lax.dot_general(jnp.concatenate([q_ref[:, sls[hl]], kb], axis=0), kb,
                                               (((1,), (1,)), ((), ())), preferred_element_type=F32))

    def p1_decay():
        lv = lv_ref[...]
        incl = lv >= -1
        colg = colg_ref[...]
        p1["beta"] = [colg[:, hl:hl + 1] for hl in heads]
        p1["exp_g"] = [jnp.exp(colg[:, hp + hl:hp + hl + 1]) for hl in heads]
        p1["low"], p1["xinv"] = [], []
        for hl in heads:
            gap = colg[:, hp + hl:hp + hl + 1] - rowg_ref[hl:hl + 1, :]
            decay = jnp.where(incl, jnp.exp(jnp.where(incl, gap, 0.0)), 0.0)
            at_buf[wslot, hl] = (decay * p1["qk_kk"][hl][0:R]).astype(BF16)
            lo = jnp.where(lv >= 0, p1["beta"][hl] * decay * p1["qk_kk"][hl][R:2 * R], 0.0)
            p1["low"].append(lo)
            p1["xinv"].append(jnp.where(lv == -1, 1.0, 0.0) - jnp.where(lv == 0, lo, 0.0))

    def p1_level_a(level):
        def run():
            lv = lv_ref[...]
            p1["ys"] = [_bdot(jnp.where(lv == level, p1["low"][hl], 0.0), p1["xinv"][hl]) for hl in heads]
        return run

    def p1_level_b():
        p1["xinv"] = [p1["xinv"][hl] - _bdot(p1["xinv"][hl], p1["ys"][hl]) for hl in heads]

    def p1_solve():
        p1["sol"] = []
        for hl in heads:
            bk = (p1["beta"][hl] * p1["exp_g"][hl]) * k_ref[:, sls[hl]].astype(F32)
            rhs = jnp.concatenate([p1["beta"][hl] * v_ref[:, sls[hl]].astype(F32), bk], axis=1)
            p1["sol"].append(_bdot(p1["xinv"][hl], rhs))

    def p1_store():
        colg = colg_ref[...]
        for hl in heads:
            q_dec = q_ref[:, sls[hl]].astype(F32) * p1["exp_g"][hl]
            uv_buf[wslot, hl] = p1["sol"][hl][:, 0:GDN_DV]
            wq_buf[wslot, hl, 0:R, :] = p1["sol"][hl][:, GDN_DV:].astype(BF16)
            wq_buf[wslot, hl, R:2 * R, :] = q_dec.astype(BF16)
            kd_buf[wslot, hl] = (k_ref[:, sls[hl]].astype(F32)
                                 * jnp.exp(colg[:, 2 * hp + hl:2 * hp + hl + 1])).astype(BF16)

    p1_stages = [p1_scores, p1_decay]
    for level in range(1, LOG2_CHUNK):
        p1_stages += [p1_level_a(level), p1_level_b]
    p1_stages += [p1_solve, p1_store]

    p2 = {}
    tile = jnp.maximum(si - 1, 0)

    def p2_read(c):
        def run():
            p2["states"] = [s_ref[hl] for hl in heads]
            p2["wqs"] = []
            for hl in heads:
                lhs = jnp.concatenate([wq_buf[rslot, hl, c * CHUNK:(c + 1) * CHUNK, :],
                                       wq_buf[rslot, hl, R + c * CHUNK:R + (c + 1) * CHUNK, :]], axis=0)
                p2["wqs"].append(jnp.dot(lhs, p2["states"][hl].astype(BF16),
                                         preferred_element_type=F32))
        return run

    def p2_update(c):
        def run():
            rs = slice(c * CHUNK, (c + 1) * CHUNK)
            for hl in heads:
                u = (uv_buf[rslot, hl, rs, :] - p2["wqs"][hl][0:CHUNK]).astype(BF16)
                ubuf[hl, rs, :] = u
                obuf[hl, rs, :] = p2["wqs"][hl][CHUNK:]
                cd = cd_ref[b * GDN_HEADS + hg * HEADS_PER_GROUP + hl, tile * CHUNKS_PER_STEP + c]
                upd = lax.dot_general(kd_buf[rslot, hl, rs, :], u, (((0,), (0,)), ((), ())),
                                      preferred_element_type=F32)
                s_ref[hl] = cd * p2["states"][hl] + upd
        return run

    def p2_output():
        for hl in heads:
            o = obuf[hl] + jnp.dot(at_buf[rslot, hl], ubuf[hl], preferred_element_type=F32)
            on = _rms(o, gn_ref[...]) * z_ref[:, sls[hl]].astype(F32)
            o_ref[:, sls[hl]] = on.astype(o_ref.dtype)

    p2_stages = []
    for c in range(CHUNKS_PER_STEP):
        p2_stages += [p2_read(c), p2_update(c)]
    p2_stages.append(p2_output)

    n1, n2 = len(p1_stages), len(p2_stages)
    i2 = 0
    for i1, stage in enumerate(p1_stages):
        stage()
        while i2 < n2 and (i2 + 1) * n1 <= (i1 + 1) * n2:
            p2_stages[i2]()
            i2 += 1
    while i2 < n2:
        p2_stages[i2]()
        i2 += 1


def _gdn(qkvz, colg4, rowg4, cdecay, gdn_norm, batch, seq):
    m = qkvz.shape[0]
    ns = seq // GDN_ROWS
    gw = GROUP_W
    R = GDN_ROWS

    def cur(s):
        return jnp.minimum(s, ns - 1)

    def prv(s):
        return jnp.maximum(s - 1, 0)

    def xspec(col0):
        return pl.BlockSpec((R, gw), lambda b, h, s: (b * ns + cur(s), col0 // gw + h))

    hp = HEADS_PER_GROUP
    return pl.pallas_call(
        functools.partial(_gdn_kernel, n_tiles=ns),
        grid=(batch, HEAD_GROUPS, ns + 1),
        in_specs=[
            pl.BlockSpec(memory_space=pltpu.SMEM),
            xspec(0), xspec(QK_W), xspec(2 * QK_W),
            pl.BlockSpec((R, gw), lambda b, h, s: (b * ns + prv(s), COL_Z // gw + h)),
            pl.BlockSpec((None, R, LANES), lambda b, h, s: (h, b * ns + cur(s), 0)),
            pl.BlockSpec((None, None, hp, R), lambda b, h, s: (b, h, 0, cur(s))),
            pl.BlockSpec((1, GDN_DV), lambda b, h, s: (0, 0)),
        ],
        out_specs=pl.BlockSpec((R, gw), lambda b, h, s: (b * ns + prv(s), h)),
        out_shape=jax.ShapeDtypeStruct((m, V_W), BF16),
        scratch_shapes=[
            pltpu.VMEM((R, R), jnp.int32),
            pltpu.VMEM((hp, GDN_DK, GDN_DV), F32),
            pltpu.VMEM((2, hp, R, GDN_DV), F32),
            pltpu.VMEM((2, hp, 2 * R, GDN_DK), BF16),
            pltpu.VMEM((2, hp, R, GDN_DK), BF16),
            pltpu.VMEM((2, hp, R, R), BF16),
            pltpu.VMEM((hp, R, GDN_DV), BF16),
            pltpu.VMEM((hp, R, GDN_DV), F32),
        ],
        compiler_params=_cparams(("parallel", "parallel", "arbitrary")),
        name="gdn",
    )(cdecay, qkvz, qkvz, qkvz, qkvz, colg4, rowg4, gdn_norm.reshape(1, GDN_DV))


POOL_HALO = 16


def _pool_kernel(p_ref, w_ref, sc_ref, o_ref, buf):
    si = pl.program_id(1)
    R = p_ref.shape[0]

    @pl.when(si == 0)
    def _():
        buf[0:POOL_HALO, :] = jnp.zeros((POOL_HALO, POOL_WIDTH), F32)

    buf[POOL_HALO:POOL_HALO + R, :] = p_ref[...].astype(F32)
    t = si * R + lax.broadcasted_iota(jnp.int32, (R, 1), 0)
    for gi, win in enumerate(POOL_WINDOWS):
        sl = slice(gi * POOL_GROUP, (gi + 1) * POOL_GROUP)
        x = buf[POOL_HALO:POOL_HALO + R, sl]
        acc = x
        for d in range(1, win):
            acc = acc + buf[POOL_HALO - d:POOL_HALO - d + R, sl]
        cnt = jnp.minimum(t + 1, win).astype(F32)
        y = acc / cnt - x
        o = _bdot(y, w_ref[gi]) * sc_ref[:, sl]
        o_ref[:, sl] = o.astype(o_ref.dtype)
    buf[0:POOL_HALO, :] = buf[R:R + POOL_HALO, :]


def _pool(gate, pool_w, pool_scale, batch, seq):
    m = gate.shape[0]
    rows = TILES["pool_rows"]
    ns = seq // rows
    return pl.pallas_call(
        _pool_kernel,
        grid=(batch, ns),
        in_specs=[
            pl.BlockSpec((rows, POOL_WIDTH), lambda b, s: (b * ns + s, GCOL_P // POOL_WIDTH)),
            pl.BlockSpec((len(POOL_WINDOWS), POOL_GROUP, POOL_GROUP), lambda b, s: (0, 0, 0)),
            pl.BlockSpec((1, POOL_WIDTH), lambda b, s: (0, 0)),
        ],
        out_specs=pl.BlockSpec((rows, POOL_WIDTH), lambda b, s: (b * ns + s, 0)),
        out_shape=jax.ShapeDtypeStruct((m, POOL_WIDTH), BF16),
        scratch_shapes=[pltpu.VMEM((rows + POOL_HALO, POOL_WIDTH), F32)],
        compiler_params=_cparams(("parallel", "arbitrary")),
        name="pool",
    )(gate, pool_w, pool_scale.reshape(1, POOL_WIDTH))


def _merge_kernel(a_ref, p_ref, wa_ref, wb_ref, ga_ref, gb_ref, o_ref):
    ya = jnp.dot(a_ref[...], wa_ref[...], preferred_element_type=F32)
    yb = jnp.dot(p_ref[...], wb_ref[...], preferred_element_type=F32)
    merged = ga_ref[...].astype(F32) * ya + gb_ref[...].astype(F32) * yb
    o_ref[...] = merged.astype(o_ref.dtype)


def _merge(og, pooled, wa, wb, gate):
    m = og.shape[0]
    n = wa.shape[1]
    tm, tn = TILES["merge_tm"], TILES["merge_tn"]
    return pl.pallas_call(
        _merge_kernel,
        grid=(m // tm, n // tn),
        in_specs=[
            pl.BlockSpec((tm, V_W), lambda i, j: (i, 0)),
            pl.BlockSpec((tm, POOL_WIDTH), lambda i, j: (i, 0)),
            pl.BlockSpec((V_W, tn), lambda i, j: (0, j)),
            pl.BlockSpec((POOL_WIDTH, tn), lambda i, j: (0, j)),
            pl.BlockSpec((tm, tn), lambda i, j: (i, GCOL_GA // tn + j)),
            pl.BlockSpec((tm, tn), lambda i, j: (i, GCOL_GB // tn + j)),
        ],
        out_specs=pl.BlockSpec((tm, tn), lambda i, j: (i, j)),
        out_shape=jax.ShapeDtypeStruct((m, n), BF16),
        compiler_params=_cparams(("parallel", "arbitrary")),
        name="merge",
    )(og, pooled, wa, wb, gate, gate)


def _xattn_kernel(q_ref, k_ref, v_ref, o_ref):
    s = lax.dot_general(q_ref[...], k_ref[...], (((1,), (1,)), ((), ())),
                        preferred_element_type=F32) * (XA_HEAD_DIM ** -0.5)
    s = s - jnp.max(s, axis=-1, keepdims=True)
    e = jnp.exp(s)
    p = e / jnp.sum(e, axis=-1, keepdims=True)
    o_ref[...] = jnp.dot(p.astype(BF16), v_ref[...], preferred_element_type=F32).astype(o_ref.dtype)


def _xattn(q, kv, batch, seq, mem_len):
    m = q.shape[0]
    ts = TILES["xattn_ts"]
    ns = seq // ts
    return pl.pallas_call(
        _xattn_kernel,
        grid=(batch, ns, XA_HEADS),
        in_specs=[
            pl.BlockSpec((ts, XA_HEAD_DIM), lambda b, s, h: (b * ns + s, h)),
            pl.BlockSpec((mem_len, XA_HEAD_DIM), lambda b, s, h: (b, h)),
            pl.BlockSpec((mem_len, XA_HEAD_DIM), lambda b, s, h: (b, XA_HEADS + h)),
        ],
        out_specs=pl.BlockSpec((ts, XA_HEAD_DIM), lambda b, s, h: (b * ns + s, h)),
        out_shape=jax.ShapeDtypeStruct((m, D_MODEL), BF16),
        compiler_params=_cparams(("parallel", "parallel", "arbitrary")),
        name="xattn",
    )(q, kv, kv)


def _ffn_up_kernel(h_ref, wa_ref, wb_ref, cwa_ref, cwb_ref, ba_ref, bb_ref, o_ref, ubuf, carry,
                   *, tiles_per_seq):
    i = pl.program_id(0)
    j = pl.program_id(1)
    tm = h_ref.shape[0]
    halves = []
    for t, (w_ref, cw_ref, b_ref) in enumerate(((wa_ref, cwa_ref, ba_ref), (wb_ref, cwb_ref, bb_ref))):
        u = jnp.dot(h_ref[...], w_ref[...], preferred_element_type=F32)
        prev = jnp.where(i % tiles_per_seq == 0, 0.0, carry[t, j])
        carry[t, j] = u[tm - SUBLANES:tm, :]
        halves.append(_causal_conv(u, prev, ubuf.at[t], cw_ref[...]) + b_ref[...])
    o_ref[...] = (_silu(halves[0]) * halves[1]).astype(o_ref.dtype)


def _ffn_up(h, wa, wb, cwa, cwb, ba, bb, seq):
    m, k = h.shape
    n = wa.shape[1]
    tm, tn = TILES["ffn_tm"], TILES["ffn_tn"]
    wspec = pl.BlockSpec((k, tn), lambda i, j: (0, j))
    cspec = pl.BlockSpec((FFN_CONV, tn), lambda i, j: (0, j))
    bspec = pl.BlockSpec((1, tn), lambda i, j: (0, j))
    return pl.pallas_call(
        functools.partial(_ffn_up_kernel, tiles_per_seq=seq // tm),
        grid=(m // tm, n // tn),
        in_specs=[pl.BlockSpec((tm, k), lambda i, j: (i, 0)), wspec, wspec, cspec, cspec, bspec, bspec],
        out_specs=pl.BlockSpec((tm, tn), lambda i, j: (i, j)),
        out_shape=jax.ShapeDtypeStruct((m, n), BF16),
        scratch_shapes=[
            pltpu.VMEM((2, tm + SUBLANES, tn), F32),
            pltpu.VMEM((2, n // tn, SUBLANES, tn), F32),
        ],
        compiler_params=_cparams(("arbitrary", "arbitrary")),
        name="ffn_up",
    )(h, wa, wb, cwa, cwb, ba, bb)


def _pad_cols(a, n):
    return jnp.pad(a, ((0, 0), (0, n - a.shape[1])))


def _layer(x2d, mem2d, batch, seq, mem_len, p):
    m = x2d.shape[0]
    h = GDN_HEADS
    hp = HEADS_PER_GROUP
    w_in = p["w_in"]
    o_z = QKV_W
    o_b = o_z + V_W
    o_p = o_b + 2 * h
    w_qkvz = w_in[:, :o_b].astype(BF16)
    w_gate = w_in[:, o_p:].astype(BF16)
    w_ba = _pad_cols(w_in[:, o_b:o_p], LANES).astype(BF16)
    alog_l = jnp.zeros((1, LANES), F32).at[0, h:2 * h].set(p["a_log"])
    dtb_l = jnp.zeros((1, LANES), F32).at[0, h:2 * h].set(p["dt_bias"])
    w_up = p["w_up"]
    wa_up = _pad_cols(w_up[:, :D_FF], D_FF_PAD).astype(BF16)
    wb_up = _pad_cols(w_up[:, D_FF:], D_FF_PAD).astype(BF16)
    cwa = _pad_cols(p["ffn_conv_w"][:, :D_FF], D_FF_PAD)
    cwb = _pad_cols(p["ffn_conv_w"][:, D_FF:], D_FF_PAD)
    fba = _pad_cols(p["ffn_conv_b"][None, :D_FF], D_FF_PAD)
    fbb = _pad_cols(p["ffn_conv_b"][None, D_FF:], D_FF_PAD)
    w_down = jnp.pad(p["w_down"], ((0, D_FF_PAD - D_FF), (0, 0))).astype(BF16)

    qkvz, ba = _qkv_proj(x2d, p["mix_pre_norm"], w_qkvz, p["conv_qkv"], w_ba, seq)
    gate = _gate_proj(x2d, p["mix_pre_norm"], w_gate)
    gates = _gates(ba, alog_l, dtb_l)
    g4 = gates[:, :4 * h].reshape(m, 4, HEAD_GROUPS, hp)
    colg4 = jnp.pad(g4[:, :3].transpose(2, 0, 1, 3).reshape(HEAD_GROUPS, m, 3 * hp),
                    ((0, 0), (0, 0), (0, LANES - 3 * hp)))
    rowg4 = gates[:, h:2 * h].reshape(batch, seq, HEAD_GROUPS, hp).transpose(0, 2, 3, 1)
    cdecay = gates[CHUNK - 1::CHUNK, 3 * h:4 * h].reshape(batch, seq // CHUNK, h)
    cdecay = cdecay.transpose(0, 2, 1).reshape(batch * h, seq // CHUNK)
    og = _gdn(qkvz, colg4, rowg4, cdecay, p["gdn_norm"], batch, seq)
    pooled = _pool(gate, p["pool_w"].astype(BF16), p["pool_scale"], batch, seq)
    merged = _merge(og, pooled, p["w_branch_a"].astype(BF16), p["w_branch_b"].astype(BF16), gate)
    x1, h2 = _mm_norm_res(merged, p["w_mix_out"].astype(BF16), x2d, p["mix_post_norm"],
                          p["xa_pre_norm"], tm=TILES["res_tm"])
    kv = _norm_mm(mem2d, p["mem_norm"], p["w_xkv"].astype(BF16), tm=mem2d.shape[0], tn=1024,
                  out_dtype=BF16)
    q = _mm(h2, p["w_xq"].astype(BF16), tm=TILES["q_tm"], tn=TILES["q_tn"], out_dtype=BF16)
    xo = _xattn(q, kv, batch, seq, mem_len)
    x2, h3 = _mm_norm_res(xo, p["w_xo"].astype(BF16), x1, p["xa_post_norm"], p["ffn_pre_norm"],
                          tm=TILES["res_tm"])
    act = _ffn_up(h3, wa_up, wb_up, cwa, cwb, fba, fbb, seq)
    x3, _ = _mm_norm_res(act, w_down, x2, p["ffn_post_norm"], None, tm=TILES["down_tm"])
    return x3


def kernel(x, mem, mix_pre_norm, w_in, conv_qkv, a_log, dt_bias, gdn_norm, pool_w, pool_scale,
           w_branch_a, w_branch_b, w_mix_out, mix_post_norm, xa_pre_norm, mem_norm, w_xq, w_xkv,
           w_xo, xa_post_norm, ffn_pre_norm, w_up, ffn_conv_w, ffn_conv_b, w_down, ffn_post_norm):
    params = dict(mix_pre_norm=mix_pre_norm, w_in=w_in, conv_qkv=conv_qkv, a_log=a_log,
                  dt_bias=dt_bias, gdn_norm=gdn_norm, pool_w=pool_w, pool_scale=pool_scale,
                  w_branch_a=w_branch_a, w_branch_b=w_branch_b, w_mix_out=w_mix_out,
                  mix_post_norm=mix_post_norm, xa_pre_norm=xa_pre_norm, mem_norm=mem_norm,
                  w_xq=w_xq, w_xkv=w_xkv, w_xo=w_xo, xa_post_norm=xa_post_norm,
                  ffn_pre_norm=ffn_pre_norm, w_up=w_up, ffn_conv_w=ffn_conv_w,
                  ffn_conv_b=ffn_conv_b, w_down=w_down, ffn_post_norm=ffn_post_norm)
    batch, seq, d = x.shape
    mem_len = mem.shape[1]
    x2d = x.reshape(batch * seq, d)
    mem2d = mem.reshape(batch * mem_len, d)
    for l in range(w_in.shape[0]):
        x2d = _layer(x2d, mem2d, batch, seq, mem_len, {k: v[l] for k, v in params.items()})
    return x2d.reshape(batch, seq, d)
```

```python
import functools

import jax
import jax.numpy as jnp
from jax import lax
from jax.experimental import pallas as pl
from jax.experimental.pallas import tpu as pltpu

F32 = jnp.float32
BF16 = jnp.bfloat16

D_MODEL = 2048
CHUNK = 64
GDN_HEADS = 16
GDN_DK = 128
GDN_DV = 128
GDN_CONV = 4
QK_W = GDN_HEADS * GDN_DK
V_W = GDN_HEADS * GDN_DV
QKV_W = 2 * QK_W + V_W
POOL_WINDOWS = (2, 4, 8, 16)
POOL_WIDTH = D_MODEL // 2
POOL_GROUP = POOL_WIDTH // 4
XA_HEADS = 4
XA_HEAD_DIM = D_MODEL // XA_HEADS
D_FF = 5504
FFN_CONV = 3
EPS = 1e-6

LANES = 128
SUBLANES = 8
VMEM_LIMIT = 56 * 1024 * 1024

COL_Z = QKV_W
QKVZ_W = QKV_W + V_W
GCOL_P = 0
GCOL_GA = POOL_WIDTH
GCOL_GB = GCOL_GA + D_MODEL

HEADS_PER_GROUP = 16
HEAD_GROUPS = GDN_HEADS // HEADS_PER_GROUP
GROUP_W = HEADS_PER_GROUP * GDN_DK
GDN_ROWS = 128
CHUNKS_PER_STEP = GDN_ROWS // CHUNK
LOG2_CHUNK = 6

D_FF_PAD = 5632
PROJ_SUB = 256
PROJ_ROWS = 128
FFN_ROWS = 1024
FFN_SUB = 512

TILES = dict(
    proj_tm=1024, proj_tn=1024,
    gate_rows=512,
    pool_rows=512,
    merge_tm=1024, merge_tn=512,
    res_tm=512,
    xattn_ts=1024,
    ffn_tm=1024, ffn_tn=512,
    down_tm=256,
)


def _cparams(sem):
    return pltpu.CompilerParams(dimension_semantics=sem, vmem_limit_bytes=VMEM_LIMIT)


def _rms(xf, w):
    ms = jnp.mean(xf * xf, axis=-1, keepdims=True)
    return xf * lax.rsqrt(ms + EPS) * w


def _silu(x):
    return x * jax.nn.sigmoid(x)


def _bdot(a, b):
    return jnp.dot(a.astype(BF16), b.astype(BF16), preferred_element_type=F32)


def _causal_conv(u, prev, ubuf, cw):
    tm = u.shape[0]
    width = cw.shape[0]
    ubuf[0:SUBLANES, :] = prev
    ubuf[SUBLANES:SUBLANES + tm, :] = u
    y = u * cw[width - 1:width, :]
    for d in range(width - 1):
        off = SUBLANES - (width - 1) + d
        y = y + ubuf[off:off + tm, :] * cw[d:d + 1, :]
    return y


def _qkv_proj_kernel(x_ref, nw_ref, w_ref, cw_ref, wba_ref, o_ref, ba_ref, h_ref, ubuf, carry, *,
                     tiles_per_seq, tiles_per_part):
    i = pl.program_id(0)
    j = pl.program_id(1)
    tm, tn = o_ref.shape

    @pl.when(j == 0)
    def _():
        h_ref[...] = _rms(x_ref[...], nw_ref[...]).astype(BF16)
        ba_ref[...] = jnp.dot(h_ref[...], wba_ref[...], preferred_element_type=F32)

    is_qk = j < 2 * tiles_per_part
    scale = jnp.where(j < tiles_per_part, GDN_DK ** -0.5, 1.0)
    first = i % tiles_per_seq == 0

    @pl.when(j < 3 * tiles_per_part)
    def _():
        rb = PROJ_ROWS
        for c in range(tn // PROJ_SUB):
            cs = slice(c * PROJ_SUB, (c + 1) * PROJ_SUB)
            cw = cw_ref[:, cs]
            ubuf[0:SUBLANES, cs] = jnp.where(first, 0.0, carry[j, :, cs])
            for r in range(tm // rb):
                base = SUBLANES + r * rb
                u = jnp.dot(h_ref[r * rb:(r + 1) * rb, :], w_ref[:, cs], preferred_element_type=F32)
                ubuf[base:base + rb, cs] = u
                if r == tm // rb - 1:
                    carry[j, :, cs] = u[rb - SUBLANES:rb, :]
                y = u * cw[GDN_CONV - 1:GDN_CONV, :]
                for d in range(GDN_CONV - 1):
                    off = base - (GDN_CONV - 1) + d
                    y = y + ubuf[off:off + rb, cs] * cw[d:d + 1, :]
                y = _silu(y)
                for hb in range(PROJ_SUB // GDN_DK):
                    yh = y[:, hb * GDN_DK:(hb + 1) * GDN_DK]
                    ss = jnp.sum(yh * yh, axis=-1, keepdims=True)
                    f = jnp.where(is_qk, lax.rsqrt(ss + EPS) * scale, 1.0)
                    o_ref[r * rb:(r + 1) * rb, c * PROJ_SUB + hb * GDN_DK:c * PROJ_SUB + (hb + 1) * GDN_DK] = (
                        yh * f).astype(o_ref.dtype)

    @pl.when(j >= 3 * tiles_per_part)
    def _():
        rb = PROJ_ROWS
        for c in range(tn // PROJ_SUB):
            cs = slice(c * PROJ_SUB, (c + 1) * PROJ_SUB)
            for r in range(tm // rb):
                rs = slice(r * rb, (r + 1) * rb)
                u = jnp.dot(h_ref[rs, :], w_ref[:, cs], preferred_element_type=F32)
                o_ref[rs, cs] = _silu(u).astype(o_ref.dtype)


def _qkv_proj(x, nw, w, conv_w, w_ba, seq):
    m, k = x.shape
    n = w.shape[1]
    tm, tn = TILES["proj_tm"], TILES["proj_tn"]
    return pl.pallas_call(
        functools.partial(_qkv_proj_kernel, tiles_per_seq=seq // tm, tiles_per_part=QK_W // tn),
        grid=(m // tm, n // tn),
        in_specs=[
            pl.BlockSpec((tm, k), lambda i, j: (i, 0)),
            pl.BlockSpec((1, k), lambda i, j: (0, 0)),
            pl.BlockSpec((k, tn), lambda i, j: (0, j)),
            pl.BlockSpec((GDN_CONV, tn), lambda i, j: (0, jnp.minimum(j, QKV_W // tn - 1))),
            pl.BlockSpec((k, LANES), lambda i, j: (0, 0)),
        ],
        out_specs=[pl.BlockSpec((tm, tn), lambda i, j: (i, j)),
                   pl.BlockSpec((tm, LANES), lambda i, j: (i, 0))],
        out_shape=[jax.ShapeDtypeStruct((m, n), BF16), jax.ShapeDtypeStruct((m, LANES), F32)],
        scratch_shapes=[
            pltpu.VMEM((tm, k), BF16),
            pltpu.VMEM((tm + SUBLANES, tn), F32),
            pltpu.VMEM((n // tn, SUBLANES, tn), F32),
        ],
        compiler_params=_cparams(("arbitrary", "arbitrary")),
        name="qkv_proj",
    )(x, nw.reshape(1, k), w, conv_w, w_ba)


def _gate_proj_kernel(x_ref, nw_ref, w_ref, o_ref, h_ref, *, pool_tiles):
    j = pl.program_id(1)

    @pl.when(j == 0)
    def _():
        h_ref[...] = _rms(x_ref[...], nw_ref[...]).astype(BF16)

    is_pool = j < pool_tiles
    tm, tn = o_ref.shape
    rb = PROJ_ROWS
    for c in range(tn // PROJ_SUB):
        cs = slice(c * PROJ_SUB, (c + 1) * PROJ_SUB)
        for r in range(tm // rb):
            rs = slice(r * rb, (r + 1) * rb)
            u = jnp.dot(h_ref[rs, :], w_ref[:, cs], preferred_element_type=F32)
            out = jnp.where(is_pool, u, jax.nn.sigmoid(u))
            o_ref[rs, cs] = out.astype(o_ref.dtype)


def _gate_proj(x, nw, w):
    m, k = x.shape
    n = w.shape[1]
    tm, tn = TILES["proj_tm"], TILES["proj_tn"]
    return pl.pallas_call(
        functools.partial(_gate_proj_kernel, pool_tiles=POOL_WIDTH // tn),
        grid=(m // tm, n // tn),
        in_specs=[
            pl.BlockSpec((tm, k), lambda i, j: (i, 0)),
            pl.BlockSpec((1, k), lambda i, j: (0, 0)),
            pl.BlockSpec((k, tn), lambda i, j: (0, j)),
        ],
        out_specs=pl.BlockSpec((tm, tn), lambda i, j: (i, j)),
        out_shape=jax.ShapeDtypeStruct((m, n), BF16),
        scratch_shapes=[pltpu.VMEM((tm, k), BF16)],
        compiler_params=_cparams(("parallel", "arbitrary")),
        name="gate_proj",
    )(x, nw.reshape(1, k), w)


def _norm_mm_kernel(x_ref, nw_ref, w_ref, o_ref, h_ref):
    @pl.when(pl.program_id(1) == 0)
    def _():
        h_ref[...] = _rms(x_ref[...], nw_ref[...]).astype(BF16)

    o_ref[...] = jnp.dot(h_ref[...], w_ref[...], preferred_element_type=F32).astype(o_ref.dtype)


def _norm_mm(x, nw, w, *, tm, tn, out_dtype):
    m, k = x.shape
    n = w.shape[1]
    return pl.pallas_call(
        _norm_mm_kernel,
        grid=(m // tm, n // tn),
        in_specs=[
            pl.BlockSpec((tm, k), lambda i, j: (i, 0)),
            pl.BlockSpec((1, k), lambda i, j: (0, 0)),
            pl.BlockSpec((k, tn), lambda i, j: (0, j)),
        ],
        out_specs=pl.BlockSpec((tm, tn), lambda i, j: (i, j)),
        out_shape=jax.ShapeDtypeStruct((m, n), out_dtype),
        scratch_shapes=[pltpu.VMEM((tm, k), BF16)],
        compiler_params=_cparams(("parallel", "arbitrary")),
        name="norm_mm",
    )(x, nw.reshape(1, k), w)


def _mm_norm_res_kernel(a_ref, w_ref, res_ref, pw_ref, *rest, with_next, with_proj):
    rest = list(rest)
    nw_ref = rest.pop(0) if with_next else None
    w2_ref = rest.pop(0) if with_proj else None
    x_ref = rest.pop(0)
    y = jnp.dot(a_ref[...], w_ref[...], preferred_element_type=F32)
    xn = res_ref[...] + _rms(y, pw_ref[...])
    x_ref[...] = xn
    if with_next:
        h = _rms(xn, nw_ref[...]).astype(BF16)
        if with_proj:
            h = jnp.dot(h, w2_ref[...], preferred_element_type=F32).astype(BF16)
        rest[0][...] = h


def _mm_norm_res(a, w, res, post_w, next_w, *, tm, next_proj=None):
    m, k = a.shape
    n = w.shape[1]
    with_next = next_w is not None
    with_proj = next_proj is not None
    row = pl.BlockSpec((tm, n), lambda i: (i, 0))
    vec = pl.BlockSpec((1, n), lambda i: (0, 0))
    wspec = pl.BlockSpec((k, n), lambda i: (0, 0), pipeline_mode=pl.Buffered(1))
    in_specs = [pl.BlockSpec((tm, k), lambda i: (i, 0)), wspec, row, vec]
    args = [a, w, res, post_w.reshape(1, n)]
    out_specs = [row]
    out_shape = [jax.ShapeDtypeStruct((m, n), F32)]
    if with_next:
        in_specs.append(vec)
        args.append(next_w.reshape(1, n))
        if with_proj:
            in_specs.append(pl.BlockSpec(next_proj.shape, lambda i: (0, 0), pipeline_mode=pl.Buffered(1)))
            args.append(next_proj)
        out_specs.append(row)
        out_shape.append(jax.ShapeDtypeStruct((m, n), BF16))
    outs = pl.pallas_call(
        functools.partial(_mm_norm_res_kernel, with_next=with_next, with_proj=with_proj),
        grid=(m // tm,),
        in_specs=in_specs,
        out_specs=out_specs,
        out_shape=out_shape,
        compiler_params=_cparams(("parallel",)),
        name="mm_norm_res",
    )(*args)
    return outs if with_next else (outs[0], None)


def _gates_kernel(ba_ref, alog_ref, dtb_ref, o_ref):
    rows = ba_ref.shape[0]
    ba = ba_ref[...]
    beta = jax.nn.sigmoid(ba)
    xs = ba + dtb_ref[...]
    softplus = jnp.maximum(xs, 0.0) + jnp.log1p(jnp.exp(-jnp.abs(xs)))
    g = -jnp.exp(alog_ref[...]) * softplus
    ri = lax.broadcasted_iota(jnp.int32, (rows, rows), 0)
    ci = lax.broadcasted_iota(jnp.int32, (rows, rows), 1)
    same = (ri >> LOG2_CHUNK) == (ci >> LOG2_CHUNK)
    tri = jnp.where(same & (ri >= ci), 1.0, 0.0).astype(F32)
    ones = jnp.where(same, 1.0, 0.0).astype(F32)
    gc = jnp.dot(tri, g, preferred_element_type=F32, precision=lax.Precision.HIGHEST)
    gl = jnp.dot(ones, g, preferred_element_type=F32, precision=lax.Precision.HIGHEST)
    lane = lax.broadcasted_iota(jnp.int32, ba.shape, 1)
    h = GDN_HEADS
    out = jnp.where(lane < h, beta, gc)
    out = jnp.where(lane < 2 * h, out, pltpu.roll(gl - gc, h, axis=1))
    out = jnp.where(lane < 3 * h, out, pltpu.roll(jnp.exp(gl), 2 * h, axis=1))
    o_ref[...] = out


def _gates(ba, alog_l, dtb_l):
    m = ba.shape[0]
    rows = TILES["gate_rows"]
    vec = pl.BlockSpec((1, LANES), lambda i: (0, 0))
    return pl.pallas_call(
        _gates_kernel,
        grid=(m // rows,),
        in_specs=[pl.BlockSpec((rows, LANES), lambda i: (i, 0)), vec, vec],
        out_specs=pl.BlockSpec((rows, LANES), lambda i: (i, 0)),
        out_shape=jax.ShapeDtypeStruct((m, LANES), F32),
        compiler_params=_cparams(("parallel",)),
        name="gdn_gates",
    )(ba, alog_l, dtb_l)


def _gdn_kernel(cd_ref, q_ref, k_ref, v_ref, z_ref, colg_ref, rowg_ref, gn_ref, o_ref,
                lv_ref, s_ref, uv_buf, wq_buf, kd_buf, at_buf, ubuf, obuf, *, n_tiles):
    b = pl.program_id(0)
    hg = pl.program_id(1)
    si = pl.program_id(2)
    R = GDN_ROWS
    hp = HEADS_PER_GROUP
    heads = range(hp)
    sls = [slice(hl * GDN_DK, (hl + 1) * GDN_DK) for hl in heads]
    wslot = si % 2
    rslot = 1 - wslot

    @pl.when(si == 0)
    def _():
        s_ref[...] = jnp.zeros_like(s_ref)
        uv_buf[1] = jnp.zeros(uv_buf.shape[1:], uv_buf.dtype)
        wq_buf[1] = jnp.zeros(wq_buf.shape[1:], wq_buf.dtype)
        kd_buf[1] = jnp.zeros(kd_buf.shape[1:], kd_buf.dtype)
        at_buf[1] = jnp.zeros(at_buf.shape[1:], at_buf.dtype)
        ri = lax.broadcasted_iota(jnp.int32, (R, R), 0)
        ci = lax.broadcasted_iota(jnp.int32, (R, R), 1)
        x = ri ^ ci
        lv = jnp.full((R, R), -1, jnp.int32)
        for t in range(LOG2_CHUNK):
            lv = lv + jnp.where(x >= (1 << t), 1, 0)
        ok = ((ri >> LOG2_CHUNK) == (ci >> LOG2_CHUNK)) & (ri >= ci)
        lv_ref[...] = jnp.where(ok, lv, -2)

    p1 = {}

    def p1_scores():
        p1["qk_kk"] = []
        for hl in heads:
            kb = k_ref[:, sls[hl]]
            p1["qk_kk"].append(lax.dot_general(jnp.concatenate([q_ref[:, sls[hl]], kb], axis=0), kb,
                                               (((1,), (1,)), ((), ())), preferred_element_type=F32))

    def p1_decay():
        lv = lv_ref[...]
        incl = lv >= -1
        colg = colg_ref[...]
        p1["beta"] = [colg[:, hl:hl + 1] for hl in heads]
        p1["exp_g"] = [jnp.exp(colg[:, hp + hl:hp + hl + 1]) for hl in heads]
        p1["low"], p1["xinv"] = [], []
        for hl in heads:
            gap = colg[:, hp + hl:hp + hl + 1] - rowg_ref[hl:hl + 1, :]
            decay = jnp.where(incl, jnp.exp(jnp.where(incl, gap, 0.0)), 0.0)
            at_buf[wslot, hl] = (decay * p1["qk_kk"][hl][0:R]).astype(BF16)
            lo = jnp.where(lv >= 0, p1["beta"][hl] * decay * p1["qk_kk"][hl][R:2 * R], 0.0)
            p1["low"].append(lo)
            p1["xinv"].append(jnp.where(lv == -1, 1.0, 0.0) - jnp.where(lv == 0, lo, 0.0))

    def p1_level_a(level):
        def run():
            lv = lv_ref[...]
            p1["ys"] = [_bdot(jnp.where(lv == level, p1["low"][hl], 0.0), p1["xinv"][hl]) for hl in heads]
        return run

    def p1_level_b():
        p1["xinv"] = [p1["xinv"][hl] - _bdot(p1["xinv"][hl], p1["ys"][hl]) for hl in heads]

    def p1_solve():
        p1["sol"] = []
        for hl in heads:
            bk = (p1["beta"][hl] * p1["exp_g"][hl]) * k_ref[:, sls[hl]].astype(F32)
            rhs = jnp.concatenate([p1["beta"][hl] * v_ref[:, sls[hl]].astype(F32), bk], axis=1)
            p1["sol"].append(_bdot(p1["xinv"][hl], rhs))

    def p1_store():
        colg = colg_ref[...]
        for hl in heads:
            q_dec = q_ref[:, sls[hl]].astype(F32) * p1["exp_g"][hl]
            uv_buf[wslot, hl] = p1["sol"][hl][:, 0:GDN_DV]
            wq_buf[wslot, hl, 0:R, :] = p1["sol"][hl][:, GDN_DV:].astype(BF16)
            wq_buf[wslot, hl, R:2 * R, :] = q_dec.astype(BF16)
            kd_buf[wslot, hl] = (k_ref[:, sls[hl]].astype(F32)
                                 * jnp.exp(colg[:, 2 * hp + hl:2 * hp + hl + 1])).astype(BF16)

    p1_stages = [p1_scores, p1_decay]
    for level in range(1, LOG2_CHUNK):
        p1_stages += [p1_level_a(level), p1_level_b]
    p1_stages += [p1_solve, p1_store]

    p2 = {}
    tile = jnp.maximum(si - 1, 0)

    def p2_read(c):
        def run():
            p2["states"] = [s_ref[hl] for hl in heads]
            p2["wqs"] = []
            for hl in heads:
                lhs = jnp.concatenate([wq_buf[rslot, hl, c * CHUNK:(c + 1) * CHUNK, :],
                                       wq_buf[rslot, hl, R + c * CHUNK:R + (c + 1) * CHUNK, :]], axis=0)
                p2["wqs"].append(jnp.dot(lhs, p2["states"][hl].astype(BF16),
                                         preferred_element_type=F32))
        return run

    def p2_update(c):
        def run():
            rs = slice(c * CHUNK, (c + 1) * CHUNK)
            for hl in heads:
                u = (uv_buf[rslot, hl, rs, :] - p2["wqs"][hl][0:CHUNK]).astype(BF16)
                ubuf[hl, rs, :] = u
                obuf[hl, rs, :] = p2["wqs"][hl][CHUNK:]
                cd = cd_ref[b * GDN_HEADS + hg * HEADS_PER_GROUP + hl, tile * CHUNKS_PER_STEP + c]
                upd = lax.dot_general(kd_buf[rslot, hl, rs, :], u, (((0,), (0,)), ((), ())),
                                      preferred_element_type=F32)
                s_ref[hl] = cd * p2["states"][hl] + upd
        return run

    def p2_output():
        for hl in heads:
            o = obuf[hl] + jnp.dot(at_buf[rslot, hl], ubuf[hl], preferred_element_type=F32)
            on = _rms(o, gn_ref[...]) * z_ref[:, sls[hl]].astype(F32)
            o_ref[:, sls[hl]] = on.astype(o_ref.dtype)

    p2_stages = []
    for c in range(CHUNKS_PER_STEP):
        p2_stages += [p2_read(c), p2_update(c)]
    p2_stages.append(p2_output)

    n1, n2 = len(p1_stages), len(p2_stages)
    i2 = 0
    for i1, stage in enumerate(p1_stages):
        stage()
        while i2 < n2 and (i2 + 1) * n1 <= (i1 + 1) * n2:
            p2_stages[i2]()
            i2 += 1
    while i2 < n2:
        p2_stages[i2]()
        i2 += 1


def _gdn(qkvz, colg4, rowg4, cdecay, gdn_norm, batch, seq):
    m = qkvz.shape[0]
    ns = seq // GDN_ROWS
    gw = GROUP_W
    R = GDN_ROWS

    def cur(s):
        return jnp.minimum(s, ns - 1)

    def prv(s):
        return jnp.maximum(s - 1, 0)

    def xspec(col0):
        return pl.BlockSpec((R, gw), lambda b, h, s: (b * ns + cur(s), col0 // gw + h))

    hp = HEADS_PER_GROUP
    return pl.pallas_call(
        functools.partial(_gdn_kernel, n_tiles=ns),
        grid=(batch, HEAD_GROUPS, ns + 1),
        in_specs=[
            pl.BlockSpec(memory_space=pltpu.SMEM),
            xspec(0), xspec(QK_W), xspec(2 * QK_W),
            pl.BlockSpec((R, gw), lambda b, h, s: (b * ns + prv(s), COL_Z // gw + h)),
            pl.BlockSpec((None, R, LANES), lambda b, h, s: (h, b * ns + cur(s), 0)),
            pl.BlockSpec((None, None, hp, R), lambda b, h, s: (b, h, 0, cur(s))),
            pl.BlockSpec((1, GDN_DV), lambda b, h, s: (0, 0)),
        ],
        out_specs=pl.BlockSpec((R, gw), lambda b, h, s: (b * ns + prv(s), h)),
        out_shape=jax.ShapeDtypeStruct((m, V_W), BF16),
        scratch_shapes=[
            pltpu.VMEM((R, R), jnp.int32),
            pltpu.VMEM((hp, GDN_DK, GDN_DV), F32),
            pltpu.VMEM((2, hp, R, GDN_DV), F32),
            pltpu.VMEM((2, hp, 2 * R, GDN_DK), BF16),
            pltpu.VMEM((2, hp, R, GDN_DK), BF16),
            pltpu.VMEM((2, hp, R, R), BF16),
            pltpu.VMEM((hp, R, GDN_DV), BF16),
            pltpu.VMEM((hp, R, GDN_DV), F32),
        ],
        compiler_params=_cparams(("parallel", "parallel", "arbitrary")),
        name="gdn",
    )(cdecay, qkvz, qkvz, qkvz, qkvz, colg4, rowg4, gdn_norm.reshape(1, GDN_DV))


POOL_HALO = 16


def _pool_kernel(p_ref, w_ref, sc_ref, o_ref, buf):
    si = pl.program_id(1)
    R = p_ref.shape[0]

    @pl.when(si == 0)
    def _():
        buf[0:POOL_HALO, :] = jnp.zeros((POOL_HALO, POOL_WIDTH), F32)

    buf[POOL_HALO:POOL_HALO + R, :] = p_ref[...].astype(F32)
    t = si * R + lax.broadcasted_iota(jnp.int32, (R, 1), 0)
    for gi, win in enumerate(POOL_WINDOWS):
        sl = slice(gi * POOL_GROUP, (gi + 1) * POOL_GROUP)
        x = buf[POOL_HALO:POOL_HALO + R, sl]
        acc = x
        for d in range(1, win):
            acc = acc + buf[POOL_HALO - d:POOL_HALO - d + R, sl]
        cnt = jnp.minimum(t + 1, win).astype(F32)
        y = acc / cnt - x
        o = _bdot(y, w_ref[gi]) * sc_ref[:, sl]
        o_ref[:, sl] = o.astype(o_ref.dtype)
    buf[0:POOL_HALO, :] = buf[R:R + POOL_HALO, :]


def _pool(gate, pool_w, pool_scale, batch, seq):
    m = gate.shape[0]
    rows = TILES["pool_rows"]
    ns = seq // rows
    return pl.pallas_call(
        _pool_kernel,
        grid=(batch, ns),
        in_specs=[
            pl.BlockSpec((rows, POOL_WIDTH), lambda b, s: (b * ns + s, GCOL_P // POOL_WIDTH)),
            pl.BlockSpec((len(POOL_WINDOWS), POOL_GROUP, POOL_GROUP), lambda b, s: (0, 0, 0)),
            pl.BlockSpec((1, POOL_WIDTH), lambda b, s: (0, 0)),
        ],
        out_specs=pl.BlockSpec((rows, POOL_WIDTH), lambda b, s: (b * ns + s, 0)),
        out_shape=jax.ShapeDtypeStruct((m, POOL_WIDTH), BF16),
        scratch_shapes=[pltpu.VMEM((rows + POOL_HALO, POOL_WIDTH), F32)],
        compiler_params=_cparams(("parallel", "arbitrary")),
        name="pool",
    )(gate, pool_w, pool_scale.reshape(1, POOL_WIDTH))


def _merge_kernel(a_ref, p_ref, wa_ref, wb_ref, ga_ref, gb_ref, o_ref):
    tm, tn = o_ref.shape
    rb = PROJ_ROWS
    for c in range(tn // PROJ_SUB):
        cs = slice(c * PROJ_SUB, (c + 1) * PROJ_SUB)
        for r in range(tm // rb):
            rs = slice(r * rb, (r + 1) * rb)
            ya = jnp.dot(a_ref[rs, :], wa_ref[:, cs], preferred_element_type=F32)
            yb = jnp.dot(p_ref[rs, :], wb_ref[:, cs], preferred_element_type=F32)
            merged = ga_ref[rs, cs].astype(F32) * ya + gb_ref[rs, cs].astype(F32) * yb
            o_ref[rs, cs] = merged.astype(o_ref.dtype)


def _merge(og, pooled, wa, wb, gate):
    m = og.shape[0]
    n = wa.shape[1]
    tm, tn = TILES["merge_tm"], TILES["merge_tn"]
    return pl.pallas_call(
        _merge_kernel,
        grid=(m // tm, n // tn),
        in_specs=[
            pl.BlockSpec((tm, V_W), lambda i, j: (i, 0)),
            pl.BlockSpec((tm, POOL_WIDTH), lambda i, j: (i, 0)),
            pl.BlockSpec((V_W, tn), lambda i, j: (0, j)),
            pl.BlockSpec((POOL_WIDTH, tn), lambda i, j: (0, j)),
            pl.BlockSpec((tm, tn), lambda i, j: (i, GCOL_GA // tn + j)),
            pl.BlockSpec((tm, tn), lambda i, j: (i, GCOL_GB // tn + j)),
        ],
        out_specs=pl.BlockSpec((tm, tn), lambda i, j: (i, j)),
        out_shape=jax.ShapeDtypeStruct((m, n), BF16),
        compiler_params=_cparams(("parallel", "arbitrary")),
        name="merge",
    )(og, pooled, wa, wb, gate, gate)


def _xattn_kernel(q_ref, k_ref, v_ref, o_ref):
    s = lax.dot_general(q_ref[...], k_ref[...], (((1,), (1,)), ((), ())),
                        preferred_element_type=F32) * (XA_HEAD_DIM ** -0.5)
    s = s - jnp.max(s, axis=-1, keepdims=True)
    e = jnp.exp(s)
    p = e / jnp.sum(e, axis=-1, keepdims=True)
    o_ref[...] = jnp.dot(p.astype(BF16), v_ref[...], preferred_element_type=F32).astype(o_ref.dtype)


def _xattn(q, kv, batch, seq, mem_len):
    m = q.shape[0]
    ts = TILES["xattn_ts"]
    ns = seq // ts
    return pl.pallas_call(
        _xattn_kernel,
        grid=(batch, ns, XA_HEADS),
        in_specs=[
            pl.BlockSpec((ts, XA_HEAD_DIM), lambda b, s, h: (b * ns + s, h)),
            pl.BlockSpec((mem_len, XA_HEAD_DIM), lambda b, s, h: (b, h)),
            pl.BlockSpec((mem_len, XA_HEAD_DIM), lambda b, s, h: (b, XA_HEADS + h)),
        ],
        out_specs=pl.BlockSpec((ts, XA_HEAD_DIM), lambda b, s, h: (b * ns + s, h)),
        out_shape=jax.ShapeDtypeStruct((m, D_MODEL), BF16),
        compiler_params=_cparams(("parallel", "parallel", "arbitrary")),
        name="xattn",
    )(q, kv, kv)


def _ffn_up_kernel(h_ref, wa_ref, wb_ref, cwa_ref, cwb_ref, ba_ref, bb_ref, o_ref, ubuf, carry,
                   *, tiles_per_seq):
    i = pl.program_id(0)
    j = pl.program_id(1)
    tm, tn = o_ref.shape
    rb = FFN_ROWS
    first = i % tiles_per_seq == 0
    parts = ((wa_ref, cwa_ref, ba_ref), (wb_ref, cwb_ref, bb_ref))
    for c in range(tn // FFN_SUB):
        cs = slice(c * FFN_SUB, (c + 1) * FFN_SUB)
        for t in range(2):
            ubuf[t, 0:SUBLANES, cs] = jnp.where(first, 0.0, carry[t, j, :, cs])
        for r in range(tm // rb):
            base = SUBLANES + r * rb
            halves = []
            for t, (w_ref, cw_ref, b_ref) in enumerate(parts):
                u = jnp.dot(h_ref[r * rb:(r + 1) * rb, :], w_ref[:, cs], preferred_element_type=F32)
                ubuf[t, base:base + rb, cs] = u
                if r == tm // rb - 1:
                    carry[t, j, :, cs] = u[rb - SUBLANES:rb, :]
                cw = cw_ref[:, cs]
                y = u * cw[FFN_CONV - 1:FFN_CONV, :] + b_ref[:, cs]
                for d in range(FFN_CONV - 1):
                    off = base - (FFN_CONV - 1) + d
                    y = y + ubuf[t, off:off + rb, cs] * cw[d:d + 1, :]
                halves.append(y)
            o_ref[r * rb:(r + 1) * rb, cs] = (_silu(halves[0]) * halves[1]).astype(o_ref.dtype)


def _ffn_up(h, wa, wb, cwa, cwb, ba, bb, seq):
    m, k = h.shape
    n = wa.shape[1]
    tm, tn = TILES["ffn_tm"], TILES["ffn_tn"]
    wspec = pl.BlockSpec((k, tn), lambda i, j: (0, j))
    cspec = pl.BlockSpec((FFN_CONV, tn), lambda i, j: (0, j))
    bspec = pl.BlockSpec((1, tn), lambda i, j: (0, j))
    return pl.pallas_call(
        functools.partial(_ffn_up_kernel, tiles_per_seq=seq // tm),
        grid=(m // tm, n // tn),
        in_specs=[pl.BlockSpec((tm, k), lambda i, j: (i, 0)), wspec, wspec, cspec, cspec, bspec, bspec],
        out_specs=pl.BlockSpec((tm, tn), lambda i, j: (i, j)),
        out_shape=jax.ShapeDtypeStruct((m, n), BF16),
        scratch_shapes=[
            pltpu.VMEM((2, tm + SUBLANES, tn), F32),
            pltpu.VMEM((2, n // tn, SUBLANES, tn), F32),
        ],
        compiler_params=_cparams(("arbitrary", "arbitrary")),
        name="ffn_up",
    )(h, wa, wb, cwa, cwb, ba, bb)


def _pad_cols(a, n):
    return jnp.pad(a, ((0, 0), (0, n - a.shape[1])))


def _layer(x2d, mem2d, batch, seq, mem_len, p):
    m = x2d.shape[0]
    h = GDN_HEADS
    hp = HEADS_PER_GROUP
    w_in = p["w_in"]
    o_z = QKV_W
    o_b = o_z + V_W
    o_p = o_b + 2 * h
    w_qkvz = w_in[:, :o_b].astype(BF16)
    w_gate = w_in[:, o_p:].astype(BF16)
    w_ba = _pad_cols(w_in[:, o_b:o_p], LANES).astype(BF16)
    alog_l = jnp.zeros((1, LANES), F32).at[0, h:2 * h].set(p["a_log"])
    dtb_l = jnp.zeros((1, LANES), F32).at[0, h:2 * h].set(p["dt_bias"])
    w_up = p["w_up"]
    wa_up = _pad_cols(w_up[:, :D_FF], D_FF_PAD).astype(BF16)
    wb_up = _pad_cols(w_up[:, D_FF:], D_FF_PAD).astype(BF16)
    cwa = _pad_cols(p["ffn_conv_w"][:, :D_FF], D_FF_PAD)
    cwb = _pad_cols(p["ffn_conv_w"][:, D_FF:], D_FF_PAD)
    fba = _pad_cols(p["ffn_conv_b"][None, :D_FF], D_FF_PAD)
    fbb = _pad_cols(p["ffn_conv_b"][None, D_FF:], D_FF_PAD)
    w_down = jnp.pad(p["w_down"], ((0, D_FF_PAD - D_FF), (0, 0))).astype(BF16)

    qkvz, ba = _qkv_proj(x2d, p["mix_pre_norm"], w_qkvz, p["conv_qkv"], w_ba, seq)
    gate = _gate_proj(x2d, p["mix_pre_norm"], w_gate)
    gates = _gates(ba, alog_l, dtb_l)
    g4 = gates[:, :4 * h].reshape(m, 4, HEAD_GROUPS, hp)
    colg4 = jnp.pad(g4[:, :3].transpose(2, 0, 1, 3).reshape(HEAD_GROUPS, m, 3 * hp),
                    ((0, 0), (0, 0), (0, LANES - 3 * hp)))
    rowg4 = gates[:, h:2 * h].reshape(batch, seq, HEAD_GROUPS, hp).transpose(0, 2, 3, 1)
    cdecay = gates[CHUNK - 1::CHUNK, 3 * h:4 * h].reshape(batch, seq // CHUNK, h)
    cdecay = cdecay.transpose(0, 2, 1).reshape(batch * h, seq // CHUNK)
    og = _gdn(qkvz, colg4, rowg4, cdecay, p["gdn_norm"], batch, seq)
    pooled = _pool(gate, p["pool_w"].astype(BF16), p["pool_scale"], batch, seq)
    merged = _merge(og, pooled, p["w_branch_a"].astype(BF16), p["w_branch_b"].astype(BF16), gate)
    x1, q = _mm_norm_res(merged, p["w_mix_out"].astype(BF16), x2d, p["mix_post_norm"],
                         p["xa_pre_norm"], tm=TILES["res_tm"], next_proj=p["w_xq"].astype(BF16))
    kv = _norm_mm(mem2d, p["mem_norm"], p["w_xkv"].astype(BF16), tm=mem2d.shape[0], tn=1024,
                  out_dtype=BF16)
    xo = _xattn(q, kv, batch, seq, mem_len)
    x2, h3 = _mm_norm_res(xo, p["w_xo"].astype(BF16), x1, p["xa_post_norm"], p["ffn_pre_norm"],
                          tm=TILES["res_tm"])
    act = _ffn_up(h3, wa_up, wb_up, cwa, cwb, fba, fbb, seq)
    x3, _ = _mm_norm_res(act, w_down, x2, p["ffn_post_norm"], None, tm=TILES["down_tm"])
    return x3


def kernel(x, mem, mix_pre_norm, w_in, conv_qkv, a_log, dt_bias, gdn_norm, pool_w, pool_scale,
           w_branch_a, w_branch_b, w_mix_out, mix_post_norm, xa_pre_norm, mem_norm, w_xq, w_xkv,
           w_xo, xa_post_norm, ffn_pre_norm, w_up, ffn_conv_w, ffn_conv_b, w_down, ffn_post_norm):
    params = dict(mix_pre_norm=mix_pre_norm, w_in=w_in, conv_qkv=conv_qkv, a_log=a_log,
                  dt_bias=dt_bias, gdn_norm=gdn_norm, pool_w=pool_w, pool_scale=pool_scale,
                  w_branch_a=w_branch_a, w_branch_b=w_branch_b, w_mix_out=w_mix_out,
                  mix_post_norm=mix_post_norm, xa_pre_norm=xa_pre_norm, mem_norm=mem_norm,
                  w_xq=w_xq, w_xkv=w_xkv, w_xo=w_xo, xa_post_norm=xa_post_norm,
                  ffn_pre_norm=ffn_pre_norm, w_up=w_up, ffn_conv_w=ffn_conv_w,
                  ffn_conv_b=ffn_conv_b, w_down=w_down, ffn_post_norm=ffn_post_norm)
    batch, seq, d = x.shape
    mem_len = mem.shape[1]
    x2d = x.reshape(batch * seq, d)
    mem2d = mem.reshape(batch * mem_len, d)
    for l in range(w_in.shape[0]):
        x2d = _layer(x2d, mem2d, batch, seq, mem_len, {k: v[l] for k, v in params.items()})
    return x2d.reshape(batch, seq, d)
```

```python
import functools

import jax
import jax.numpy as jnp
from jax import lax
from jax.experimental import pallas as pl
from jax.experimental.pallas import tpu as pltpu

F32 = jnp.float32
BF16 = jnp.bfloat16

D_MODEL = 2048
CHUNK = 64
GDN_HEADS = 16
GDN_DK = 128
GDN_DV = 128
GDN_CONV = 4
QK_W = GDN_HEADS * GDN_DK
V_W = GDN_HEADS * GDN_DV
QKV_W = 2 * QK_W + V_W
POOL_WINDOWS = (2, 4, 8, 16)
POOL_WIDTH = D_MODEL // 2
POOL_GROUP = POOL_WIDTH // 4
XA_HEADS = 4
XA_HEAD_DIM = D_MODEL // XA_HEADS
D_FF = 5504
FFN_CONV = 3
EPS = 1e-6

LANES = 128
SUBLANES = 8
VMEM_LIMIT = 56 * 1024 * 1024

COL_Z = QKV_W
QKVZ_W = QKV_W + V_W
GCOL_P = 0
GCOL_GA = POOL_WIDTH
GCOL_GB = GCOL_GA + D_MODEL

HEADS_PER_GROUP = 16
HEAD_GROUPS = GDN_HEADS // HEADS_PER_GROUP
GROUP_W = HEADS_PER_GROUP * GDN_DK
GDN_ROWS = 128
CHUNKS_PER_STEP = GDN_ROWS // CHUNK
LOG2_CHUNK = 6
NEG_BIG = -1e30

D_FF_PAD = 5632
PROJ_SUB = 256
PROJ_ROWS = 128
FFN_ROWS = 1024
FFN_SUB = 512

TILES = dict(
    proj_tm=1024, proj_tn=1024,
    gate_rows=512,
    pool_rows=512,
    merge_tm=1024, merge_tn=512,
    res_tm=512,
    xattn_ts=1024,
    ffn_tm=1024, ffn_tn=512,
    down_tm=256,
)


def _cparams(sem):
    return pltpu.CompilerParams(dimension_semantics=sem, vmem_limit_bytes=VMEM_LIMIT)


def _rms(xf, w):
    ms = jnp.mean(xf * xf, axis=-1, keepdims=True)
    return xf * lax.rsqrt(ms + EPS) * w


def _silu(x):
    return x * jax.nn.sigmoid(x)


def _bdot(a, b):
    return jnp.dot(a.astype(BF16), b.astype(BF16), preferred_element_type=F32)


def _causal_conv(u, prev, ubuf, cw):
    tm = u.shape[0]
    width = cw.shape[0]
    ubuf[0:SUBLANES, :] = prev
    ubuf[SUBLANES:SUBLANES + tm, :] = u
    y = u * cw[width - 1:width, :]
    for d in range(width - 1):
        off = SUBLANES - (width - 1) + d
        y = y + ubuf[off:off + tm, :] * cw[d:d + 1, :]
    return y


def _qkv_proj_kernel(x_ref, nw_ref, w_ref, cw_ref, wba_ref, o_ref, ba_ref, h_ref, ubuf, carry, *,
                     tiles_per_seq, tiles_per_part):
    i = pl.program_id(0)
    j = pl.program_id(1)
    tm, tn = o_ref.shape

    @pl.when(j == 0)
    def _():
        h_ref[...] = _rms(x_ref[...], nw_ref[...]).astype(BF16)
        ba_ref[...] = jnp.dot(h_ref[...], wba_ref[...], preferred_element_type=F32)

    is_qk = j < 2 * tiles_per_part
    scale = jnp.where(j < tiles_per_part, GDN_DK ** -0.5, 1.0)
    first = i % tiles_per_seq == 0

    @pl.when(j < 3 * tiles_per_part)
    def _():
        rb = PROJ_ROWS
        for c in range(tn // PROJ_SUB):
            cs = slice(c * PROJ_SUB, (c + 1) * PROJ_SUB)
            cw = cw_ref[:, cs]
            ubuf[0:SUBLANES, cs] = jnp.where(first, 0.0, carry[j, :, cs])
            for r in range(tm // rb):
                base = SUBLANES + r * rb
                u = jnp.dot(h_ref[r * rb:(r + 1) * rb, :], w_ref[:, cs], preferred_element_type=F32)
                ubuf[base:base + rb, cs] = u
                if r == tm // rb - 1:
                    carry[j, :, cs] = u[rb - SUBLANES:rb, :]
                y = u * cw[GDN_CONV - 1:GDN_CONV, :]
                for d in range(GDN_CONV - 1):
                    off = base - (GDN_CONV - 1) + d
                    y = y + ubuf[off:off + rb, cs] * cw[d:d + 1, :]
                y = _silu(y)
                for hb in range(PROJ_SUB // GDN_DK):
                    yh = y[:, hb * GDN_DK:(hb + 1) * GDN_DK]
                    ss = jnp.sum(yh * yh, axis=-1, keepdims=True)
                    f = jnp.where(is_qk, lax.rsqrt(ss + EPS) * scale, 1.0)
                    o_ref[r * rb:(r + 1) * rb, c * PROJ_SUB + hb * GDN_DK:c * PROJ_SUB + (hb + 1) * GDN_DK] = (
                        yh * f).astype(o_ref.dtype)

    @pl.when(j >= 3 * tiles_per_part)
    def _():
        rb = PROJ_ROWS
        for c in range(tn // PROJ_SUB):
            cs = slice(c * PROJ_SUB, (c + 1) * PROJ_SUB)
            for r in range(tm // rb):
                rs = slice(r * rb, (r + 1) * rb)
                u = jnp.dot(h_ref[rs, :], w_ref[:, cs], preferred_element_type=F32)
                o_ref[rs, cs] = _silu(u).astype(o_ref.dtype)


def _qkv_proj(x, nw, w, conv_w, w_ba, seq):
    m, k = x.shape
    n = w.shape[1]
    tm, tn = TILES["proj_tm"], TILES["proj_tn"]
    return pl.pallas_call(
        functools.partial(_qkv_proj_kernel, tiles_per_seq=seq // tm, tiles_per_part=QK_W // tn),
        grid=(m // tm, n // tn),
        in_specs=[
            pl.BlockSpec((tm, k), lambda i, j: (i, 0)),
            pl.BlockSpec((1, k), lambda i, j: (0, 0)),
            pl.BlockSpec((k, tn), lambda i, j: (0, j)),
            pl.BlockSpec((GDN_CONV, tn), lambda i, j: (0, jnp.minimum(j, QKV_W // tn - 1))),
            pl.BlockSpec((k, LANES), lambda i, j: (0, 0)),
        ],
        out_specs=[pl.BlockSpec((tm, tn), lambda i, j: (i, j)),
                   pl.BlockSpec((tm, LANES), lambda i, j: (i, 0)),
                   pl.BlockSpec((tm, k), lambda i, j: (i, 0))],
        out_shape=[jax.ShapeDtypeStruct((m, n), BF16), jax.ShapeDtypeStruct((m, LANES), F32),
                   jax.ShapeDtypeStruct((m, k), BF16)],
        scratch_shapes=[
            pltpu.VMEM((tm + SUBLANES, tn), F32),
            pltpu.VMEM((n // tn, SUBLANES, tn), F32),
        ],
        compiler_params=_cparams(("arbitrary", "arbitrary")),
        name="qkv_proj",
    )(x, nw.reshape(1, k), w, conv_w, w_ba)


def _gate_proj_kernel(h_ref, w_ref, o_ref, *, pool_tiles):
    j = pl.program_id(1)
    is_pool = j < pool_tiles
    tm, tn = o_ref.shape
    rb = PROJ_ROWS
    for c in range(tn // PROJ_SUB):
        cs = slice(c * PROJ_SUB, (c + 1) * PROJ_SUB)
        for r in range(tm // rb):
            rs = slice(r * rb, (r + 1) * rb)
            u = jnp.dot(h_ref[rs, :], w_ref[:, cs], preferred_element_type=F32)
            out = jnp.where(is_pool, u, jax.nn.sigmoid(u))
            o_ref[rs, cs] = out.astype(o_ref.dtype)


def _gate_proj(h, w):
    m, k = h.shape
    n = w.shape[1]
    tm, tn = TILES["proj_tm"], TILES["proj_tn"]
    return pl.pallas_call(
        functools.partial(_gate_proj_kernel, pool_tiles=POOL_WIDTH // tn),
        grid=(m // tm, n // tn),
        in_specs=[
            pl.BlockSpec((tm, k), lambda i, j: (i, 0)),
            pl.BlockSpec((k, tn), lambda i, j: (0, j)),
        ],
        out_specs=pl.BlockSpec((tm, tn), lambda i, j: (i, j)),
        out_shape=jax.ShapeDtypeStruct((m, n), BF16),
        compiler_params=_cparams(("parallel", "arbitrary")),
        name="gate_proj",
    )(h, w)


def _norm_mm_kernel(x_ref, nw_ref, w_ref, o_ref, h_ref):
    @pl.when(pl.program_id(1) == 0)
    def _():
        h_ref[...] = _rms(x_ref[...], nw_ref[...]).astype(BF16)

    o_ref[...] = jnp.dot(h_ref[...], w_ref[...], preferred_element_type=F32).astype(o_ref.dtype)


def _norm_mm(x, nw, w, *, tm, tn, out_dtype):
    m, k = x.shape
    n = w.shape[1]
    return pl.pallas_call(
        _norm_mm_kernel,
        grid=(m // tm, n // tn),
        in_specs=[
            pl.BlockSpec((tm, k), lambda i, j: (i, 0)),
            pl.BlockSpec((1, k), lambda i, j: (0, 0)),
            pl.BlockSpec((k, tn), lambda i, j: (0, j)),
        ],
        out_specs=pl.BlockSpec((tm, tn), lambda i, j: (i, j)),
        out_shape=jax.ShapeDtypeStruct((m, n), out_dtype),
        scratch_shapes=[pltpu.VMEM((tm, k), BF16)],
        compiler_params=_cparams(("parallel", "arbitrary")),
        name="norm_mm",
    )(x, nw.reshape(1, k), w)


def _mm_norm_res_kernel(a_ref, w_ref, res_ref, pw_ref, *rest, with_next, with_proj):
    rest = list(rest)
    nw_ref = rest.pop(0) if with_next else None
    w2_ref = rest.pop(0) if with_proj else None
    x_ref = rest.pop(0)
    y = jnp.dot(a_ref[...], w_ref[...], preferred_element_type=F32)
    xn = res_ref[...] + _rms(y, pw_ref[...])
    x_ref[...] = xn
    if with_next:
        h = _rms(xn, nw_ref[...]).astype(BF16)
        if with_proj:
            h = jnp.dot(h, w2_ref[...], preferred_element_type=F32).astype(BF16)
        rest[0][...] = h


def _mm_norm_res(a, w, res, post_w, next_w, *, tm, next_proj=None):
    m, k = a.shape
    n = w.shape[1]
    with_next = next_w is not None
    with_proj = next_proj is not None
    row = pl.BlockSpec((tm, n), lambda i: (i, 0))
    vec = pl.BlockSpec((1, n), lambda i: (0, 0))
    wspec = pl.BlockSpec((k, n), lambda i: (0, 0), pipeline_mode=pl.Buffered(1))
    in_specs = [pl.BlockSpec((tm, k), lambda i: (i, 0)), wspec, row, vec]
    args = [a, w, res, post_w.reshape(1, n)]
    out_specs = [row]
    out_shape = [jax.ShapeDtypeStruct((m, n), F32)]
    if with_next:
        in_specs.append(vec)
        args.append(next_w.reshape(1, n))
        if with_proj:
            in_specs.append(pl.BlockSpec(next_proj.shape, lambda i: (0, 0), pipeline_mode=pl.Buffered(1)))
            args.append(next_proj)
        out_specs.append(row)
        out_shape.append(jax.ShapeDtypeStruct((m, n), BF16))
    outs = pl.pallas_call(
        functools.partial(_mm_norm_res_kernel, with_next=with_next, with_proj=with_proj),
        grid=(m // tm,),
        in_specs=in_specs,
        out_specs=out_specs,
        out_shape=out_shape,
        compiler_params=_cparams(("parallel",)),
        name="mm_norm_res",
    )(*args)
    return outs if with_next else (outs[0], None)


def _gates_kernel(ba_ref, alog_ref, dtb_ref, o_ref):
    rows = ba_ref.shape[0]
    ba = ba_ref[...]
    beta = jax.nn.sigmoid(ba)
    xs = ba + dtb_ref[...]
    softplus = jnp.maximum(xs, 0.0) + jnp.log1p(jnp.exp(-jnp.abs(xs)))
    g = -jnp.exp(alog_ref[...]) * softplus
    ri = lax.broadcasted_iota(jnp.int32, (CHUNK, CHUNK), 0)
    ci = lax.broadcasted_iota(jnp.int32, (CHUNK, CHUNK), 1)
    tri = jnp.where(ri >= ci, 1.0, 0.0).astype(F32)
    ones = jnp.ones((CHUNK, CHUNK), F32)
    gcs, gls = [], []
    for c in range(rows // CHUNK):
        gch = g[c * CHUNK:(c + 1) * CHUNK]
        gcs.append(jnp.dot(tri, gch, preferred_element_type=F32, precision=lax.Precision.HIGHEST))
        gls.append(jnp.dot(ones, gch, preferred_element_type=F32, precision=lax.Precision.HIGHEST))
    gc = jnp.concatenate(gcs, axis=0)
    gl = jnp.concatenate(gls, axis=0)
    lane = lax.broadcasted_iota(jnp.int32, ba.shape, 1)
    h = GDN_HEADS
    out = jnp.where(lane < h, beta, gc)
    out = jnp.where(lane < 2 * h, out, pltpu.roll(gl - gc, h, axis=1))
    out = jnp.where(lane < 3 * h, out, pltpu.roll(jnp.exp(gl), 2 * h, axis=1))
    o_ref[...] = out


def _gates(ba, alog_l, dtb_l):
    m = ba.shape[0]
    rows = TILES["gate_rows"]
    vec = pl.BlockSpec((1, LANES), lambda i: (0, 0))
    return pl.pallas_call(
        _gates_kernel,
        grid=(m // rows,),
        in_specs=[pl.BlockSpec((rows, LANES), lambda i: (i, 0)), vec, vec],
        out_specs=pl.BlockSpec((rows, LANES), lambda i: (i, 0)),
        out_shape=jax.ShapeDtypeStruct((m, LANES), F32),
        compiler_params=_cparams(("parallel",)),
        name="gdn_gates",
    )(ba, alog_l, dtb_l)


def _gdn_kernel(cd_ref, q_ref, k_ref, v_ref, z_ref, colg_ref, rowg_ref, gn_ref, o_ref,
                lv_ref, lmask, s_ref, uv_buf, wq_buf, kd_buf, at_buf, ubuf, obuf, *, n_tiles):
    b = pl.program_id(0)
    hg = pl.program_id(1)
    si = pl.program_id(2)
    R = GDN_ROWS
    hp = HEADS_PER_GROUP
    heads = range(hp)
    sls = [slice(hl * GDN_DK, (hl + 1) * GDN_DK) for hl in heads]
    wslot = si % 2
    rslot = 1 - wslot

    @pl.when(si == 0)
    def _():
        s_ref[...] = jnp.zeros_like(s_ref)
        uv_buf[1] = jnp.zeros(uv_buf.shape[1:], uv_buf.dtype)
        wq_buf[1] = jnp.zeros(wq_buf.shape[1:], wq_buf.dtype)
        kd_buf[1] = jnp.zeros(kd_buf.shape[1:], kd_buf.dtype)
        at_buf[1] = jnp.zeros(at_buf.shape[1:], at_buf.dtype)
        ri = lax.broadcasted_iota(jnp.int32, (R, R), 0)
        ci = lax.broadcasted_iota(jnp.int32, (R, R), 1)
        x = ri ^ ci
        lv = jnp.full((R, R), -1, jnp.int32)
        for t in range(LOG2_CHUNK):
            lv = lv + jnp.where(x >= (1 << t), 1, 0)
        ok = ((ri >> LOG2_CHUNK) == (ci >> LOG2_CHUNK)) & (ri >= ci)
        lv = jnp.where(ok, lv, -2)
        lv_ref[...] = lv
        for level in range(1, LOG2_CHUNK):
            lmask[level - 1] = jnp.where(lv == level, 1.0, 0.0).astype(BF16)

    p1 = {}

    def p1_scores():
        p1["qk_kk"] = []
        for hl in heads:
            kb = k_ref[:, sls[hl]]
            p1["qk_kk"].append(lax.dot_general(jnp.concatenate([q_ref[:, sls[hl]], kb], axis=0), kb,
                                               (((1,), (1,)), ((), ())), preferred_element_type=F32))

    def p1_decay():
        lv = lv_ref[...]
        incl = lv >= -1
        colg = colg_ref[...]
        p1["beta"] = [colg[:, hl:hl + 1] for hl in heads]
        p1["exp_g"] = [jnp.exp(colg[:, hp + hl:hp + hl + 1]) for hl in heads]
        p1["low"], p1["xinv"] = [], []
        for hl in heads:
            gap = colg[:, hp + hl:hp + hl + 1] - rowg_ref[hl:hl + 1, :]
            decay = jnp.exp(jnp.where(incl, gap, NEG_BIG))
            at_buf[wslot, hl] = (decay * p1["qk_kk"][hl][0:R]).astype(BF16)
            lo = jnp.where(lv >= 0, p1["beta"][hl] * decay * p1["qk_kk"][hl][R:2 * R], 0.0)
            p1["low"].append(lo.astype(BF16))
            p1["xinv"].append((jnp.where(lv == -1, 1.0, 0.0) - jnp.where(lv == 0, lo, 0.0)).astype(BF16))

    def p1_level_a(level):
        def run():
            p1["ys"] = [jnp.dot(p1["low"][hl] * lmask[level - 1], p1["xinv"][hl],
                                preferred_element_type=F32).astype(BF16) for hl in heads]
        return run

    def p1_level_b():
        p1["xinv"] = [p1["xinv"][hl] - jnp.dot(p1["xinv"][hl], p1["ys"][hl],
                                               preferred_element_type=F32).astype(BF16) for hl in heads]

    def p1_solve():
        p1["sol"] = []
        for hl in heads:
            bk = (p1["beta"][hl] * p1["exp_g"][hl]) * k_ref[:, sls[hl]].astype(F32)
            rhs = jnp.concatenate([p1["beta"][hl] * v_ref[:, sls[hl]].astype(F32), bk], axis=1)
            p1["sol"].append(jnp.dot(p1["xinv"][hl], rhs.astype(BF16), preferred_element_type=F32))

    def p1_store():
        colg = colg_ref[...]
        for hl in heads:
            q_dec = q_ref[:, sls[hl]].astype(F32) * p1["exp_g"][hl]
            uv_buf[wslot, hl] = p1["sol"][hl][:, 0:GDN_DV]
            wq_buf[wslot, hl, 0:R, :] = p1["sol"][hl][:, GDN_DV:].astype(BF16)
            wq_buf[wslot, hl, R:2 * R, :] = q_dec.astype(BF16)
            kd_buf[wslot, hl] = (k_ref[:, sls[hl]].astype(F32)
                                 * jnp.exp(colg[:, 2 * hp + hl:2 * hp + hl + 1])).astype(BF16)

    p1_stages = [p1_scores, p1_decay]
    for level in range(1, LOG2_CHUNK):
        p1_stages += [p1_level_a(level), p1_level_b]
    p1_stages += [p1_solve, p1_store]

    p2 = {}
    tile = jnp.maximum(si - 1, 0)

    def p2_read(c):
        def run():
            p2["states"] = [s_ref[hl] for hl in heads]
            p2["wqs"] = []
            for hl in heads:
                lhs = jnp.concatenate([wq_buf[rslot, hl, c * CHUNK:(c + 1) * CHUNK, :],
                                       wq_buf[rslot, hl, R + c * CHUNK:R + (c + 1) * CHUNK, :]], axis=0)
                p2["wqs"].append(jnp.dot(lhs, p2["states"][hl].astype(BF16),
                                         preferred_element_type=F32))
        return run

    def p2_update(c):
        def run():
            rs = slice(c * CHUNK, (c + 1) * CHUNK)
            for hl in heads:
                u = (uv_buf[rslot, hl, rs, :] - p2["wqs"][hl][0:CHUNK]).astype(BF16)
                ubuf[hl, rs, :] = u
                obuf[hl, rs, :] = p2["wqs"][hl][CHUNK:]
                cd = cd_ref[b * GDN_HEADS + hg * HEADS_PER_GROUP + hl, tile * CHUNKS_PER_STEP + c]
                upd = lax.dot_general(kd_buf[rslot, hl, rs, :], u, (((0,), (0,)), ((), ())),
                                      preferred_element_type=F32)
                s_ref[hl] = cd * p2["states"][hl] + upd
        return run

    def p2_output():
        for hl in heads:
            o = obuf[hl] + jnp.dot(at_buf[rslot, hl], ubuf[hl], preferred_element_type=F32)
            on = _rms(o, gn_ref[...]) * z_ref[:, sls[hl]].astype(F32)
            o_ref[:, sls[hl]] = on.astype(o_ref.dtype)

    p2_stages = []
    for c in range(CHUNKS_PER_STEP):
        p2_stages += [p2_read(c), p2_update(c)]
    p2_stages.append(p2_output)

    n1, n2 = len(p1_stages), len(p2_stages)
    i2 = 0
    for i1, stage in enumerate(p1_stages):
        stage()
        while i2 < n2 and (i2 + 1) * n1 <= (i1 + 1) * n2:
            p2_stages[i2]()
            i2 += 1
    while i2 < n2:
        p2_stages[i2]()
        i2 += 1


def _gdn(qkvz, colg4, rowg4, cdecay, gdn_norm, batch, seq):
    m = qkvz.shape[0]
    ns = seq // GDN_ROWS
    gw = GROUP_W
    R = GDN_ROWS

    def cur(s):
        return jnp.minimum(s, ns - 1)

    def prv(s):
        return jnp.maximum(s - 1, 0)

    def xspec(col0):
        return pl.BlockSpec((R, gw), lambda b, h, s: (b * ns + cur(s), col0 // gw + h))

    hp = HEADS_PER_GROUP
    return pl.pallas_call(
        functools.partial(_gdn_kernel, n_tiles=ns),
        grid=(batch, HEAD_GROUPS, ns + 1),
        in_specs=[
            pl.BlockSpec(memory_space=pltpu.SMEM),
            xspec(0), xspec(QK_W), xspec(2 * QK_W),
            pl.BlockSpec((R, gw), lambda b, h, s: (b * ns + prv(s), COL_Z // gw + h)),
            pl.BlockSpec((None, R, LANES), lambda b, h, s: (h, b * ns + cur(s), 0)),
            pl.BlockSpec((None, None, hp, R), lambda b, h, s: (b, h, 0, cur(s))),
            pl.BlockSpec((1, GDN_DV), lambda b, h, s: (0, 0)),
        ],
        out_specs=pl.BlockSpec((R, gw), lambda b, h, s: (b * ns + prv(s), h)),
        out_shape=jax.ShapeDtypeStruct((m, V_W), BF16),
        scratch_shapes=[
            pltpu.VMEM((R, R), jnp.int32),
            pltpu.VMEM((LOG2_CHUNK - 1, R, R), BF16),
            pltpu.VMEM((hp, GDN_DK, GDN_DV), F32),
            pltpu.VMEM((2, hp, R, GDN_DV), F32),
            pltpu.VMEM((2, hp, 2 * R, GDN_DK), BF16),
            pltpu.VMEM((2, hp, R, GDN_DK), BF16),
            pltpu.VMEM((2, hp, R, R), BF16),
            pltpu.VMEM((hp, R, GDN_DV), BF16),
            pltpu.VMEM((hp, R, GDN_DV), F32),
        ],
        compiler_params=_cparams(("parallel", "parallel", "arbitrary")),
        name="gdn",
    )(cdecay, qkvz, qkvz, qkvz, qkvz, colg4, rowg4, gdn_norm.reshape(1, GDN_DV))


POOL_HALO = 16


def _pool_kernel(p_ref, w_ref, sc_ref, o_ref, buf):
    si = pl.program_id(1)
    R = p_ref.shape[0]

    @pl.when(si == 0)
    def _():
        buf[0:POOL_HALO, :] = jnp.zeros((POOL_HALO, POOL_WIDTH), F32)

    buf[POOL_HALO:POOL_HALO + R, :] = p_ref[...].astype(F32)
    t = si * R + lax.broadcasted_iota(jnp.int32, (R, 1), 0)
    for gi, win in enumerate(POOL_WINDOWS):
        sl = slice(gi * POOL_GROUP, (gi + 1) * POOL_GROUP)
        x = buf[POOL_HALO:POOL_HALO + R, sl]
        acc = x
        for d in range(1, win):
            acc = acc + buf[POOL_HALO - d:POOL_HALO - d + R, sl]
        cnt = jnp.minimum(t + 1, win).astype(F32)
        y = acc / cnt - x
        o = _bdot(y, w_ref[gi]) * sc_ref[:, sl]
        o_ref[:, sl] = o.astype(o_ref.dtype)
    buf[0:POOL_HALO, :] = buf[R:R + POOL_HALO, :]


def _pool(gate, pool_w, pool_scale, batch, seq):
    m = gate.shape[0]
    rows = TILES["pool_rows"]
    ns = seq // rows
    return pl.pallas_call(
        _pool_kernel,
        grid=(batch, ns),
        in_specs=[
            pl.BlockSpec((rows, POOL_WIDTH), lambda b, s: (b * ns + s, GCOL_P // POOL_WIDTH)),
            pl.BlockSpec((len(POOL_WINDOWS), POOL_GROUP, POOL_GROUP), lambda b, s: (0, 0, 0)),
            pl.BlockSpec((1, POOL_WIDTH), lambda b, s: (0, 0)),
        ],
        out_specs=pl.BlockSpec((rows, POOL_WIDTH), lambda b, s: (b * ns + s, 0)),
        out_shape=jax.ShapeDtypeStruct((m, POOL_WIDTH), BF16),
        scratch_shapes=[pltpu.VMEM((rows + POOL_HALO, POOL_WIDTH), F32)],
        compiler_params=_cparams(("parallel", "arbitrary")),
        name="pool",
    )(gate, pool_w, pool_scale.reshape(1, POOL_WIDTH))


def _merge_kernel(a_ref, p_ref, wa_ref, wb_ref, ga_ref, gb_ref, o_ref):
    tm, tn = o_ref.shape
    rb = PROJ_ROWS
    for c in range(tn // PROJ_SUB):
        cs = slice(c * PROJ_SUB, (c + 1) * PROJ_SUB)
        for r in range(tm // rb):
            rs = slice(r * rb, (r + 1) * rb)
            ya = jnp.dot(a_ref[rs, :], wa_ref[:, cs], preferred_element_type=F32)
            yb = jnp.dot(p_ref[rs, :], wb_ref[:, cs], preferred_element_type=F32)
            merged = ga_ref[rs, cs].astype(F32) * ya + gb_ref[rs, cs].astype(F32) * yb
            o_ref[rs, cs] = merged.astype(o_ref.dtype)


def _merge(og, pooled, wa, wb, gate):
    m = og.shape[0]
    n = wa.shape[1]
    tm, tn = TILES["merge_tm"], TILES["merge_tn"]
    return pl.pallas_call(
        _merge_kernel,
        grid=(m // tm, n // tn),
        in_specs=[
            pl.BlockSpec((tm, V_W), lambda i, j: (i, 0)),
            pl.BlockSpec((tm, POOL_WIDTH), lambda i, j: (i, 0)),
            pl.BlockSpec((V_W, tn), lambda i, j: (0, j)),
            pl.BlockSpec((POOL_WIDTH, tn), lambda i, j: (0, j)),
            pl.BlockSpec((tm, tn), lambda i, j: (i, GCOL_GA // tn + j)),
            pl.BlockSpec((tm, tn), lambda i, j: (i, GCOL_GB // tn + j)),
        ],
        out_specs=pl.BlockSpec((tm, tn), lambda i, j: (i, j)),
        out_shape=jax.ShapeDtypeStruct((m, n), BF16),
        compiler_params=_cparams(("parallel", "arbitrary")),
        name="merge",
    )(og, pooled, wa, wb, gate, gate)


def _xattn_kernel(q_ref, k_ref, v_ref, o_ref):
    s = lax.dot_general(q_ref[...], k_ref[...], (((1,), (1,)), ((), ())),
                        preferred_element_type=F32) * (XA_HEAD_DIM ** -0.5)
    s = s - jnp.max(s, axis=-1, keepdims=True)
    e = jnp.exp(s)
    p = e / jnp.sum(e, axis=-1, keepdims=True)
    o_ref[...] = jnp.dot(p.astype(BF16), v_ref[...], preferred_element_type=F32).astype(o_ref.dtype)


def _xattn(q, kv, batch, seq, mem_len):
    m = q.shape[0]
    ts = TILES["xattn_ts"]
    ns = seq // ts
    return pl.pallas_call(
        _xattn_kernel,
        grid=(batch, ns, XA_HEADS),
        in_specs=[
            pl.BlockSpec((ts, XA_HEAD_DIM), lambda b, s, h: (b * ns + s, h)),
            pl.BlockSpec((mem_len, XA_HEAD_DIM), lambda b, s, h: (b, h)),
            pl.BlockSpec((mem_len, XA_HEAD_DIM), lambda b, s, h: (b, XA_HEADS + h)),
        ],
        out_specs=pl.BlockSpec((ts, XA_HEAD_DIM), lambda b, s, h: (b * ns + s, h)),
        out_shape=jax.ShapeDtypeStruct((m, D_MODEL), BF16),
        compiler_params=_cparams(("parallel", "parallel", "arbitrary")),
        name="xattn",
    )(q, kv, kv)


def _ffn_up_kernel(h_ref, wa_ref, wb_ref, cwa_ref, cwb_ref, ba_ref, bb_ref, o_ref, ubuf, carry,
                   *, tiles_per_seq):
    i = pl.program_id(0)
    j = pl.program_id(1)
    tm, tn = o_ref.shape
    rb = FFN_ROWS
    first = i % tiles_per_seq == 0
    parts = ((wa_ref, cwa_ref, ba_ref), (wb_ref, cwb_ref, bb_ref))
    for c in range(tn // FFN_SUB):
        cs = slice(c * FFN_SUB, (c + 1) * FFN_SUB)
        for t in range(2):
            ubuf[t, 0:SUBLANES, cs] = jnp.where(first, 0.0, carry[t, j, :, cs])
        for r in range(tm // rb):
            base = SUBLANES + r * rb
            halves = []
            for t, (w_ref, cw_ref, b_ref) in enumerate(parts):
                u = jnp.dot(h_ref[r * rb:(r + 1) * rb, :], w_ref[:, cs], preferred_element_type=F32)
                ubuf[t, base:base + rb, cs] = u
                if r == tm // rb - 1:
                    carry[t, j, :, cs] = u[rb - SUBLANES:rb, :]
                cw = cw_ref[:, cs]
                y = u * cw[FFN_CONV - 1:FFN_CONV, :] + b_ref[:, cs]
                for d in range(FFN_CONV - 1):
                    off = base - (FFN_CONV - 1) + d
                    y = y + ubuf[t, off:off + rb, cs] * cw[d:d + 1, :]
                halves.append(y)
            o_ref[r * rb:(r + 1) * rb, cs] = (_silu(halves[0]) * halves[1]).astype(o_ref.dtype)


def _ffn_up(h, wa, wb, cwa, cwb, ba, bb, seq):
    m, k = h.shape
    n = wa.shape[1]
    tm, tn = TILES["ffn_tm"], TILES["ffn_tn"]
    wspec = pl.BlockSpec((k, tn), lambda i, j: (0, j))
    cspec = pl.BlockSpec((FFN_CONV, tn), lambda i, j: (0, j))
    bspec = pl.BlockSpec((1, tn), lambda i, j: (0, j))
    return pl.pallas_call(
        functools.partial(_ffn_up_kernel, tiles_per_seq=seq // tm),
        grid=(m // tm, n // tn),
        in_specs=[pl.BlockSpec((tm, k), lambda i, j: (i, 0)), wspec, wspec, cspec, cspec, bspec, bspec],
        out_specs=pl.BlockSpec((tm, tn), lambda i, j: (i, j)),
        out_shape=jax.ShapeDtypeStruct((m, n), BF16),
        scratch_shapes=[
            pltpu.VMEM((2, tm + SUBLANES, tn), F32),
            pltpu.VMEM((2, n // tn, SUBLANES, tn), F32),
        ],
        compiler_params=_cparams(("arbitrary", "arbitrary")),
        name="ffn_up",
    )(h, wa, wb, cwa, cwb, ba, bb)


def _pad_cols(a, n):
    return jnp.pad(a, ((0, 0), (0, n - a.shape[1])))


def _layer(x2d, mem2d, batch, seq, mem_len, p):
    m = x2d.shape[0]
    h = GDN_HEADS
    hp = HEADS_PER_GROUP
    w_in = p["w_in"]
    o_z = QKV_W
    o_b = o_z + V_W
    o_p = o_b + 2 * h
    w_qkvz = w_in[:, :o_b].astype(BF16)
    w_gate = w_in[:, o_p:].astype(BF16)
    w_ba = _pad_cols(w_in[:, o_b:o_p], LANES).astype(BF16)
    alog_l = jnp.zeros((1, LANES), F32).at[0, h:2 * h].set(p["a_log"])
    dtb_l = jnp.zeros((1, LANES), F32).at[0, h:2 * h].set(p["dt_bias"])
    w_up = p["w_up"]
    wa_up = _pad_cols(w_up[:, :D_FF], D_FF_PAD).astype(BF16)
    wb_up = _pad_cols(w_up[:, D_FF:], D_FF_PAD).astype(BF16)
    cwa = _pad_cols(p["ffn_conv_w"][:, :D_FF], D_FF_PAD)
    cwb = _pad_cols(p["ffn_conv_w"][:, D_FF:], D_FF_PAD)
    fba = _pad_cols(p["ffn_conv_b"][None, :D_FF], D_FF_PAD)
    fbb = _pad_cols(p["ffn_conv_b"][None, D_FF:], D_FF_PAD)
    w_down = jnp.pad(p["w_down"], ((0, D_FF_PAD - D_FF), (0, 0))).astype(BF16)

    qkvz, ba, hn = _qkv_proj(x2d, p["mix_pre_norm"], w_qkvz, p["conv_qkv"], w_ba, seq)
    gate = _gate_proj(hn, w_gate)
    gates = _gates(ba, alog_l, dtb_l)
    if HEAD_GROUPS == 1:
        colg4 = gates[None]
    else:
        g4 = gates[:, :3 * h].reshape(m, 3, HEAD_GROUPS, hp).transpose(2, 0, 1, 3)
        colg4 = jnp.pad(g4.reshape(HEAD_GROUPS, m, 3 * hp), ((0, 0), (0, 0), (0, LANES - 3 * hp)))
    rowg4 = gates[:, h:2 * h].reshape(batch, seq, HEAD_GROUPS, hp).transpose(0, 2, 3, 1)
    cdecay = gates[CHUNK - 1::CHUNK, 3 * h:4 * h].reshape(batch, seq // CHUNK, h)
    cdecay = cdecay.transpose(0, 2, 1).reshape(batch * h, seq // CHUNK)
    og = _gdn(qkvz, colg4, rowg4, cdecay, p["gdn_norm"], batch, seq)
    pooled = _pool(gate, p["pool_w"].astype(BF16), p["pool_scale"], batch, seq)
    merged = _merge(og, pooled, p["w_branch_a"].astype(BF16), p["w_branch_b"].astype(BF16), gate)
    x1, q = _mm_norm_res(merged, p["w_mix_out"].astype(BF16), x2d, p["mix_post_norm"],
                         p["xa_pre_norm"], tm=TILES["res_tm"], next_proj=p["w_xq"].astype(BF16))
    kv = _norm_mm(mem2d, p["mem_norm"], p["w_xkv"].astype(BF16), tm=mem2d.shape[0], tn=1024,
                  out_dtype=BF16)
    xo = _xattn(q, kv, batch, seq, mem_len)
    x2, h3 = _mm_norm_res(xo, p["w_xo"].astype(BF16), x1, p["xa_post_norm"], p["ffn_pre_norm"],
                          tm=TILES["res_tm"])
    act = _ffn_up(h3, wa_up, wb_up, cwa, cwb, fba, fbb, seq)
    x3, _ = _mm_norm_res(act, w_down, x2, p["ffn_post_norm"], None, tm=TILES["down_tm"])
    return x3


def kernel(x, mem, mix_pre_norm, w_in, conv_qkv, a_log, dt_bias, gdn_norm, pool_w, pool_scale,
           w_branch_a, w_branch_b, w_mix_out, mix_post_norm, xa_pre_norm, mem_norm, w_xq, w_xkv,
           w_xo, xa_post_norm, ffn_pre_norm, w_up, ffn_conv_w, ffn_conv_b, w_down, ffn_post_norm):
    params = dict(mix_pre_norm=mix_pre_norm, w_in=w_in, conv_qkv=conv_qkv, a_log=a_log,
                  dt_bias=dt_bias, gdn_norm=gdn_norm, pool_w=pool_w, pool_scale=pool_scale,
                  w_branch_a=w_branch_a, w_branch_b=w_branch_b, w_mix_out=w_mix_out,
                  mix_post_norm=mix_post_norm, xa_pre_norm=xa_pre_norm, mem_norm=mem_norm,
                  w_xq=w_xq, w_xkv=w_xkv, w_xo=w_xo, xa_post_norm=xa_post_norm,
                  ffn_pre_norm=ffn_pre_norm, w_up=w_up, ffn_conv_w=ffn_conv_w,
                  ffn_conv_b=ffn_conv_b, w_down=w_down, ffn_post_norm=ffn_post_norm)
    batch, seq, d = x.shape
    mem_len = mem.shape[1]
    x2d = x.reshape(batch * seq, d)
    mem2d = mem.reshape(batch * mem_len, d)
    for l in range(w_in.shape[0]):
        x2d = _layer(x2d, mem2d, batch, seq, mem_len, {k: v[l] for k, v in params.items()})
    return x2d.reshape(batch, seq, d)
```

```python
import functools

import jax
import jax.numpy as jnp
from jax import lax
from jax.experimental import pallas as pl
from jax.experimental.pallas import tpu as pltpu

F32 = jnp.float32
BF16 = jnp.bfloat16

D_MODEL = 2048
CHUNK = 64
GDN_HEADS = 16
GDN_DK = 128
GDN_DV = 128
GDN_CONV = 4
QK_W = GDN_HEADS * GDN_DK
V_W = GDN_HEADS * GDN_DV
QKV_W = 2 * QK_W + V_W
POOL_WINDOWS = (2, 4, 8, 16)
POOL_WIDTH = D_MODEL // 2
POOL_GROUP = POOL_WIDTH // 4
XA_HEADS = 4
XA_HEAD_DIM = D_MODEL // XA_HEADS
D_FF = 5504
FFN_CONV = 3
EPS = 1e-6

LANES = 128
SUBLANES = 8
VMEM_LIMIT = 56 * 1024 * 1024

COL_Z = QKV_W
QKVZ_W = QKV_W + V_W
GCOL_P = 0
GCOL_GA = POOL_WIDTH
GCOL_GB = GCOL_GA + D_MODEL

HEADS_PER_GROUP = 16
HEAD_GROUPS = GDN_HEADS // HEADS_PER_GROUP
GROUP_W = HEADS_PER_GROUP * GDN_DK
GDN_ROWS = 128
CHUNKS_PER_STEP = GDN_ROWS // CHUNK
LOG2_CHUNK = 6
NEG_BIG = -1e30

D_FF_PAD = 5632
PROJ_SUB = 256
PROJ_ROWS = 256
CONV_PHASES = 4

TILES = dict(
    proj_tm=1024, proj_tn=1024,
    gate_rows=512,
    pool_rows=512,
    merge_tm=1024, merge_tn=512,
    res_tm=512,
    xattn_ts=1024,
    ffn_tm=1024, ffn_tn=512,
    down_tm=256,
)


def _cparams(sem):
    return pltpu.CompilerParams(dimension_semantics=sem, vmem_limit_bytes=VMEM_LIMIT)


def _rms(xf, w):
    ms = jnp.mean(xf * xf, axis=-1, keepdims=True)
    return xf * lax.rsqrt(ms + EPS) * w


def _silu(x):
    hx = 0.5 * x
    return hx + hx * jnp.tanh(hx)


def _bdot(a, b):
    return jnp.dot(a.astype(BF16), b.astype(BF16), preferred_element_type=F32)


def _causal_conv(u, prev, ubuf, cw):
    tm = u.shape[0]
    width = cw.shape[0]
    ubuf[0:SUBLANES, :] = prev
    ubuf[SUBLANES:SUBLANES + tm, :] = u
    y = u * cw[width - 1:width, :]
    for d in range(width - 1):
        off = SUBLANES - (width - 1) + d
        y = y + ubuf[off:off + tm, :] * cw[d:d + 1, :]
    return y


def _qkv_proj_kernel(x_ref, nw_ref, w_ref, cw_ref, wba_ref, o_ref, ba_ref, h_ref, ubuf, stage, carry, *,
                     tiles_per_seq, tiles_per_part):
    i = pl.program_id(0)
    j = pl.program_id(1)
    tm, tn = o_ref.shape

    @pl.when(j == 0)
    def _():
        h_ref[...] = _rms(x_ref[...], nw_ref[...]).astype(BF16)
        ba_ref[...] = jnp.dot(h_ref[...], wba_ref[...], preferred_element_type=F32)

    is_q = j < tiles_per_part
    is_qk = j < 2 * tiles_per_part
    na = jnp.where(is_q, float(GDN_DK), jnp.where(is_qk, 1.0, 0.0))
    nb = jnp.where(is_q, GDN_DK * EPS, jnp.where(is_qk, EPS, 1.0))
    first = i % tiles_per_seq == 0

    @pl.when(j < 3 * tiles_per_part)
    def _():
        nph = CONV_PHASES
        for c in range(tn // PROJ_SUB):
            cs = slice(c * PROJ_SUB, (c + 1) * PROJ_SUB)
            u = jnp.dot(h_ref[...], w_ref[:, cs], preferred_element_type=F32)
            prev = jnp.where(first, 0.0, carry[j, :, cs])
            carry[j, :, cs] = u[tm - SUBLANES:tm, :]
            for s in range(PROJ_SUB // LANES):
                slab = c * (PROJ_SUB // LANES) + s
                ls = slice(s * LANES, (s + 1) * LANES)
                ubuf[slab, 0:SUBLANES, :] = prev[:, ls]
                ubuf[slab, SUBLANES:SUBLANES + tm, :] = u[:, ls]
            for s in range(PROJ_SUB // LANES):
                slab = c * (PROJ_SUB // LANES) + s
                cw = cw_ref[:, slab * LANES:(slab + 1) * LANES]
                for ph in range(nph):
                    y = None
                    for d in range(GDN_CONV):
                        start = SUBLANES - (GDN_CONV - 1) + d + ph
                        term = ubuf[slab, pl.ds(start, tm // nph, stride=nph), :] * cw[d:d + 1, :]
                        y = term if y is None else y + term
                    y = _silu(y)
                    ss = jnp.sum(y * y, axis=-1, keepdims=True)
                    f = lax.rsqrt(ss * na + nb)
                    stage[slab, pl.ds(ph, tm // nph, stride=nph), :] = y * f
                o_ref[:, slab * LANES:(slab + 1) * LANES] = stage[slab].astype(o_ref.dtype)

    @pl.when(j >= 3 * tiles_per_part)
    def _():
        rb = PROJ_ROWS
        for c in range(tn // PROJ_SUB):
            cs = slice(c * PROJ_SUB, (c + 1) * PROJ_SUB)
            for r in range(tm // rb):
                rs = slice(r * rb, (r + 1) * rb)
                u = jnp.dot(h_ref[rs, :], w_ref[:, cs], preferred_element_type=F32)
                o_ref[rs, cs] = _silu(u).astype(o_ref.dtype)


def _qkv_proj(x, nw, w, conv_w, w_ba, seq):
    m, k = x.shape
    n = w.shape[1]
    tm, tn = TILES["proj_tm"], TILES["proj_tn"]
    return pl.pallas_call(
        functools.partial(_qkv_proj_kernel, tiles_per_seq=seq // tm, tiles_per_part=QK_W // tn),
        grid=(m // tm, n // tn),
        in_specs=[
            pl.BlockSpec((tm, k), lambda i, j: (i, 0)),
            pl.BlockSpec((1, k), lambda i, j: (0, 0)),
            pl.BlockSpec((k, tn), lambda i, j: (0, j)),
            pl.BlockSpec((GDN_CONV, tn), lambda i, j: (0, jnp.minimum(j, QKV_W // tn - 1))),
            pl.BlockSpec((k, LANES), lambda i, j: (0, 0)),
        ],
        out_specs=[pl.BlockSpec((tm, tn), lambda i, j: (i, j)),
                   pl.BlockSpec((tm, LANES), lambda i, j: (i, 0)),
                   pl.BlockSpec((tm, k), lambda i, j: (i, 0))],
        out_shape=[jax.ShapeDtypeStruct((m, n), BF16), jax.ShapeDtypeStruct((m, LANES), F32),
                   jax.ShapeDtypeStruct((m, k), BF16)],
        scratch_shapes=[
            pltpu.VMEM((tn // LANES, tm + SUBLANES, LANES), F32),
            pltpu.VMEM((tn // LANES, tm, LANES), F32),
            pltpu.VMEM((n // tn, SUBLANES, tn), F32),
        ],
        compiler_params=_cparams(("arbitrary", "arbitrary")),
        name="qkv_proj",
    )(x, nw.reshape(1, k), w, conv_w, w_ba)


def _gate_proj_kernel(h_ref, w_ref, o_ref, *, pool_tiles):
    j = pl.program_id(1)
    is_pool = j < pool_tiles
    tm, tn = o_ref.shape
    rb = PROJ_ROWS
    for c in range(tn // PROJ_SUB):
        cs = slice(c * PROJ_SUB, (c + 1) * PROJ_SUB)
        for r in range(tm // rb):
            rs = slice(r * rb, (r + 1) * rb)
            u = jnp.dot(h_ref[rs, :], w_ref[:, cs], preferred_element_type=F32)
            out = jnp.where(is_pool, u, jax.nn.sigmoid(u))
            o_ref[rs, cs] = out.astype(o_ref.dtype)


def _gate_proj(h, w):
    m, k = h.shape
    n = w.shape[1]
    tm, tn = TILES["proj_tm"], TILES["proj_tn"]
    return pl.pallas_call(
        functools.partial(_gate_proj_kernel, pool_tiles=POOL_WIDTH // tn),
        grid=(m // tm, n // tn),
        in_specs=[
            pl.BlockSpec((tm, k), lambda i, j: (i, 0)),
            pl.BlockSpec((k, tn), lambda i, j: (0, j)),
        ],
        out_specs=pl.BlockSpec((tm, tn), lambda i, j: (i, j)),
        out_shape=jax.ShapeDtypeStruct((m, n), BF16),
        compiler_params=_cparams(("parallel", "arbitrary")),
        name="gate_proj",
    )(h, w)


def _norm_mm_kernel(x_ref, nw_ref, w_ref, o_ref, h_ref):
    @pl.when(pl.program_id(1) == 0)
    def _():
        h_ref[...] = _rms(x_ref[...], nw_ref[...]).astype(BF16)

    o_ref[...] = jnp.dot(h_ref[...], w_ref[...], preferred_element_type=F32).astype(o_ref.dtype)


def _norm_mm(x, nw, w, *, tm, tn, out_dtype):
    m, k = x.shape
    n = w.shape[1]
    return pl.pallas_call(
        _norm_mm_kernel,
        grid=(m // tm, n // tn),
        in_specs=[
            pl.BlockSpec((tm, k), lambda i, j: (i, 0)),
            pl.BlockSpec((1, k), lambda i, j: (0, 0)),
            pl.BlockSpec((k, tn), lambda i, j: (0, j)),
        ],
        out_specs=pl.BlockSpec((tm, tn), lambda i, j: (i, j)),
        out_shape=jax.ShapeDtypeStruct((m, n), out_dtype),
        scratch_shapes=[pltpu.VMEM((tm, k), BF16)],
        compiler_params=_cparams(("parallel", "arbitrary")),
        name="norm_mm",
    )(x, nw.reshape(1, k), w)


def _mm_norm_res_kernel(a_ref, w_ref, res_ref, pw_ref, *rest, with_next, with_proj):
    rest = list(rest)
    nw_ref = rest.pop(0) if with_next else None
    w2_ref = rest.pop(0) if with_proj else None
    x_ref = rest.pop(0)
    y = jnp.dot(a_ref[...], w_ref[...], preferred_element_type=F32)
    xn = res_ref[...] + _rms(y, pw_ref[...])
    x_ref[...] = xn
    if with_next:
        h = _rms(xn, nw_ref[...]).astype(BF16)
        if with_proj:
            h = jnp.dot(h, w2_ref[...], preferred_element_type=F32).astype(BF16)
        rest[0][...] = h


def _mm_norm_res(a, w, res, post_w, next_w, *, tm, next_proj=None):
    m, k = a.shape
    n = w.shape[1]
    with_next = next_w is not None
    with_proj = next_proj is not None
    row = pl.BlockSpec((tm, n), lambda i: (i, 0))
    vec = pl.BlockSpec((1, n), lambda i: (0, 0))
    wspec = pl.BlockSpec((k, n), lambda i: (0, 0), pipeline_mode=pl.Buffered(1))
    in_specs = [pl.BlockSpec((tm, k), lambda i: (i, 0)), wspec, row, vec]
    args = [a, w, res, post_w.reshape(1, n)]
    out_specs = [row]
    out_shape = [jax.ShapeDtypeStruct((m, n), F32)]
    if with_next:
        in_specs.append(vec)
        args.append(next_w.reshape(1, n))
        if with_proj:
            in_specs.append(pl.BlockSpec(next_proj.shape, lambda i: (0, 0), pipeline_mode=pl.Buffered(1)))
            args.append(next_proj)
        out_specs.append(row)
        out_shape.append(jax.ShapeDtypeStruct((m, n), BF16))
    outs = pl.pallas_call(
        functools.partial(_mm_norm_res_kernel, with_next=with_next, with_proj=with_proj),
        grid=(m // tm,),
        in_specs=in_specs,
        out_specs=out_specs,
        out_shape=out_shape,
        compiler_params=_cparams(("parallel",)),
        name="mm_norm_res",
    )(*args)
    return outs if with_next else (outs[0], None)


def _gates_kernel(ba_ref, alog_ref, dtb_ref, o_ref):
    rows = ba_ref.shape[0]
    ba = ba_ref[...]
    beta = jax.nn.sigmoid(ba)
    xs = ba + dtb_ref[...]
    softplus = jnp.maximum(xs, 0.0) + jnp.log1p(jnp.exp(-jnp.abs(xs)))
    g = -jnp.exp(alog_ref[...]) * softplus
    ri = lax.broadcasted_iota(jnp.int32, (CHUNK, CHUNK), 0)
    ci = lax.broadcasted_iota(jnp.int32, (CHUNK, CHUNK), 1)
    tri = jnp.where(ri >= ci, 1.0, 0.0).astype(F32)
    ones = jnp.ones((CHUNK, CHUNK), F32)
    gcs, gls = [], []
    for c in range(rows // CHUNK):
        gch = g[c * CHUNK:(c + 1) * CHUNK]
        gcs.append(jnp.dot(tri, gch, preferred_element_type=F32, precision=lax.Precision.HIGHEST))
        gls.append(jnp.dot(ones, gch, preferred_element_type=F32, precision=lax.Precision.HIGHEST))
    gc = jnp.concatenate(gcs, axis=0)
    gl = jnp.concatenate(gls, axis=0)
    lane = lax.broadcasted_iota(jnp.int32, ba.shape, 1)
    h = GDN_HEADS
    out = jnp.where(lane < h, beta, gc)
    out = jnp.where(lane < 2 * h, out, pltpu.roll(gl - gc, h, axis=1))
    out = jnp.where(lane < 3 * h, out, pltpu.roll(jnp.exp(gl), 2 * h, axis=1))
    o_ref[...] = out


def _gates(ba, alog_l, dtb_l):
    m = ba.shape[0]
    rows = TILES["gate_rows"]
    vec = pl.BlockSpec((1, LANES), lambda i: (0, 0))
    return pl.pallas_call(
        _gates_kernel,
        grid=(m // rows,),
        in_specs=[pl.BlockSpec((rows, LANES), lambda i: (i, 0)), vec, vec],
        out_specs=pl.BlockSpec((rows, LANES), lambda i: (i, 0)),
        out_shape=jax.ShapeDtypeStruct((m, LANES), F32),
        compiler_params=_cparams(("parallel",)),
        name="gdn_gates",
    )(ba, alog_l, dtb_l)


def _gdn_kernel(cd_ref, q_ref, k_ref, v_ref, z_ref, colg_ref, rowg_ref, gn_ref, o_ref,
                lv_ref, lmask, s_ref, uv_buf, wq_buf, kd_buf, at_buf, ubuf, obuf, *, n_tiles):
    b = pl.program_id(0)
    hg = pl.program_id(1)
    si = pl.program_id(2)
    R = GDN_ROWS
    hp = HEADS_PER_GROUP
    heads = range(hp)
    sls = [slice(hl * GDN_DK, (hl + 1) * GDN_DK) for hl in heads]
    wslot = si % 2
    rslot = 1 - wslot

    @pl.when(si == 0)
    def _():
        s_ref[...] = jnp.zeros_like(s_ref)
        uv_buf[1] = jnp.zeros(uv_buf.shape[1:], uv_buf.dtype)
        wq_buf[1] = jnp.zeros(wq_buf.shape[1:], wq_buf.dtype)
        kd_buf[1] = jnp.zeros(kd_buf.shape[1:], kd_buf.dtype)
        at_buf[1] = jnp.zeros(at_buf.shape[1:], at_buf.dtype)
        ri = lax.broadcasted_iota(jnp.int32, (R, R), 0)
        ci = lax.broadcasted_iota(jnp.int32, (R, R), 1)
        x = ri ^ ci
        lv = jnp.full((R, R), -1, jnp.int32)
        for t in range(LOG2_CHUNK):
            lv = lv + jnp.where(x >= (1 << t), 1, 0)
        ok = ((ri >> LOG2_CHUNK) == (ci >> LOG2_CHUNK)) & (ri >= ci)
        lv = jnp.where(ok, lv, -2)
        lv_ref[...] = lv
        for level in range(1, LOG2_CHUNK):
            lmask[level - 1] = jnp.where(lv == level, 1.0, 0.0).astype(BF16)

    p1 = {}

    def p1_scores():
        p1["qk_kk"] = []
        for hl in heads:
            kb = k_ref[:, sls[hl]]
            p1["qk_kk"].append(lax.dot_general(jnp.concatenate([q_ref[:, sls[hl]], kb], axis=0), kb,
                                               (((1,), (1,)), ((), ())), preferred_element_type=F32))

    def p1_decay():
        lv = lv_ref[...]
        incl = lv >= -1
        colg = colg_ref[...]
        p1["beta"] = [colg[:, hl:hl + 1] for hl in heads]
        p1["exp_g"] = [jnp.exp(colg[:, hp + hl:hp + hl + 1]) for hl in heads]
        p1["low"], p1["xinv"] = [], []
        for hl in heads:
            gap = colg[:, hp + hl:hp + hl + 1] - rowg_ref[hl:hl + 1, :]
            decay = jnp.exp(jnp.where(incl, gap, NEG_BIG))
            at_buf[wslot, hl] = (decay * p1["qk_kk"][hl][0:R]).astype(BF16)
            lo = jnp.where(lv >= 0, p1["beta"][hl] * decay * p1["qk_kk"][hl][R:2 * R], 0.0)
            p1["low"].append(lo.astype(BF16))
            p1["xinv"].append((jnp.where(lv == -1, 1.0, 0.0) - jnp.where(lv == 0, lo, 0.0)).astype(BF16))

    def p1_level_a(level):
        def run():
            p1["ys"] = [jnp.dot(p1["low"][hl] * lmask[level - 1], p1["xinv"][hl],
                                preferred_element_type=F32).astype(BF16) for hl in heads]
        return run

    def p1_level_b():
        p1["xinv"] = [p1["xinv"][hl] - jnp.dot(p1["xinv"][hl], p1["ys"][hl],
                                               preferred_element_type=F32).astype(BF16) for hl in heads]

    def p1_solve():
        p1["sol"] = []
        for hl in heads:
            bk = (p1["beta"][hl] * p1["exp_g"][hl]) * k_ref[:, sls[hl]].astype(F32)
            rhs = jnp.concatenate([p1["beta"][hl] * v_ref[:, sls[hl]].astype(F32), bk], axis=1)
            p1["sol"].append(jnp.dot(p1["xinv"][hl], rhs.astype(BF16), preferred_element_type=F32))

    def p1_store():
        colg = colg_ref[...]
        for hl in heads:
            q_dec = q_ref[:, sls[hl]].astype(F32) * p1["exp_g"][hl]
            uv_buf[wslot, hl] = p1["sol"][hl][:, 0:GDN_DV]
            wq_buf[wslot, hl, 0:R, :] = p1["sol"][hl][:, GDN_DV:].astype(BF16)
            wq_buf[wslot, hl, R:2 * R, :] = q_dec.astype(BF16)
            kd_buf[wslot, hl] = (k_ref[:, sls[hl]].astype(F32)
                                 * jnp.exp(colg[:, 2 * hp + hl:2 * hp + hl + 1])).astype(BF16)

    p1_stages = [p1_scores, p1_decay]
    for level in range(1, LOG2_CHUNK):
        p1_stages += [p1_level_a(level), p1_level_b]
    p1_stages += [p1_solve, p1_store]

    p2 = {}
    tile = jnp.maximum(si - 1, 0)

    def p2_read(c):
        def run():
            p2["states"] = [s_ref[hl] for hl in heads]
            p2["wqs"] = []
            for hl in heads:
                lhs = jnp.concatenate([wq_buf[rslot, hl, c * CHUNK:(c + 1) * CHUNK, :],
                                       wq_buf[rslot, hl, R + c * CHUNK:R + (c + 1) * CHUNK, :]], axis=0)
                p2["wqs"].append(jnp.dot(lhs, p2["states"][hl].astype(BF16),
                                         preferred_element_type=F32))
        return run

    def p2_update(c):
        def run():
            rs = slice(c * CHUNK, (c + 1) * CHUNK)
            for hl in heads:
                u = (uv_buf[rslot, hl, rs, :] - p2["wqs"][hl][0:CHUNK]).astype(BF16)
                ubuf[hl, rs, :] = u
                obuf[hl, rs, :] = p2["wqs"][hl][CHUNK:]
                cd = cd_ref[b * GDN_HEADS + hg * HEADS_PER_GROUP + hl, tile * CHUNKS_PER_STEP + c]
                upd = lax.dot_general(kd_buf[rslot, hl, rs, :], u, (((0,), (0,)), ((), ())),
                                      preferred_element_type=F32)
                s_ref[hl] = cd * p2["states"][hl] + upd
        return run

    def p2_output():
        for hl in heads:
            o = obuf[hl] + jnp.dot(at_buf[rslot, hl], ubuf[hl], preferred_element_type=F32)
            on = _rms(o, gn_ref[...]) * z_ref[:, sls[hl]].astype(F32)
            o_ref[:, sls[hl]] = on.astype(o_ref.dtype)

    p2_stages = []
    for c in range(CHUNKS_PER_STEP):
        p2_stages += [p2_read(c), p2_update(c)]
    p2_stages.append(p2_output)

    n1, n2 = len(p1_stages), len(p2_stages)
    i2 = 0
    for i1, stage in enumerate(p1_stages):
        stage()
        while i2 < n2 and (i2 + 1) * n1 <= (i1 + 1) * n2:
            p2_stages[i2]()
            i2 += 1
    while i2 < n2:
        p2_stages[i2]()
        i2 += 1


def _gdn(qkvz, colg4, rowg4, cdecay, gdn_norm, batch, seq):
    m = qkvz.shape[0]
    ns = seq // GDN_ROWS
    gw = GROUP_W
    R = GDN_ROWS

    def cur(s):
        return jnp.minimum(s, ns - 1)

    def prv(s):
        return jnp.maximum(s - 1, 0)

    def xspec(col0):
        return pl.BlockSpec((R, gw), lambda b, h, s: (b * ns + cur(s), col0 // gw + h))

    hp = HEADS_PER_GROUP
    return pl.pallas_call(
        functools.partial(_gdn_kernel, n_tiles=ns),
        grid=(batch, HEAD_GROUPS, ns + 1),
        in_specs=[
            pl.BlockSpec(memory_space=pltpu.SMEM),
            xspec(0), xspec(QK_W), xspec(2 * QK_W),
            pl.BlockSpec((R, gw), lambda b, h, s: (b * ns + prv(s), COL_Z // gw + h)),
            pl.BlockSpec((None, R, LANES), lambda b, h, s: (h, b * ns + cur(s), 0)),
            pl.BlockSpec((None, None, hp, R), lambda b, h, s: (b, h, 0, cur(s))),
            pl.BlockSpec((1, GDN_DV), lambda b, h, s: (0, 0)),
        ],
        out_specs=pl.BlockSpec((R, gw), lambda b, h, s: (b * ns + prv(s), h)),
        out_shape=jax.ShapeDtypeStruct((m, V_W), BF16),
        scratch_shapes=[
            pltpu.VMEM((R, R), jnp.int32),
            pltpu.VMEM((LOG2_CHUNK - 1, R, R), BF16),
            pltpu.VMEM((hp, GDN_DK, GDN_DV), F32),
            pltpu.VMEM((2, hp, R, GDN_DV), F32),
            pltpu.VMEM((2, hp, 2 * R, GDN_DK), BF16),
            pltpu.VMEM((2, hp, R, GDN_DK), BF16),
            pltpu.VMEM((2, hp, R, R), BF16),
            pltpu.VMEM((hp, R, GDN_DV), BF16),
            pltpu.VMEM((hp, R, GDN_DV), F32),
        ],
        compiler_params=_cparams(("parallel", "parallel", "arbitrary")),
        name="gdn",
    )(cdecay, qkvz, qkvz, qkvz, qkvz, colg4, rowg4, gdn_norm.reshape(1, GDN_DV))


POOL_HALO = 16


def _pool_kernel(p_ref, w_ref, sc_ref, o_ref, buf):
    si = pl.program_id(1)
    R = p_ref.shape[0]

    @pl.when(si == 0)
    def _():
        buf[0:POOL_HALO, :] = jnp.zeros((POOL_HALO, POOL_WIDTH), F32)

    buf[POOL_HALO:POOL_HALO + R, :] = p_ref[...].astype(F32)
    t = si * R + lax.broadcasted_iota(jnp.int32, (R, 1), 0)
    for gi, win in enumerate(POOL_WINDOWS):
        sl = slice(gi * POOL_GROUP, (gi + 1) * POOL_GROUP)
        x = buf[POOL_HALO:POOL_HALO + R, sl]
        acc = x
        for d in range(1, win):
            acc = acc + buf[POOL_HALO - d:POOL_HALO - d + R, sl]
        cnt = jnp.minimum(t + 1, win).astype(F32)
        y = acc / cnt - x
        o = _bdot(y, w_ref[gi]) * sc_ref[:, sl]
        o_ref[:, sl] = o.astype(o_ref.dtype)
    buf[0:POOL_HALO, :] = buf[R:R + POOL_HALO, :]


def _pool(gate, pool_w, pool_scale, batch, seq):
    m = gate.shape[0]
    rows = TILES["pool_rows"]
    ns = seq // rows
    return pl.pallas_call(
        _pool_kernel,
        grid=(batch, ns),
        in_specs=[
            pl.BlockSpec((rows, POOL_WIDTH), lambda b, s: (b * ns + s, GCOL_P // POOL_WIDTH)),
            pl.BlockSpec((len(POOL_WINDOWS), POOL_GROUP, POOL_GROUP), lambda b, s: (0, 0, 0)),
            pl.BlockSpec((1, POOL_WIDTH), lambda b, s: (0, 0)),
        ],
        out_specs=pl.BlockSpec((rows, POOL_WIDTH), lambda b, s: (b * ns + s, 0)),
        out_shape=jax.ShapeDtypeStruct((m, POOL_WIDTH), BF16),
        scratch_shapes=[pltpu.VMEM((rows + POOL_HALO, POOL_WIDTH), F32)],
        compiler_params=_cparams(("parallel", "arbitrary")),
        name="pool",
    )(gate, pool_w, pool_scale.reshape(1, POOL_WIDTH))


def _merge_kernel(a_ref, p_ref, wa_ref, wb_ref, ga_ref, gb_ref, o_ref):
    tm, tn = o_ref.shape
    rb = PROJ_ROWS
    for c in range(tn // PROJ_SUB):
        cs = slice(c * PROJ_SUB, (c + 1) * PROJ_SUB)
        for r in range(tm // rb):
            rs = slice(r * rb, (r + 1) * rb)
            ya = jnp.dot(a_ref[rs, :], wa_ref[:, cs], preferred_element_type=F32)
            yb = jnp.dot(p_ref[rs, :], wb_ref[:, cs], preferred_element_type=F32)
            merged = ga_ref[rs, cs].astype(F32) * ya + gb_ref[rs, cs].astype(F32) * yb
            o_ref[rs, cs] = merged.astype(o_ref.dtype)


def _merge(og, pooled, wa, wb, gate):
    m = og.shape[0]
    n = wa.shape[1]
    tm, tn = TILES["merge_tm"], TILES["merge_tn"]
    return pl.pallas_call(
        _merge_kernel,
        grid=(m // tm, n // tn),
        in_specs=[
            pl.BlockSpec((tm, V_W), lambda i, j: (i, 0)),
            pl.BlockSpec((tm, POOL_WIDTH), lambda i, j: (i, 0)),
            pl.BlockSpec((V_W, tn), lambda i, j: (0, j)),
            pl.BlockSpec((POOL_WIDTH, tn), lambda i, j: (0, j)),
            pl.BlockSpec((tm, tn), lambda i, j: (i, GCOL_GA // tn + j)),
            pl.BlockSpec((tm, tn), lambda i, j: (i, GCOL_GB // tn + j)),
        ],
        out_specs=pl.BlockSpec((tm, tn), lambda i, j: (i, j)),
        out_shape=jax.ShapeDtypeStruct((m, n), BF16),
        compiler_params=_cparams(("parallel", "arbitrary")),
        name="merge",
    )(og, pooled, wa, wb, gate, gate)


def _xattn_kernel(q_ref, k_ref, v_ref, o_ref):
    s = lax.dot_general(q_ref[...], k_ref[...], (((1,), (1,)), ((), ())),
                        preferred_element_type=F32) * (XA_HEAD_DIM ** -0.5)
    s = s - jnp.max(s, axis=-1, keepdims=True)
    e = jnp.exp(s)
    p = e / jnp.sum(e, axis=-1, keepdims=True)
    o_ref[...] = jnp.dot(p.astype(BF16), v_ref[...], preferred_element_type=F32).astype(o_ref.dtype)


def _xattn(q, kv, batch, seq, mem_len):
    m = q.shape[0]
    ts = TILES["xattn_ts"]
    ns = seq // ts
    return pl.pallas_call(
        _xattn_kernel,
        grid=(batch, ns, XA_HEADS),
        in_specs=[
            pl.BlockSpec((ts, XA_HEAD_DIM), lambda b, s, h: (b * ns + s, h)),
            pl.BlockSpec((mem_len, XA_HEAD_DIM), lambda b, s, h: (b, h)),
            pl.BlockSpec((mem_len, XA_HEAD_DIM), lambda b, s, h: (b, XA_HEADS + h)),
        ],
        out_specs=pl.BlockSpec((ts, XA_HEAD_DIM), lambda b, s, h: (b * ns + s, h)),
        out_shape=jax.ShapeDtypeStruct((m, D_MODEL), BF16),
        compiler_params=_cparams(("parallel", "parallel", "arbitrary")),
        name="xattn",
    )(q, kv, kv)


def _ffn_up_kernel(h_ref, wa_ref, wb_ref, cwa_ref, cwb_ref, ba_ref, bb_ref, o_ref, ubuf, carry,
                   *, tiles_per_seq):
    i = pl.program_id(0)
    j = pl.program_id(1)
    tm = h_ref.shape[0]
    first = i % tiles_per_seq == 0
    halves = []
    for t, (w_ref, cw_ref, b_ref) in enumerate(((wa_ref, cwa_ref, ba_ref), (wb_ref, cwb_ref, bb_ref))):
        u = jnp.dot(h_ref[...], w_ref[...], preferred_element_type=F32)
        prev = jnp.where(first, 0.0, carry[t, j])
        carry[t, j] = u[tm - SUBLANES:tm, :]
        halves.append(_causal_conv(u, prev, ubuf.at[t], cw_ref[...]) + b_ref[...])
    o_ref[...] = (_silu(halves[0]) * halves[1]).astype(o_ref.dtype)


def _ffn_up(h, wa, wb, cwa, cwb, ba, bb, seq):
    m, k = h.shape
    n = wa.shape[1]
    tm, tn = TILES["ffn_tm"], TILES["ffn_tn"]
    wspec = pl.BlockSpec((k, tn), lambda i, j: (0, j))
    cspec = pl.BlockSpec((FFN_CONV, tn), lambda i, j: (0, j))
    bspec = pl.BlockSpec((1, tn), lambda i, j: (0, j))
    return pl.pallas_call(
        functools.partial(_ffn_up_kernel, tiles_per_seq=seq // tm),
        grid=(m // tm, n // tn),
        in_specs=[pl.BlockSpec((tm, k), lambda i, j: (i, 0)), wspec, wspec, cspec, cspec, bspec, bspec],
        out_specs=pl.BlockSpec((tm, tn), lambda i, j: (i, j)),
        out_shape=jax.ShapeDtypeStruct((m, n), BF16),
        scratch_shapes=[
            pltpu.VMEM((2, tm + SUBLANES, tn), F32),
            pltpu.VMEM((2, n // tn, SUBLANES, tn), F32),
        ],
        compiler_params=_cparams(("arbitrary", "arbitrary")),
        name="ffn_up",
    )(h, wa, wb, cwa, cwb, ba, bb)


def _pad_cols(a, n):
    return jnp.pad(a, ((0, 0), (0, n - a.shape[1])))


def _layer(x2d, mem2d, batch, seq, mem_len, p):
    m = x2d.shape[0]
    h = GDN_HEADS
    hp = HEADS_PER_GROUP
    w_in = p["w_in"]
    o_z = QKV_W
    o_b = o_z + V_W
    o_p = o_b + 2 * h
    w_qkvz = w_in[:, :o_b].astype(BF16)
    w_gate = w_in[:, o_p:].astype(BF16)
    w_ba = _pad_cols(w_in[:, o_b:o_p], LANES).astype(BF16)
    alog_l = jnp.zeros((1, LANES), F32).at[0, h:2 * h].set(p["a_log"])
    dtb_l = jnp.zeros((1, LANES), F32).at[0, h:2 * h].set(p["dt_bias"])
    w_up = p["w_up"]
    wa_up = _pad_cols(w_up[:, :D_FF], D_FF_PAD).astype(BF16)
    wb_up = _pad_cols(w_up[:, D_FF:], D_FF_PAD).astype(BF16)
    cwa = _pad_cols(p["ffn_conv_w"][:, :D_FF], D_FF_PAD)
    cwb = _pad_cols(p["ffn_conv_w"][:, D_FF:], D_FF_PAD)
    fba = _pad_cols(p["ffn_conv_b"][None, :D_FF], D_FF_PAD)
    fbb = _pad_cols(p["ffn_conv_b"][None, D_FF:], D_FF_PAD)
    w_down = jnp.pad(p["w_down"], ((0, D_FF_PAD - D_FF), (0, 0))).astype(BF16)

    qkvz, ba, hn = _qkv_proj(x2d, p["mix_pre_norm"], w_qkvz, p["conv_qkv"], w_ba, seq)
    gate = _gate_proj(hn, w_gate)
    gates = _gates(ba, alog_l, dtb_l)
    if HEAD_GROUPS == 1:
        colg4 = gates[None]
    else:
        g4 = gates[:, :3 * h].reshape(m, 3, HEAD_GROUPS, hp).transpose(2, 0, 1, 3)
        colg4 = jnp.pad(g4.reshape(HEAD_GROUPS, m, 3 * hp), ((0, 0), (0, 0), (0, LANES - 3 * hp)))
    rowg4 = gates[:, h:2 * h].reshape(batch, seq, HEAD_GROUPS, hp).transpose(0, 2, 3, 1)
    cdecay = gates[CHUNK - 1::CHUNK, 3 * h:4 * h].reshape(batch, seq // CHUNK, h)
    cdecay = cdecay.transpose(0, 2, 1).reshape(batch * h, seq // CHUNK)
    og = _gdn(qkvz, colg4, rowg4, cdecay, p["gdn_norm"], batch, seq)
    pooled = _pool(gate, p["pool_w"].astype(BF16), p["pool_scale"], batch, seq)
    merged = _merge(og, pooled, p["w_branch_a"].astype(BF16), p["w_branch_b"].astype(BF16), gate)
    x1, q = _mm_norm_res(merged, p["w_mix_out"].astype(BF16), x2d, p["mix_post_norm"],
                         p["xa_pre_norm"], tm=TILES["res_tm"], next_proj=p["w_xq"].astype(BF16))
    kv = _norm_mm(mem2d, p["mem_norm"], p["w_xkv"].astype(BF16), tm=mem2d.shape[0], tn=1024,
                  out_dtype=BF16)
    xo = _xattn(q, kv, batch, seq, mem_len)
    x2, h3 = _mm_norm_res(xo, p["w_xo"].astype(BF16), x1, p["xa_post_norm"], p["ffn_pre_norm"],
                          tm=TILES["res_tm"])
    act = _ffn_up(h3, wa_up, wb_up, cwa, cwb, fba, fbb, seq)
    x3, _ = _mm_norm_res(act, w_down, x2, p["ffn_post_norm"], None, tm=TILES["down_tm"])
    return x3


def kernel(x, mem, mix_pre_norm, w_in, conv_qkv, a_log, dt_bias, gdn_norm, pool_w, pool_scale,
           w_branch_a, w_branch_b, w_mix_out, mix_post_norm, xa_pre_norm, mem_norm, w_xq, w_xkv,
           w_xo, xa_post_norm, ffn_pre_norm, w_up, ffn_conv_w, ffn_conv_b, w_down, ffn_post_norm):
    params = dict(mix_pre_norm=mix_pre_norm, w_in=w_in, conv_qkv=conv_qkv, a_log=a_log,
                  dt_bias=dt_bias, gdn_norm=gdn_norm, pool_w=pool_w, pool_scale=pool_scale,
                  w_branch_a=w_branch_a, w_branch_b=w_branch_b, w_mix_out=w_mix_out,
                  mix_post_norm=mix_post_norm, xa_pre_norm=xa_pre_norm, mem_norm=mem_norm,
                  w_xq=w_xq, w_xkv=w_xkv, w_xo=w_xo, xa_post_norm=xa_post_norm,
                  ffn_pre_norm=ffn_pre_norm, w_up=w_up, ffn_conv_w=ffn_conv_w,
                  ffn_conv_b=ffn_conv_b, w_down=w_down, ffn_post_norm=ffn_post_norm)
    batch, seq, d = x.shape
    mem_len = mem.shape[1]
    x2d = x.reshape(batch * seq, d)
    mem2d = mem.reshape(batch * mem_len, d)
    for l in range(w_in.shape[0]):
        x2d = _layer(x2d, mem2d, batch, seq, mem_len, {k: v[l] for k, v in params.items()})
    return x2d.reshape(batch, seq, d)
```

```python
import functools

import jax
import jax.numpy as jnp
from jax import lax
from jax.experimental import pallas as pl
from jax.experimental.pallas import tpu as pltpu

F32 = jnp.float32
BF16 = jnp.bfloat16

D_MODEL = 2048
CHUNK = 64
GDN_HEADS = 16
GDN_DK = 128
GDN_DV = 128
GDN_CONV = 4
QK_W = GDN_HEADS * GDN_DK
V_W = GDN_HEADS * GDN_DV
QKV_W = 2 * QK_W + V_W
POOL_WINDOWS = (2, 4, 8, 16)
POOL_WIDTH = D_MODEL // 2
POOL_GROUP = POOL_WIDTH // 4
XA_HEADS = 4
XA_HEAD_DIM = D_MODEL // XA_HEADS
D_FF = 5504
FFN_CONV = 3
EPS = 1e-6

LANES = 128
SUBLANES = 8
VMEM_LIMIT = 56 * 1024 * 1024

COL_Z = QKV_W
QKVZ_W = QKV_W + V_W
GCOL_P = 0
GCOL_GA = POOL_WIDTH
GCOL_GB = GCOL_GA + D_MODEL

HEADS_PER_GROUP = 16
HEAD_GROUPS = GDN_HEADS // HEADS_PER_GROUP
GROUP_W = HEADS_PER_GROUP * GDN_DK
GDN_ROWS = 128
CHUNKS_PER_STEP = GDN_ROWS // CHUNK
LOG2_CHUNK = 6
NEG_BIG = -1e30

D_FF_PAD = 5632
PROJ_SUB = 256
PROJ_ROWS = 256
CONV_PHASES = 4

TILES = dict(
    proj_tm=1024, proj_tn=1024,
    gate_rows=512,
    pool_rows=512,
    merge_tm=1024, merge_tn=1024,
    res_tm=512,
    xattn_ts=4096,
    ffn_tm=1024, ffn_tn=512,
    down_tm=512,
)


def _cparams(sem):
    return pltpu.CompilerParams(dimension_semantics=sem, vmem_limit_bytes=VMEM_LIMIT)


def _rms(xf, w):
    ms = jnp.mean(xf * xf, axis=-1, keepdims=True)
    return xf * lax.rsqrt(ms + EPS) * w


def _silu(x):
    hx = 0.5 * x
    return hx + hx * jnp.tanh(hx)


def _bdot(a, b):
    return jnp.dot(a.astype(BF16), b.astype(BF16), preferred_element_type=F32)


def _causal_conv(u, prev, ubuf, cw):
    tm = u.shape[0]
    width = cw.shape[0]
    ubuf[0:SUBLANES, :] = prev
    ubuf[SUBLANES:SUBLANES + tm, :] = u
    y = u * cw[width - 1:width, :]
    for d in range(width - 1):
        off = SUBLANES - (width - 1) + d
        y = y + ubuf[off:off + tm, :] * cw[d:d + 1, :]
    return y


def _qkv_proj_kernel(x_ref, nw_ref, w_ref, cw_ref, wba_ref, o_ref, ba_ref, h_ref, ubuf, stage, carry, *,
                     tiles_per_seq, tiles_per_part):
    i = pl.program_id(0)
    j = pl.program_id(1)
    tm, tn = o_ref.shape

    @pl.when(j == 0)
    def _():
        h_ref[...] = _rms(x_ref[...], nw_ref[...]).astype(BF16)
        ba_ref[...] = jnp.dot(h_ref[...], wba_ref[...], preferred_element_type=F32)

    is_q = j < tiles_per_part
    is_qk = j < 2 * tiles_per_part
    na = jnp.where(is_q, float(GDN_DK), jnp.where(is_qk, 1.0, 0.0))
    nb = jnp.where(is_q, GDN_DK * EPS, jnp.where(is_qk, EPS, 1.0))
    first = i % tiles_per_seq == 0

    @pl.when(j < 3 * tiles_per_part)
    def _():
        nph = CONV_PHASES
        for c in range(tn // PROJ_SUB):
            cs = slice(c * PROJ_SUB, (c + 1) * PROJ_SUB)
            u = jnp.dot(h_ref[...], w_ref[:, cs], preferred_element_type=F32)
            prev = jnp.where(first, 0.0, carry[j, :, cs])
            carry[j, :, cs] = u[tm - SUBLANES:tm, :]
            for s in range(PROJ_SUB // LANES):
                slab = c * (PROJ_SUB // LANES) + s
                ls = slice(s * LANES, (s + 1) * LANES)
                ubuf[slab, 0:SUBLANES, :] = prev[:, ls]
                ubuf[slab, SUBLANES:SUBLANES + tm, :] = u[:, ls]
            for s in range(PROJ_SUB // LANES):
                slab = c * (PROJ_SUB // LANES) + s
                cw = cw_ref[:, slab * LANES:(slab + 1) * LANES]
                for ph in range(nph):
                    y = None
                    for d in range(GDN_CONV):
                        start = SUBLANES - (GDN_CONV - 1) + d + ph
                        term = ubuf[slab, pl.ds(start, tm // nph, stride=nph), :] * cw[d:d + 1, :]
                        y = term if y is None else y + term
                    y = _silu(y)
                    ss = jnp.sum(y * y, axis=-1, keepdims=True)
                    f = lax.rsqrt(ss * na + nb)
                    stage[slab, pl.ds(ph, tm // nph, stride=nph), :] = y * f
                o_ref[:, slab * LANES:(slab + 1) * LANES] = stage[slab].astype(o_ref.dtype)

    @pl.when(j >= 3 * tiles_per_part)
    def _():
        rb = PROJ_ROWS
        for c in range(tn // PROJ_SUB):
            cs = slice(c * PROJ_SUB, (c + 1) * PROJ_SUB)
            for r in range(tm // rb):
                rs = slice(r * rb, (r + 1) * rb)
                u = jnp.dot(h_ref[rs, :], w_ref[:, cs], preferred_element_type=F32)
                o_ref[rs, cs] = _silu(u).astype(o_ref.dtype)


def _qkv_proj(x, nw, w, conv_w, w_ba, seq):
    m, k = x.shape
    n = w.shape[1]
    tm, tn = TILES["proj_tm"], TILES["proj_tn"]
    return pl.pallas_call(
        functools.partial(_qkv_proj_kernel, tiles_per_seq=seq // tm, tiles_per_part=QK_W // tn),
        grid=(m // tm, n // tn),
        in_specs=[
            pl.BlockSpec((tm, k), lambda i, j: (i, 0)),
            pl.BlockSpec((1, k), lambda i, j: (0, 0)),
            pl.BlockSpec((k, tn), lambda i, j: (0, j)),
            pl.BlockSpec((GDN_CONV, tn), lambda i, j: (0, jnp.minimum(j, QKV_W // tn - 1))),
            pl.BlockSpec((k, LANES), lambda i, j: (0, 0)),
        ],
        out_specs=[pl.BlockSpec((tm, tn), lambda i, j: (i, j)),
                   pl.BlockSpec((tm, LANES), lambda i, j: (i, 0)),
                   pl.BlockSpec((tm, k), lambda i, j: (i, 0))],
        out_shape=[jax.ShapeDtypeStruct((m, n), BF16), jax.ShapeDtypeStruct((m, LANES), F32),
                   jax.ShapeDtypeStruct((m, k), BF16)],
        scratch_shapes=[
            pltpu.VMEM((tn // LANES, tm + SUBLANES, LANES), F32),
            pltpu.VMEM((tn // LANES, tm, LANES), F32),
            pltpu.VMEM((n // tn, SUBLANES, tn), F32),
        ],
        compiler_params=_cparams(("arbitrary", "arbitrary")),
        name="qkv_proj",
    )(x, nw.reshape(1, k), w, conv_w, w_ba)


def _gate_proj_kernel(h_ref, w_ref, o_ref, *, pool_tiles):
    j = pl.program_id(1)
    is_pool = j < pool_tiles
    tm, tn = o_ref.shape
    rb = PROJ_ROWS
    for c in range(tn // PROJ_SUB):
        cs = slice(c * PROJ_SUB, (c + 1) * PROJ_SUB)
        for r in range(tm // rb):
            rs = slice(r * rb, (r + 1) * rb)
            u = jnp.dot(h_ref[rs, :], w_ref[:, cs], preferred_element_type=F32)
            out = jnp.where(is_pool, u, jax.nn.sigmoid(u))
            o_ref[rs, cs] = out.astype(o_ref.dtype)


def _gate_proj(h, w):
    m, k = h.shape
    n = w.shape[1]
    tm, tn = TILES["proj_tm"], TILES["proj_tn"]
    return pl.pallas_call(
        functools.partial(_gate_proj_kernel, pool_tiles=POOL_WIDTH // tn),
        grid=(m // tm, n // tn),
        in_specs=[
            pl.BlockSpec((tm, k), lambda i, j: (i, 0)),
            pl.BlockSpec((k, tn), lambda i, j: (0, j)),
        ],
        out_specs=pl.BlockSpec((tm, tn), lambda i, j: (i, j)),
        out_shape=jax.ShapeDtypeStruct((m, n), BF16),
        compiler_params=_cparams(("parallel", "arbitrary")),
        name="gate_proj",
    )(h, w)


def _norm_mm_kernel(x_ref, nw_ref, w_ref, o_ref, h_ref):
    @pl.when(pl.program_id(1) == 0)
    def _():
        h_ref[...] = _rms(x_ref[...], nw_ref[...]).astype(BF16)

    o_ref[...] = jnp.dot(h_ref[...], w_ref[...], preferred_element_type=F32).astype(o_ref.dtype)


def _norm_mm(x, nw, w, *, tm, tn, out_dtype):
    m, k = x.shape
    n = w.shape[1]
    return pl.pallas_call(
        _norm_mm_kernel,
        grid=(m // tm, n // tn),
        in_specs=[
            pl.BlockSpec((tm, k), lambda i, j: (i, 0)),
            pl.BlockSpec((1, k), lambda i, j: (0, 0)),
            pl.BlockSpec((k, tn), lambda i, j: (0, j)),
        ],
        out_specs=pl.BlockSpec((tm, tn), lambda i, j: (i, j)),
        out_shape=jax.ShapeDtypeStruct((m, n), out_dtype),
        scratch_shapes=[pltpu.VMEM((tm, k), BF16)],
        compiler_params=_cparams(("parallel", "arbitrary")),
        name="norm_mm",
    )(x, nw.reshape(1, k), w)


def _mm_norm_res_kernel(a_ref, w_ref, res_ref, pw_ref, *rest, with_next, with_proj):
    rest = list(rest)
    nw_ref = rest.pop(0) if with_next else None
    w2_ref = rest.pop(0) if with_proj else None
    x_ref = rest.pop(0)
    y = jnp.dot(a_ref[...], w_ref[...], preferred_element_type=F32)
    xn = res_ref[...] + _rms(y, pw_ref[...])
    x_ref[...] = xn
    if with_next:
        h = _rms(xn, nw_ref[...]).astype(BF16)
        if with_proj:
            h = jnp.dot(h, w2_ref[...], preferred_element_type=F32).astype(BF16)
        rest[0][...] = h


def _mm_norm_res(a, w, res, post_w, next_w, *, tm, next_proj=None):
    m, k = a.shape
    n = w.shape[1]
    with_next = next_w is not None
    with_proj = next_proj is not None
    row = pl.BlockSpec((tm, n), lambda i: (i, 0))
    vec = pl.BlockSpec((1, n), lambda i: (0, 0))
    wspec = pl.BlockSpec((k, n), lambda i: (0, 0), pipeline_mode=pl.Buffered(1))
    in_specs = [pl.BlockSpec((tm, k), lambda i: (i, 0)), wspec, row, vec]
    args = [a, w, res, post_w.reshape(1, n)]
    out_specs = [row]
    out_shape = [jax.ShapeDtypeStruct((m, n), F32)]
    if with_next:
        in_specs.append(vec)
        args.append(next_w.reshape(1, n))
        if with_proj:
            in_specs.append(pl.BlockSpec(next_proj.shape, lambda i: (0, 0), pipeline_mode=pl.Buffered(1)))
            args.append(next_proj)
        out_specs.append(row)
        out_shape.append(jax.ShapeDtypeStruct((m, n), BF16))
    outs = pl.pallas_call(
        functools.partial(_mm_norm_res_kernel, with_next=with_next, with_proj=with_proj),
        grid=(m // tm,),
        in_specs=in_specs,
        out_specs=out_specs,
        out_shape=out_shape,
        compiler_params=_cparams(("parallel",)),
        name="mm_norm_res",
    )(*args)
    return outs if with_next else (outs[0], None)


def _gates_kernel(ba_ref, alog_ref, dtb_ref, o_ref):
    rows = ba_ref.shape[0]
    ba = ba_ref[...]
    beta = jax.nn.sigmoid(ba)
    xs = ba + dtb_ref[...]
    softplus = jnp.maximum(xs, 0.0) + jnp.log1p(jnp.exp(-jnp.abs(xs)))
    g = -jnp.exp(alog_ref[...]) * softplus
    ri = lax.broadcasted_iota(jnp.int32, (CHUNK, CHUNK), 0)
    ci = lax.broadcasted_iota(jnp.int32, (CHUNK, CHUNK), 1)
    tri = jnp.where(ri >= ci, 1.0, 0.0).astype(F32)
    ones = jnp.ones((CHUNK, CHUNK), F32)
    gcs, gls = [], []
    for c in range(rows // CHUNK):
        gch = g[c * CHUNK:(c + 1) * CHUNK]
        gcs.append(jnp.dot(tri, gch, preferred_element_type=F32, precision=lax.Precision.HIGHEST))
        gls.append(jnp.dot(ones, gch, preferred_element_type=F32, precision=lax.Precision.HIGHEST))
    gc = jnp.concatenate(gcs, axis=0)
    gl = jnp.concatenate(gls, axis=0)
    lane = lax.broadcasted_iota(jnp.int32, ba.shape, 1)
    h = GDN_HEADS
    out = jnp.where(lane < h, beta, gc)
    out = jnp.where(lane < 2 * h, out, pltpu.roll(gl - gc, h, axis=1))
    out = jnp.where(lane < 3 * h, out, pltpu.roll(jnp.exp(gl), 2 * h, axis=1))
    o_ref[...] = out


def _gates(ba, alog_l, dtb_l):
    m = ba.shape[0]
    rows = TILES["gate_rows"]
    vec = pl.BlockSpec((1, LANES), lambda i: (0, 0))
    return pl.pallas_call(
        _gates_kernel,
        grid=(m // rows,),
        in_specs=[pl.BlockSpec((rows, LANES), lambda i: (i, 0)), vec, vec],
        out_specs=pl.BlockSpec((rows, LANES), lambda i: (i, 0)),
        out_shape=jax.ShapeDtypeStruct((m, LANES), F32),
        compiler_params=_cparams(("parallel",)),
        name="gdn_gates",
    )(ba, alog_l, dtb_l)


def _gdn_kernel(cd_ref, q_ref, k_ref, v_ref, z_ref, colg_ref, rowg_ref, gn_ref, o_ref,
                lv_ref, lmask, s_ref, uv_buf, wq_buf, kd_buf, at_buf, ubuf, obuf, *, n_tiles):
    b = pl.program_id(0)
    hg = pl.program_id(1)
    si = pl.program_id(2)
    R = GDN_ROWS
    hp = HEADS_PER_GROUP
    heads = range(hp)
    sls = [slice(hl * GDN_DK, (hl + 1) * GDN_DK) for hl in heads]
    wslot = si % 2
    rslot = 1 - wslot

    @pl.when(si == 0)
    def _():
        s_ref[...] = jnp.zeros_like(s_ref)
        uv_buf[1] = jnp.zeros(uv_buf.shape[1:], uv_buf.dtype)
        wq_buf[1] = jnp.zeros(wq_buf.shape[1:], wq_buf.dtype)
        kd_buf[1] = jnp.zeros(kd_buf.shape[1:], kd_buf.dtype)
        at_buf[1] = jnp.zeros(at_buf.shape[1:], at_buf.dtype)
        ri = lax.broadcasted_iota(jnp.int32, (R, R), 0)
        ci = lax.broadcasted_iota(jnp.int32, (R, R), 1)
        x = ri ^ ci
        lv = jnp.full((R, R), -1, jnp.int32)
        for t in range(LOG2_CHUNK):
            lv = lv + jnp.where(x >= (1 << t), 1, 0)
        ok = ((ri >> LOG2_CHUNK) == (ci >> LOG2_CHUNK)) & (ri >= ci)
        lv = jnp.where(ok, lv, -2)
        lv_ref[...] = lv
        for level in range(1, LOG2_CHUNK):
            lmask[level - 1] = jnp.where(lv == level, 1.0, 0.0).astype(BF16)

    p1 = {}

    def p1_scores():
        p1["qk_kk"] = []
        for hl in heads:
            kb = k_ref[:, sls[hl]]
            p1["qk_kk"].append(lax.dot_general(jnp.concatenate([q_ref[:, sls[hl]], kb], axis=0), kb,
                                               (((1,), (1,)), ((), ())), preferred_element_type=F32))

    def p1_decay():
        lv = lv_ref[...]
        incl = lv >= -1
        colg = colg_ref[...]
        p1["beta"] = [colg[:, hl:hl + 1] for hl in heads]
        p1["exp_g"] = [jnp.exp(colg[:, hp + hl:hp + hl + 1]) for hl in heads]
        p1["low"], p1["xinv"] = [], []
        for hl in heads:
            gap = colg[:, hp + hl:hp + hl + 1] - rowg_ref[hl:hl + 1, :]
            decay = jnp.exp(jnp.where(incl, gap, NEG_BIG))
            at_buf[wslot, hl] = (decay * p1["qk_kk"][hl][0:R]).astype(BF16)
            lo = jnp.where(lv >= 0, p1["beta"][hl] * decay * p1["qk_kk"][hl][R:2 * R], 0.0)
            p1["low"].append(lo.astype(BF16))
            p1["xinv"].append((jnp.where(lv == -1, 1.0, 0.0) - jnp.where(lv == 0, lo, 0.0)).astype(BF16))

    def p1_level_a(level):
        def run():
            p1["ys"] = [jnp.dot(p1["low"][hl] * lmask[level - 1], p1["xinv"][hl],
                                preferred_element_type=F32).astype(BF16) for hl in heads]
        return run

    def p1_level_b():
        p1["xinv"] = [p1["xinv"][hl] - jnp.dot(p1["xinv"][hl], p1["ys"][hl],
                                               preferred_element_type=F32).astype(BF16) for hl in heads]

    def p1_solve():
        p1["sol"] = []
        for hl in heads:
            bk = (p1["beta"][hl] * p1["exp_g"][hl]) * k_ref[:, sls[hl]].astype(F32)
            rhs = jnp.concatenate([p1["beta"][hl] * v_ref[:, sls[hl]].astype(F32), bk], axis=1)
            p1["sol"].append(jnp.dot(p1["xinv"][hl], rhs.astype(BF16), preferred_element_type=F32))

    def p1_store():
        colg = colg_ref[...]
        for hl in heads:
            q_dec = q_ref[:, sls[hl]].astype(F32) * p1["exp_g"][hl]
            uv_buf[wslot, hl] = p1["sol"][hl][:, 0:GDN_DV]
            wq_buf[wslot, hl, 0:R, :] = p1["sol"][hl][:, GDN_DV:].astype(BF16)
            wq_buf[wslot, hl, R:2 * R, :] = q_dec.astype(BF16)
            kd_buf[wslot, hl] = (k_ref[:, sls[hl]].astype(F32)
                                 * jnp.exp(colg[:, 2 * hp + hl:2 * hp + hl + 1])).astype(BF16)

    p1_stages = [p1_scores, p1_decay]
    for level in range(1, LOG2_CHUNK):
        p1_stages += [p1_level_a(level), p1_level_b]
    p1_stages += [p1_solve, p1_store]

    p2 = {}
    tile = jnp.maximum(si - 1, 0)

    def p2_read(c):
        def run():
            p2["states"] = [s_ref[hl] for hl in heads]
            p2["wqs"] = []
            for hl in heads:
                lhs = jnp.concatenate([wq_buf[rslot, hl, c * CHUNK:(c + 1) * CHUNK, :],
                                       wq_buf[rslot, hl, R + c * CHUNK:R + (c + 1) * CHUNK, :]], axis=0)
                p2["wqs"].append(jnp.dot(lhs, p2["states"][hl].astype(BF16),
                                         preferred_element_type=F32))
        return run

    def p2_update(c):
        def run():
            rs = slice(c * CHUNK, (c + 1) * CHUNK)
            for hl in heads:
                u = (uv_buf[rslot, hl, rs, :] - p2["wqs"][hl][0:CHUNK]).astype(BF16)
                ubuf[hl, rs, :] = u
                obuf[hl, rs, :] = p2["wqs"][hl][CHUNK:]
                cd = cd_ref[b * GDN_HEADS + hg * HEADS_PER_GROUP + hl, tile * CHUNKS_PER_STEP + c]
                upd = lax.dot_general(kd_buf[rslot, hl, rs, :], u, (((0,), (0,)), ((), ())),
                                      preferred_element_type=F32)
                s_ref[hl] = cd * p2["states"][hl] + upd
        return run

    def p2_output():
        for hl in heads:
            o = obuf[hl] + jnp.dot(at_buf[rslot, hl], ubuf[hl], preferred_element_type=F32)
            on = _rms(o, gn_ref[...]) * z_ref[:, sls[hl]].astype(F32)
            o_ref[:, sls[hl]] = on.astype(o_ref.dtype)

    p2_stages = []
    for c in range(CHUNKS_PER_STEP):
        p2_stages += [p2_read(c), p2_update(c)]
    p2_stages.append(p2_output)

    n1, n2 = len(p1_stages), len(p2_stages)
    i2 = 0
    for i1, stage in enumerate(p1_stages):
        stage()
        while i2 < n2 and (i2 + 1) * n1 <= (i1 + 1) * n2:
            p2_stages[i2]()
            i2 += 1
    while i2 < n2:
        p2_stages[i2]()
        i2 += 1


def _gdn(qkvz, colg4, rowg4, cdecay, gdn_norm, batch, seq):
    m = qkvz.shape[0]
    ns = seq // GDN_ROWS
    gw = GROUP_W
    R = GDN_ROWS

    def cur(s):
        return jnp.minimum(s, ns - 1)

    def prv(s):
        return jnp.maximum(s - 1, 0)

    def xspec(col0):
        return pl.BlockSpec((R, gw), lambda b, h, s: (b * ns + cur(s), col0 // gw + h))

    hp = HEADS_PER_GROUP
    return pl.pallas_call(
        functools.partial(_gdn_kernel, n_tiles=ns),
        grid=(batch, HEAD_GROUPS, ns + 1),
        in_specs=[
            pl.BlockSpec(memory_space=pltpu.SMEM),
            xspec(0), xspec(QK_W), xspec(2 * QK_W),
            pl.BlockSpec((R, gw), lambda b, h, s: (b * ns + prv(s), COL_Z // gw + h)),
            pl.BlockSpec((None, R, LANES), lambda b, h, s: (h, b * ns + cur(s), 0)),
            pl.BlockSpec((None, None, hp, R), lambda b, h, s: (b, h, 0, cur(s))),
            pl.BlockSpec((1, GDN_DV), lambda b, h, s: (0, 0)),
        ],
        out_specs=pl.BlockSpec((R, gw), lambda b, h, s: (b * ns + prv(s), h)),
        out_shape=jax.ShapeDtypeStruct((m, V_W), BF16),
        scratch_shapes=[
            pltpu.VMEM((R, R), jnp.int32),
            pltpu.VMEM((LOG2_CHUNK - 1, R, R), BF16),
            pltpu.VMEM((hp, GDN_DK, GDN_DV), F32),
            pltpu.VMEM((2, hp, R, GDN_DV), F32),
            pltpu.VMEM((2, hp, 2 * R, GDN_DK), BF16),
            pltpu.VMEM((2, hp, R, GDN_DK), BF16),
            pltpu.VMEM((2, hp, R, R), BF16),
            pltpu.VMEM((hp, R, GDN_DV), BF16),
            pltpu.VMEM((hp, R, GDN_DV), F32),
        ],
        compiler_params=_cparams(("parallel", "parallel", "arbitrary")),
        name="gdn",
    )(cdecay, qkvz, qkvz, qkvz, qkvz, colg4, rowg4, gdn_norm.reshape(1, GDN_DV))


POOL_HALO = 16


def _pool_kernel(p_ref, w_ref, sc_ref, o_ref, buf):
    si = pl.program_id(1)
    R = p_ref.shape[0]

    @pl.when(si == 0)
    def _():
        buf[0:POOL_HALO, :] = jnp.zeros((POOL_HALO, POOL_WIDTH), F32)

    buf[POOL_HALO:POOL_HALO + R, :] = p_ref[...].astype(F32)
    t = si * R + lax.broadcasted_iota(jnp.int32, (R, 1), 0)
    for gi, win in enumerate(POOL_WINDOWS):
        sl = slice(gi * POOL_GROUP, (gi + 1) * POOL_GROUP)
        x = buf[POOL_HALO:POOL_HALO + R, sl]
        acc = x
        for d in range(1, win):
            acc = acc + buf[POOL_HALO - d:POOL_HALO - d + R, sl]
        cnt = jnp.minimum(t + 1, win).astype(F32)
        y = acc / cnt - x
        o = _bdot(y, w_ref[gi]) * sc_ref[:, sl]
        o_ref[:, sl] = o.astype(o_ref.dtype)
    buf[0:POOL_HALO, :] = buf[R:R + POOL_HALO, :]


def _pool(gate, pool_w, pool_scale, batch, seq):
    m = gate.shape[0]
    rows = TILES["pool_rows"]
    ns = seq // rows
    return pl.pallas_call(
        _pool_kernel,
        grid=(batch, ns),
        in_specs=[
            pl.BlockSpec((rows, POOL_WIDTH), lambda b, s: (b * ns + s, GCOL_P // POOL_WIDTH)),
            pl.BlockSpec((len(POOL_WINDOWS), POOL_GROUP, POOL_GROUP), lambda b, s: (0, 0, 0)),
            pl.BlockSpec((1, POOL_WIDTH), lambda b, s: (0, 0)),
        ],
        out_specs=pl.BlockSpec((rows, POOL_WIDTH), lambda b, s: (b * ns + s, 0)),
        out_shape=jax.ShapeDtypeStruct((m, POOL_WIDTH), BF16),
        scratch_shapes=[pltpu.VMEM((rows + POOL_HALO, POOL_WIDTH), F32)],
        compiler_params=_cparams(("parallel", "arbitrary")),
        name="pool",
    )(gate, pool_w, pool_scale.reshape(1, POOL_WIDTH))


def _merge_kernel(a_ref, p_ref, wa_ref, wb_ref, ga_ref, gb_ref, o_ref):
    tm, tn = o_ref.shape
    rb = PROJ_ROWS
    for c in range(tn // PROJ_SUB):
        cs = slice(c * PROJ_SUB, (c + 1) * PROJ_SUB)
        for r in range(tm // rb):
            rs = slice(r * rb, (r + 1) * rb)
            ya = jnp.dot(a_ref[rs, :], wa_ref[:, cs], preferred_element_type=F32)
            yb = jnp.dot(p_ref[rs, :], wb_ref[:, cs], preferred_element_type=F32)
            merged = ga_ref[rs, cs].astype(F32) * ya + gb_ref[rs, cs].astype(F32) * yb
            o_ref[rs, cs] = merged.astype(o_ref.dtype)


def _merge(og, pooled, wa, wb, gate):
    m = og.shape[0]
    n = wa.shape[1]
    tm, tn = TILES["merge_tm"], TILES["merge_tn"]
    return pl.pallas_call(
        _merge_kernel,
        grid=(m // tm, n // tn),
        in_specs=[
            pl.BlockSpec((tm, V_W), lambda i, j: (i, 0)),
            pl.BlockSpec((tm, POOL_WIDTH), lambda i, j: (i, 0)),
            pl.BlockSpec((V_W, tn), lambda i, j: (0, j)),
            pl.BlockSpec((POOL_WIDTH, tn), lambda i, j: (0, j)),
            pl.BlockSpec((tm, tn), lambda i, j: (i, GCOL_GA // tn + j)),
            pl.BlockSpec((tm, tn), lambda i, j: (i, GCOL_GB // tn + j)),
        ],
        out_specs=pl.BlockSpec((tm, tn), lambda i, j: (i, j)),
        out_shape=jax.ShapeDtypeStruct((m, n), BF16),
        compiler_params=_cparams(("parallel", "arbitrary")),
        name="merge",
    )(og, pooled, wa, wb, gate, gate)


def _xattn_kernel(q_ref, k_ref, v_ref, o_ref):
    s = lax.dot_general(q_ref[...], k_ref[...], (((1,), (1,)), ((), ())),
                        preferred_element_type=F32) * (XA_HEAD_DIM ** -0.5)
    s = s - jnp.max(s, axis=-1, keepdims=True)
    e = jnp.exp(s)
    p = e / jnp.sum(e, axis=-1, keepdims=True)
    o_ref[...] = jnp.dot(p.astype(BF16), v_ref[...], preferred_element_type=F32).astype(o_ref.dtype)


def _xattn(q, kv, batch, seq, mem_len):
    m = q.shape[0]
    ts = TILES["xattn_ts"]
    assert seq % ts == 0, (seq, ts)
    ns = seq // ts
    return pl.pallas_call(
        _xattn_kernel,
        grid=(batch, ns, XA_HEADS),
        in_specs=[
            pl.BlockSpec((ts, XA_HEAD_DIM), lambda b, s, h: (b * ns + s, h)),
            pl.BlockSpec((mem_len, XA_HEAD_DIM), lambda b, s, h: (b, h)),
            pl.BlockSpec((mem_len, XA_HEAD_DIM), lambda b, s, h: (b, XA_HEADS + h)),
        ],
        out_specs=pl.BlockSpec((ts, XA_HEAD_DIM), lambda b, s, h: (b * ns + s, h)),
        out_shape=jax.ShapeDtypeStruct((m, D_MODEL), BF16),
        compiler_params=_cparams(("parallel", "parallel", "arbitrary")),
        name="xattn",
    )(q, kv, kv)


def _ffn_up_kernel(h_ref, wa_ref, wb_ref, cwa_ref, cwb_ref, ba_ref, bb_ref, o_ref, ubuf, carry,
                   *, tiles_per_seq):
    i = pl.program_id(0)
    j = pl.program_id(1)
    tm = h_ref.shape[0]
    first = i % tiles_per_seq == 0
    halves = []
    for t, (w_ref, cw_ref, b_ref) in enumerate(((wa_ref, cwa_ref, ba_ref), (wb_ref, cwb_ref, bb_ref))):
        u = jnp.dot(h_ref[...], w_ref[...], preferred_element_type=F32)
        prev = jnp.where(first, 0.0, carry[t, j])
        carry[t, j] = u[tm - SUBLANES:tm, :]
        halves.append(_causal_conv(u, prev, ubuf.at[t], cw_ref[...]) + b_ref[...])
    o_ref[...] = (_silu(halves[0]) * halves[1]).astype(o_ref.dtype)


def _ffn_up(h, wa, wb, cwa, cwb, ba, bb, seq):
    m, k = h.shape
    n = wa.shape[1]
    tm, tn = TILES["ffn_tm"], TILES["ffn_tn"]
    wspec = pl.BlockSpec((k, tn), lambda i, j: (0, j))
    cspec = pl.BlockSpec((FFN_CONV, tn), lambda i, j: (0, j))
    bspec = pl.BlockSpec((1, tn), lambda i, j: (0, j))
    return pl.pallas_call(
        functools.partial(_ffn_up_kernel, tiles_per_seq=seq // tm),
        grid=(m // tm, n // tn),
        in_specs=[pl.BlockSpec((tm, k), lambda i, j: (i, 0)), wspec, wspec, cspec, cspec, bspec, bspec],
        out_specs=pl.BlockSpec((tm, tn), lambda i, j: (i, j)),
        out_shape=jax.ShapeDtypeStruct((m, n), BF16),
        scratch_shapes=[
            pltpu.VMEM((2, tm + SUBLANES, tn), F32),
            pltpu.VMEM((2, n // tn, SUBLANES, tn), F32),
        ],
        compiler_params=_cparams(("arbitrary", "arbitrary")),
        name="ffn_up",
    )(h, wa, wb, cwa, cwb, ba, bb)


def _pad_cols(a, n):
    return jnp.pad(a, ((0, 0), (0, n - a.shape[1])))


def _layer(x2d, mem2d, batch, seq, mem_len, p):
    m = x2d.shape[0]
    h = GDN_HEADS
    hp = HEADS_PER_GROUP
    w_in = p["w_in"]
    o_z = QKV_W
    o_b = o_z + V_W
    o_p = o_b + 2 * h
    w_qkvz = w_in[:, :o_b].astype(BF16)
    w_gate = w_in[:, o_p:].astype(BF16)
    w_ba = _pad_cols(w_in[:, o_b:o_p], LANES).astype(BF16)
    alog_l = jnp.zeros((1, LANES), F32).at[0, h:2 * h].set(p["a_log"])
    dtb_l = jnp.zeros((1, LANES), F32).at[0, h:2 * h].set(p["dt_bias"])
    w_up = p["w_up"]
    wa_up = _pad_cols(w_up[:, :D_FF], D_FF_PAD).astype(BF16)
    wb_up = _pad_cols(w_up[:, D_FF:], D_FF_PAD).astype(BF16)
    cwa = _pad_cols(p["ffn_conv_w"][:, :D_FF], D_FF_PAD)
    cwb = _pad_cols(p["ffn_conv_w"][:, D_FF:], D_FF_PAD)
    fba = _pad_cols(p["ffn_conv_b"][None, :D_FF], D_FF_PAD)
    fbb = _pad_cols(p["ffn_conv_b"][None, D_FF:], D_FF_PAD)
    w_down = jnp.pad(p["w_down"], ((0, D_FF_PAD - D_FF), (0, 0))).astype(BF16)

    qkvz, ba, hn = _qkv_proj(x2d, p["mix_pre_norm"], w_qkvz, p["conv_qkv"], w_ba, seq)
    gate = _gate_proj(hn, w_gate)
    gates = _gates(ba, alog_l, dtb_l)
    if HEAD_GROUPS == 1:
        colg4 = gates[None]
    else:
        g4 = gates[:, :3 * h].reshape(m, 3, HEAD_GROUPS, hp).transpose(2, 0, 1, 3)
        colg4 = jnp.pad(g4.reshape(HEAD_GROUPS, m, 3 * hp), ((0, 0), (0, 0), (0, LANES - 3 * hp)))
    rowg4 = gates[:, h:2 * h].reshape(batch, seq, HEAD_GROUPS, hp).transpose(0, 2, 3, 1)
    cdecay = gates[CHUNK - 1::CHUNK, 3 * h:4 * h].reshape(batch, seq // CHUNK, h)
    cdecay = cdecay.transpose(0, 2, 1).reshape(batch * h, seq // CHUNK)
    og = _gdn(qkvz, colg4, rowg4, cdecay, p["gdn_norm"], batch, seq)
    pooled = _pool(gate, p["pool_w"].astype(BF16), p["pool_scale"], batch, seq)
    merged = _merge(og, pooled, p["w_branch_a"].astype(BF16), p["w_branch_b"].astype(BF16), gate)
    x1, q = _mm_norm_res(merged, p["w_mix_out"].astype(BF16), x2d, p["mix_post_norm"],
                         p["xa_pre_norm"], tm=TILES["res_tm"], next_proj=p["w_xq"].astype(BF16))
    kv = _norm_mm(mem2d, p["mem_norm"], p["w_xkv"].astype(BF16), tm=mem2d.shape[0], tn=1024,
                  out_dtype=BF16)
    xo = _xattn(q, kv, batch, seq, mem_len)
    x2, h3 = _mm_norm_res(xo, p["w_xo"].astype(BF16), x1, p["xa_post_norm"], p["ffn_pre_norm"],
                          tm=TILES["res_tm"])
    act = _ffn_up(h3, wa_up, wb_up, cwa, cwb, fba, fbb, seq)
    x3, _ = _mm_norm_res(act, w_down, x2, p["ffn_post_norm"], None, tm=TILES["down_tm"])
    return x3


def kernel(x, mem, mix_pre_norm, w_in, conv_qkv, a_log, dt_bias, gdn_norm, pool_w, pool_scale,
           w_branch_a, w_branch_b, w_mix_out, mix_post_norm, xa_pre_norm, mem_norm, w_xq, w_xkv,
           w_xo, xa_post_norm, ffn_pre_norm, w_up, ffn_conv_w, ffn_conv_b, w_down, ffn_post_norm):
    params = dict(mix_pre_norm=mix_pre_norm, w_in=w_in, conv_qkv=conv_qkv, a_log=a_log,
                  dt_bias=dt_bias, gdn_norm=gdn_norm, pool_w=pool_w, pool_scale=pool_scale,
                  w_branch_a=w_branch_a, w_branch_b=w_branch_b, w_mix_out=w_mix_out,
                  mix_post_norm=mix_post_norm, xa_pre_norm=xa_pre_norm, mem_norm=mem_norm,
                  w_xq=w_xq, w_xkv=w_xkv, w_xo=w_xo, xa_post_norm=xa_post_norm,
                  ffn_pre_norm=ffn_pre_norm, w_up=w_up, ffn_conv_w=ffn_conv_w,
                  ffn_conv_b=ffn_conv_b, w_down=w_down, ffn_post_norm=ffn_post_norm)
    batch, seq, d = x.shape
    mem_len = mem.shape[1]
    x2d = x.reshape(batch * seq, d)
    mem2d = mem.reshape(batch * mem_len, d)
    for l in range(w_in.shape[0]):
        x2d = _layer(x2d, mem2d, batch, seq, mem_len, {k: v[l] for k, v in params.items()})
    return x2d.reshape(batch, seq, d)
```

```python
import functools

import jax
import jax.numpy as jnp
from jax import lax
from jax.experimental import pallas as pl
from jax.experimental.pallas import tpu as pltpu

F32 = jnp.float32
BF16 = jnp.bfloat16

D_MODEL = 2048
CHUNK = 64
GDN_HEADS = 16
GDN_DK = 128
GDN_DV = 128
GDN_CONV = 4
QK_W = GDN_HEADS * GDN_DK
V_W = GDN_HEADS * GDN_DV
QKV_W = 2 * QK_W + V_W
POOL_WINDOWS = (2, 4, 8, 16)
POOL_WIDTH = D_MODEL // 2
POOL_GROUP = POOL_WIDTH // 4
XA_HEADS = 4
XA_HEAD_DIM = D_MODEL // XA_HEADS
D_FF = 5504
FFN_CONV = 3
EPS = 1e-6

LANES = 128
SUBLANES = 8
VMEM_LIMIT = 56 * 1024 * 1024

COL_Z = QKV_W
GCOL_P = 0
GCOL_GA = POOL_WIDTH
GCOL_GB = GCOL_GA + D_MODEL

HEADS_PER_GROUP = 16
HEAD_GROUPS = GDN_HEADS // HEADS_PER_GROUP
GROUP_W = HEADS_PER_GROUP * GDN_DK
GDN_ROWS = 128
CHUNKS_PER_STEP = GDN_ROWS // CHUNK
LOG2_CHUNK = 6
NEG_BIG = -1e30

D_FF_PAD = 5632
PROJ_SUB = 256
PROJ_ROWS = 256
CONV_PHASES = 4

TILES = dict(
    proj_tm=1024, proj_tn=1024,
    gate_rows=512,
    pool_rows=512,
    merge_tm=1024, merge_tn=1024,
    res_tm=512,
    xattn_ts=4096,
    ffn_tm=1024, ffn_tn=512,
    down_tm=512,
)


def _cparams(sem):
    return pltpu.CompilerParams(dimension_semantics=sem, vmem_limit_bytes=VMEM_LIMIT)


def _rms(xf, w):
    ms = jnp.mean(xf * xf, axis=-1, keepdims=True)
    return xf * lax.rsqrt(ms + EPS) * w


def _silu(x):
    hx = 0.5 * x
    return hx + hx * jnp.tanh(hx)


def _bdot(a, b):
    return jnp.dot(a.astype(BF16), b.astype(BF16), preferred_element_type=F32)


def _causal_conv(u, prev, ubuf, cw):
    tm = u.shape[0]
    width = cw.shape[0]
    ubuf[0:SUBLANES, :] = prev
    ubuf[SUBLANES:SUBLANES + tm, :] = u
    y = u * cw[width - 1:width, :]
    for d in range(width - 1):
        off = SUBLANES - (width - 1) + d
        y = y + ubuf[off:off + tm, :] * cw[d:d + 1, :]
    return y


def _qkv_proj_kernel(x_ref, nw_ref, w_ref, cw_ref, wba_ref, o_ref, ba_ref, h_ref, ubuf, stage, carry, *,
                     tiles_per_seq, tiles_per_part):
    i = pl.program_id(0)
    j = pl.program_id(1)
    tm, tn = o_ref.shape

    @pl.when(j == 0)
    def _():
        h_ref[...] = _rms(x_ref[...], nw_ref[...]).astype(BF16)
        ba_ref[...] = jnp.dot(h_ref[...], wba_ref[...], preferred_element_type=F32)

    is_q = j < tiles_per_part
    is_qk = j < 2 * tiles_per_part
    na = jnp.where(is_q, float(GDN_DK), jnp.where(is_qk, 1.0, 0.0))
    nb = jnp.where(is_q, GDN_DK * EPS, jnp.where(is_qk, EPS, 1.0))
    first = i % tiles_per_seq == 0

    @pl.when(j < 3 * tiles_per_part)
    def _():
        nph = CONV_PHASES
        for c in range(tn // PROJ_SUB):
            cs = slice(c * PROJ_SUB, (c + 1) * PROJ_SUB)
            u = jnp.dot(h_ref[...], w_ref[:, cs], preferred_element_type=F32)
            prev = jnp.where(first, 0.0, carry[j, :, cs])
            carry[j, :, cs] = u[tm - SUBLANES:tm, :]
            for s in range(PROJ_SUB // LANES):
                slab = c * (PROJ_SUB // LANES) + s
                ls = slice(s * LANES, (s + 1) * LANES)
                ubuf[slab, 0:SUBLANES, :] = prev[:, ls]
                ubuf[slab, SUBLANES:SUBLANES + tm, :] = u[:, ls]
            for s in range(PROJ_SUB // LANES):
                slab = c * (PROJ_SUB // LANES) + s
                cw = cw_ref[:, slab * LANES:(slab + 1) * LANES]
                for ph in range(nph):
                    y = None
                    for d in range(GDN_CONV):
                        start = SUBLANES - (GDN_CONV - 1) + d + ph
                        term = ubuf[slab, pl.ds(start, tm // nph, stride=nph), :] * cw[d:d + 1, :]
                        y = term if y is None else y + term
                    y = _silu(y)
                    ss = jnp.sum(y * y, axis=-1, keepdims=True)
                    f = lax.rsqrt(ss * na + nb)
                    stage[slab, pl.ds(ph, tm // nph, stride=nph), :] = y * f
                o_ref[:, slab * LANES:(slab + 1) * LANES] = stage[slab].astype(o_ref.dtype)

    @pl.when(j >= 3 * tiles_per_part)
    def _():
        rb = PROJ_ROWS
        for c in range(tn // PROJ_SUB):
            cs = slice(c * PROJ_SUB, (c + 1) * PROJ_SUB)
            for r in range(tm // rb):
                rs = slice(r * rb, (r + 1) * rb)
                u = jnp.dot(h_ref[rs, :], w_ref[:, cs], preferred_element_type=F32)
                o_ref[rs, cs] = _silu(u).astype(o_ref.dtype)


def _qkv_proj(x, nw, w, conv_w, w_ba, seq):
    m, k = x.shape
    n = w.shape[1]
    tm, tn = TILES["proj_tm"], TILES["proj_tn"]
    return pl.pallas_call(
        functools.partial(_qkv_proj_kernel, tiles_per_seq=seq // tm, tiles_per_part=QK_W // tn),
        grid=(m // tm, n // tn),
        in_specs=[
            pl.BlockSpec((tm, k), lambda i, j: (i, 0)),
            pl.BlockSpec((1, k), lambda i, j: (0, 0)),
            pl.BlockSpec((k, tn), lambda i, j: (0, j)),
            pl.BlockSpec((GDN_CONV, tn), lambda i, j: (0, jnp.minimum(j, QKV_W // tn - 1))),
            pl.BlockSpec((k, LANES), lambda i, j: (0, 0)),
        ],
        out_specs=[pl.BlockSpec((tm, tn), lambda i, j: (i, j)),
                   pl.BlockSpec((tm, LANES), lambda i, j: (i, 0)),
                   pl.BlockSpec((tm, k), lambda i, j: (i, 0))],
        out_shape=[jax.ShapeDtypeStruct((m, n), BF16), jax.ShapeDtypeStruct((m, LANES), F32),
                   jax.ShapeDtypeStruct((m, k), BF16)],
        scratch_shapes=[
            pltpu.VMEM((tn // LANES, tm + SUBLANES, LANES), F32),
            pltpu.VMEM((tn // LANES, tm, LANES), F32),
            pltpu.VMEM((n // tn, SUBLANES, tn), F32),
        ],
        compiler_params=_cparams(("arbitrary", "arbitrary")),
        name="qkv_proj",
    )(x, nw.reshape(1, k), w, conv_w, w_ba)


def _gate_proj_kernel(h_ref, w_ref, o_ref, *, pool_tiles):
    j = pl.program_id(1)
    is_pool = j < pool_tiles
    tm, tn = o_ref.shape
    rb = PROJ_ROWS
    for c in range(tn // PROJ_SUB):
        cs = slice(c * PROJ_SUB, (c + 1) * PROJ_SUB)
        for r in range(tm // rb):
            rs = slice(r * rb, (r + 1) * rb)
            u = jnp.dot(h_ref[rs, :], w_ref[:, cs], preferred_element_type=F32)
            out = jnp.where(is_pool, u, 0.5 + 0.5 * jnp.tanh(0.5 * u))
            o_ref[rs, cs] = out.astype(o_ref.dtype)


def _gate_proj(h, w):
    m, k = h.shape
    n = w.shape[1]
    tm, tn = TILES["proj_tm"], TILES["proj_tn"]
    return pl.pallas_call(
        functools.partial(_gate_proj_kernel, pool_tiles=POOL_WIDTH // tn),
        grid=(m // tm, n // tn),
        in_specs=[
            pl.BlockSpec((tm, k), lambda i, j: (i, 0)),
            pl.BlockSpec((k, tn), lambda i, j: (0, j)),
        ],
        out_specs=pl.BlockSpec((tm, tn), lambda i, j: (i, j)),
        out_shape=jax.ShapeDtypeStruct((m, n), BF16),
        compiler_params=_cparams(("parallel", "arbitrary")),
        name="gate_proj",
    )(h, w)


def _norm_mm_kernel(x_ref, nw_ref, w_ref, o_ref, h_ref):
    @pl.when(pl.program_id(1) == 0)
    def _():
        h_ref[...] = _rms(x_ref[...], nw_ref[...]).astype(BF16)

    o_ref[...] = jnp.dot(h_ref[...], w_ref[...], preferred_element_type=F32).astype(o_ref.dtype)


def _norm_mm(x, nw, w, *, tm, tn, out_dtype):
    m, k = x.shape
    n = w.shape[1]
    return pl.pallas_call(
        _norm_mm_kernel,
        grid=(m // tm, n // tn),
        in_specs=[
            pl.BlockSpec((tm, k), lambda i, j: (i, 0)),
            pl.BlockSpec((1, k), lambda i, j: (0, 0)),
            pl.BlockSpec((k, tn), lambda i, j: (0, j)),
        ],
        out_specs=pl.BlockSpec((tm, tn), lambda i, j: (i, j)),
        out_shape=jax.ShapeDtypeStruct((m, n), out_dtype),
        scratch_shapes=[pltpu.VMEM((tm, k), BF16)],
        compiler_params=_cparams(("parallel", "arbitrary")),
        name="norm_mm",
    )(x, nw.reshape(1, k), w)


def _mm_norm_res_kernel(a_ref, w_ref, res_ref, pw_ref, *rest, with_next, with_proj):
    rest = list(rest)
    nw_ref = rest.pop(0) if with_next else None
    w2_ref = rest.pop(0) if with_proj else None
    x_ref = rest.pop(0)
    y = jnp.dot(a_ref[...], w_ref[...], preferred_element_type=F32)
    xn = res_ref[...] + _rms(y, pw_ref[...])
    x_ref[...] = xn
    if with_next:
        h = _rms(xn, nw_ref[...]).astype(BF16)
        if with_proj:
            h = jnp.dot(h, w2_ref[...], preferred_element_type=F32).astype(BF16)
        rest[0][...] = h


def _mm_norm_res(a, w, res, post_w, next_w, *, tm, next_proj=None):
    m, k = a.shape
    n = w.shape[1]
    with_next = next_w is not None
    with_proj = next_proj is not None
    row = pl.BlockSpec((tm, n), lambda i: (i, 0))
    vec = pl.BlockSpec((1, n), lambda i: (0, 0))
    wspec = pl.BlockSpec((k, n), lambda i: (0, 0), pipeline_mode=pl.Buffered(1))
    in_specs = [pl.BlockSpec((tm, k), lambda i: (i, 0)), wspec, row, vec]
    args = [a, w, res, post_w.reshape(1, n)]
    out_specs = [row]
    out_shape = [jax.ShapeDtypeStruct((m, n), F32)]
    if with_next:
        in_specs.append(vec)
        args.append(next_w.reshape(1, n))
        if with_proj:
            in_specs.append(pl.BlockSpec(next_proj.shape, lambda i: (0, 0), pipeline_mode=pl.Buffered(1)))
            args.append(next_proj)
        out_specs.append(row)
        out_shape.append(jax.ShapeDtypeStruct((m, n), BF16))
    outs = pl.pallas_call(
        functools.partial(_mm_norm_res_kernel, with_next=with_next, with_proj=with_proj),
        grid=(m // tm,),
        in_specs=in_specs,
        out_specs=out_specs,
        out_shape=out_shape,
        compiler_params=_cparams(("parallel",)),
        name="mm_norm_res",
    )(*args)
    return outs if with_next else (outs[0], None)


def _gates_kernel(ba_ref, alog_ref, dtb_ref, o_ref):
    rows = ba_ref.shape[0]
    ba = ba_ref[...]
    beta = jax.nn.sigmoid(ba)
    xs = ba + dtb_ref[...]
    softplus = jnp.maximum(xs, 0.0) + jnp.log1p(jnp.exp(-jnp.abs(xs)))
    g = -jnp.exp(alog_ref[...]) * softplus
    ri = lax.broadcasted_iota(jnp.int32, (CHUNK, CHUNK), 0)
    ci = lax.broadcasted_iota(jnp.int32, (CHUNK, CHUNK), 1)
    tri = jnp.where(ri >= ci, 1.0, 0.0).astype(F32)
    ones = jnp.ones((CHUNK, CHUNK), F32)
    gcs, gls = [], []
    for c in range(rows // CHUNK):
        gch = g[c * CHUNK:(c + 1) * CHUNK]
        gcs.append(jnp.dot(tri, gch, preferred_element_type=F32, precision=lax.Precision.HIGHEST))
        gls.append(jnp.dot(ones, gch, preferred_element_type=F32, precision=lax.Precision.HIGHEST))
    gc = jnp.concatenate(gcs, axis=0)
    gl = jnp.concatenate(gls, axis=0)
    lane = lax.broadcasted_iota(jnp.int32, ba.shape, 1)
    h = GDN_HEADS
    out = jnp.where(lane < h, beta, gc)
    out = jnp.where(lane < 2 * h, out, pltpu.roll(gl - gc, h, axis=1))
    out = jnp.where(lane < 3 * h, out, pltpu.roll(jnp.exp(gl), 2 * h, axis=1))
    o_ref[...] = out


def _gates(ba, alog_l, dtb_l):
    m = ba.shape[0]
    rows = TILES["gate_rows"]
    vec = pl.BlockSpec((1, LANES), lambda i: (0, 0))
    return pl.pallas_call(
        _gates_kernel,
        grid=(m // rows,),
        in_specs=[pl.BlockSpec((rows, LANES), lambda i: (i, 0)), vec, vec],
        out_specs=pl.BlockSpec((rows, LANES), lambda i: (i, 0)),
        out_shape=jax.ShapeDtypeStruct((m, LANES), F32),
        compiler_params=_cparams(("parallel",)),
        name="gdn_gates",
    )(ba, alog_l, dtb_l)


def _gdn_kernel(cd_ref, q_ref, k_ref, v_ref, z_ref, colg_ref, rowg_ref, gn_ref, o_ref,
                lv_ref, lmask, s_ref, uv_buf, wq_buf, kd_buf, at_buf, ubuf, obuf, *, n_tiles):
    b = pl.program_id(0)
    hg = pl.program_id(1)
    si = pl.program_id(2)
    R = GDN_ROWS
    hp = HEADS_PER_GROUP
    heads = range(hp)
    sls = [slice(hl * GDN_DK, (hl + 1) * GDN_DK) for hl in heads]
    wslot = si % 2
    rslot = 1 - wslot

    @pl.when(si == 0)
    def _():
        s_ref[...] = jnp.zeros_like(s_ref)
        uv_buf[1] = jnp.zeros(uv_buf.shape[1:], uv_buf.dtype)
        wq_buf[1] = jnp.zeros(wq_buf.shape[1:], wq_buf.dtype)
        kd_buf[1] = jnp.zeros(kd_buf.shape[1:], kd_buf.dtype)
        at_buf[1] = jnp.zeros(at_buf.shape[1:], at_buf.dtype)
        ri = lax.broadcasted_iota(jnp.int32, (R, R), 0)
        ci = lax.broadcasted_iota(jnp.int32, (R, R), 1)
        x = ri ^ ci
        lv = jnp.full((R, R), -1, jnp.int32)
        for t in range(LOG2_CHUNK):
            lv = lv + jnp.where(x >= (1 << t), 1, 0)
        ok = ((ri >> LOG2_CHUNK) == (ci >> LOG2_CHUNK)) & (ri >= ci)
        lv = jnp.where(ok, lv, -2)
        lv_ref[...] = lv
        for level in range(1, LOG2_CHUNK):
            lmask[level - 1] = jnp.where(lv == level, 1.0, 0.0).astype(BF16)

    p1 = {}

    def p1_scores():
        p1["qk_kk"] = []
        for hl in heads:
            kb = k_ref[:, sls[hl]]
            p1["qk_kk"].append(lax.dot_general(jnp.concatenate([q_ref[:, sls[hl]], kb], axis=0), kb,
                                               (((1,), (1,)), ((), ())), preferred_element_type=F32))

    def p1_decay():
        lv = lv_ref[...]
        incl = lv >= -1
        colg = colg_ref[...]
        p1["beta"] = [colg[:, hl:hl + 1] for hl in heads]
        p1["exp_g"] = [jnp.exp(colg[:, hp + hl:hp + hl + 1]) for hl in heads]
        p1["low"], p1["xinv"] = [], []
        for hl in heads:
            gap = colg[:, hp + hl:hp + hl + 1] - rowg_ref[hl:hl + 1, :]
            decay = jnp.exp(jnp.where(incl, gap, NEG_BIG))
            at_buf[wslot, hl] = (decay * p1["qk_kk"][hl][0:R]).astype(BF16)
            lo = jnp.where(lv >= 0, p1["beta"][hl] * decay * p1["qk_kk"][hl][R:2 * R], 0.0)
            p1["low"].append(lo.astype(BF16))
            p1["xinv"].append((jnp.where(lv == -1, 1.0, 0.0) - jnp.where(lv == 0, lo, 0.0)).astype(BF16))

    def p1_level_a(level):
        def run():
            p1["ys"] = [jnp.dot(p1["low"][hl] * lmask[level - 1], p1["xinv"][hl],
                                preferred_element_type=F32).astype(BF16) for hl in heads]
        return run

    def p1_level_b():
        p1["xinv"] = [p1["xinv"][hl] - jnp.dot(p1["xinv"][hl], p1["ys"][hl],
                                               preferred_element_type=F32).astype(BF16) for hl in heads]

    def p1_solve():
        p1["sol"] = []
        for hl in heads:
            bk = (p1["beta"][hl] * p1["exp_g"][hl]) * k_ref[:, sls[hl]].astype(F32)
            rhs = jnp.concatenate([p1["beta"][hl] * v_ref[:, sls[hl]].astype(F32), bk], axis=1)
            p1["sol"].append(jnp.dot(p1["xinv"][hl], rhs.astype(BF16), preferred_element_type=F32))

    def p1_store():
        colg = colg_ref[...]
        for hl in heads:
            q_dec = q_ref[:, sls[hl]].astype(F32) * p1["exp_g"][hl]
            uv_buf[wslot, hl] = p1["sol"][hl][:, 0:GDN_DV]
            wq_buf[wslot, hl, 0:R, :] = p1["sol"][hl][:, GDN_DV:].astype(BF16)
            wq_buf[wslot, hl, R:2 * R, :] = q_dec.astype(BF16)
            kd_buf[wslot, hl] = (k_ref[:, sls[hl]].astype(F32)
                                 * jnp.exp(colg[:, 2 * hp + hl:2 * hp + hl + 1])).astype(BF16)

    p1_stages = [p1_scores, p1_decay]
    for level in range(1, LOG2_CHUNK):
        p1_stages += [p1_level_a(level), p1_level_b]
    p1_stages += [p1_solve, p1_store]

    p2 = {}
    tile = jnp.maximum(si - 1, 0)

    def p2_read(c):
        def run():
            p2["states"] = [s_ref[hl] for hl in heads]
            p2["wqs"] = []
            for hl in heads:
                lhs = jnp.concatenate([wq_buf[rslot, hl, c * CHUNK:(c + 1) * CHUNK, :],
                                       wq_buf[rslot, hl, R + c * CHUNK:R + (c + 1) * CHUNK, :]], axis=0)
                p2["wqs"].append(jnp.dot(lhs, p2["states"][hl].astype(BF16),
                                         preferred_element_type=F32))
        return run

    def p2_update(c):
        def run():
            rs = slice(c * CHUNK, (c + 1) * CHUNK)
            for hl in heads:
                u = (uv_buf[rslot, hl, rs, :] - p2["wqs"][hl][0:CHUNK]).astype(BF16)
                ubuf[hl, rs, :] = u
                obuf[hl, rs, :] = p2["wqs"][hl][CHUNK:]
                cd = cd_ref[b * GDN_HEADS + hg * HEADS_PER_GROUP + hl, tile * CHUNKS_PER_STEP + c]
                upd = lax.dot_general(kd_buf[rslot, hl, rs, :], u, (((0,), (0,)), ((), ())),
                                      preferred_element_type=F32)
                s_ref[hl] = cd * p2["states"][hl] + upd
        return run

    def p2_output():
        for hl in heads:
            o = obuf[hl] + jnp.dot(at_buf[rslot, hl], ubuf[hl], preferred_element_type=F32)
            on = _rms(o, gn_ref[...]) * z_ref[:, sls[hl]].astype(F32)
            o_ref[:, sls[hl]] = on.astype(o_ref.dtype)

    p2_stages = []
    for c in range(CHUNKS_PER_STEP):
        p2_stages += [p2_read(c), p2_update(c)]
    p2_stages.append(p2_output)

    n1, n2 = len(p1_stages), len(p2_stages)
    i2 = 0
    for i1, stage in enumerate(p1_stages):
        stage()
        while i2 < n2 and (i2 + 1) * n1 <= (i1 + 1) * n2:
            p2_stages[i2]()
            i2 += 1
    while i2 < n2:
        p2_stages[i2]()
        i2 += 1


def _gdn(qkvz, colg4, rowg4, cdecay, gdn_norm, batch, seq):
    m = qkvz.shape[0]
    ns = seq // GDN_ROWS
    gw = GROUP_W
    R = GDN_ROWS

    def cur(s):
        return jnp.minimum(s, ns - 1)

    def prv(s):
        return jnp.maximum(s - 1, 0)

    def xspec(col0):
        return pl.BlockSpec((R, gw), lambda b, h, s: (b * ns + cur(s), col0 // gw + h))

    hp = HEADS_PER_GROUP
    return pl.pallas_call(
        functools.partial(_gdn_kernel, n_tiles=ns),
        grid=(batch, HEAD_GROUPS, ns + 1),
        in_specs=[
            pl.BlockSpec(memory_space=pltpu.SMEM),
            xspec(0), xspec(QK_W), xspec(2 * QK_W),
            pl.BlockSpec((R, gw), lambda b, h, s: (b * ns + prv(s), COL_Z // gw + h)),
            pl.BlockSpec((None, R, LANES), lambda b, h, s: (h, b * ns + cur(s), 0)),
            pl.BlockSpec((None, None, hp, R), lambda b, h, s: (b, h, 0, cur(s))),
            pl.BlockSpec((1, GDN_DV), lambda b, h, s: (0, 0)),
        ],
        out_specs=pl.BlockSpec((R, gw), lambda b, h, s: (b * ns + prv(s), h)),
        out_shape=jax.ShapeDtypeStruct((m, V_W), BF16),
        scratch_shapes=[
            pltpu.VMEM((R, R), jnp.int32),
            pltpu.VMEM((LOG2_CHUNK - 1, R, R), BF16),
            pltpu.VMEM((hp, GDN_DK, GDN_DV), F32),
            pltpu.VMEM((2, hp, R, GDN_DV), F32),
            pltpu.VMEM((2, hp, 2 * R, GDN_DK), BF16),
            pltpu.VMEM((2, hp, R, GDN_DK), BF16),
            pltpu.VMEM((2, hp, R, R), BF16),
            pltpu.VMEM((hp, R, GDN_DV), BF16),
            pltpu.VMEM((hp, R, GDN_DV), F32),
        ],
        compiler_params=_cparams(("parallel", "parallel", "arbitrary")),
        name="gdn",
    )(cdecay, qkvz, qkvz, qkvz, qkvz, colg4, rowg4, gdn_norm.reshape(1, GDN_DV))


POOL_HALO = 16


def _pool_kernel(p_ref, w_ref, sc_ref, o_ref, buf):
    si = pl.program_id(1)
    R = p_ref.shape[0]

    @pl.when(si == 0)
    def _():
        buf[0:POOL_HALO, :] = jnp.zeros((POOL_HALO, POOL_WIDTH), F32)

    buf[POOL_HALO:POOL_HALO + R, :] = p_ref[...].astype(F32)
    t = si * R + lax.broadcasted_iota(jnp.int32, (R, 1), 0)
    for gi, win in enumerate(POOL_WINDOWS):
        sl = slice(gi * POOL_GROUP, (gi + 1) * POOL_GROUP)
        x = buf[POOL_HALO:POOL_HALO + R, sl]
        acc = x
        for d in range(1, win):
            acc = acc + buf[POOL_HALO - d:POOL_HALO - d + R, sl]
        cnt = jnp.minimum(t + 1, win).astype(F32)
        y = acc / cnt - x
        o = _bdot(y, w_ref[gi]) * sc_ref[:, sl]
        o_ref[:, sl] = o.astype(o_ref.dtype)
    buf[0:POOL_HALO, :] = buf[R:R + POOL_HALO, :]


def _pool(gate, pool_w, pool_scale, batch, seq):
    m = gate.shape[0]
    rows = TILES["pool_rows"]
    ns = seq // rows
    return pl.pallas_call(
        _pool_kernel,
        grid=(batch, ns),
        in_specs=[
            pl.BlockSpec((rows, POOL_WIDTH), lambda b, s: (b * ns + s, GCOL_P // POOL_WIDTH)),
            pl.BlockSpec((len(POOL_WINDOWS), POOL_GROUP, POOL_GROUP), lambda b, s: (0, 0, 0)),
            pl.BlockSpec((1, POOL_WIDTH), lambda b, s: (0, 0)),
        ],
        out_specs=pl.BlockSpec((rows, POOL_WIDTH), lambda b, s: (b * ns + s, 0)),
        out_shape=jax.ShapeDtypeStruct((m, POOL_WIDTH), BF16),
        scratch_shapes=[pltpu.VMEM((rows + POOL_HALO, POOL_WIDTH), F32)],
        compiler_params=_cparams(("parallel", "arbitrary")),
        name="pool",
    )(gate, pool_w, pool_scale.reshape(1, POOL_WIDTH))


def _merge_kernel(a_ref, p_ref, wa_ref, wb_ref, ga_ref, gb_ref, o_ref):
    tm, tn = o_ref.shape
    rb = PROJ_ROWS
    for c in range(tn // PROJ_SUB):
        cs = slice(c * PROJ_SUB, (c + 1) * PROJ_SUB)
        for r in range(tm // rb):
            rs = slice(r * rb, (r + 1) * rb)
            ya = jnp.dot(a_ref[rs, :], wa_ref[:, cs], preferred_element_type=F32)
            yb = jnp.dot(p_ref[rs, :], wb_ref[:, cs], preferred_element_type=F32)
            merged = ga_ref[rs, cs].astype(F32) * ya + gb_ref[rs, cs].astype(F32) * yb
            o_ref[rs, cs] = merged.astype(o_ref.dtype)


def _merge(og, pooled, wa, wb, gate):
    m = og.shape[0]
    n = wa.shape[1]
    tm, tn = TILES["merge_tm"], TILES["merge_tn"]
    return pl.pallas_call(
        _merge_kernel,
        grid=(m // tm, n // tn),
        in_specs=[
            pl.BlockSpec((tm, V_W), lambda i, j: (i, 0)),
            pl.BlockSpec((tm, POOL_WIDTH), lambda i, j: (i, 0)),
            pl.BlockSpec((V_W, tn), lambda i, j: (0, j)),
            pl.BlockSpec((POOL_WIDTH, tn), lambda i, j: (0, j)),
            pl.BlockSpec((tm, tn), lambda i, j: (i, GCOL_GA // tn + j)),
            pl.BlockSpec((tm, tn), lambda i, j: (i, GCOL_GB // tn + j)),
        ],
        out_specs=pl.BlockSpec((tm, tn), lambda i, j: (i, j)),
        out_shape=jax.ShapeDtypeStruct((m, n), BF16),
        compiler_params=_cparams(("parallel", "arbitrary")),
        name="merge",
    )(og, pooled, wa, wb, gate, gate)


def _xattn_kernel(q_ref, k_ref, v_ref, o_ref):
    s = lax.dot_general(q_ref[...], k_ref[...], (((1,), (1,)), ((), ())),
                        preferred_element_type=F32) * (XA_HEAD_DIM ** -0.5)
    s = s - jnp.max(s, axis=-1, keepdims=True)
    e = jnp.exp(s)
    p = e / jnp.sum(e, axis=-1, keepdims=True)
    o_ref[...] = jnp.dot(p.astype(BF16), v_ref[...], preferred_element_type=F32).astype(o_ref.dtype)


def _xattn(q, kv, batch, seq, mem_len):
    m = q.shape[0]
    ts = TILES["xattn_ts"]
    assert seq % ts == 0, (seq, ts)
    ns = seq // ts
    return pl.pallas_call(
        _xattn_kernel,
        grid=(batch, ns, XA_HEADS),
        in_specs=[
            pl.BlockSpec((ts, XA_HEAD_DIM), lambda b, s, h: (b * ns + s, h)),
            pl.BlockSpec((mem_len, XA_HEAD_DIM), lambda b, s, h: (b, h)),
            pl.BlockSpec((mem_len, XA_HEAD_DIM), lambda b, s, h: (b, XA_HEADS + h)),
        ],
        out_specs=pl.BlockSpec((ts, XA_HEAD_DIM), lambda b, s, h: (b * ns + s, h)),
        out_shape=jax.ShapeDtypeStruct((m, D_MODEL), BF16),
        compiler_params=_cparams(("parallel", "parallel", "arbitrary")),
        name="xattn",
    )(q, kv, kv)


def _ffn_up_kernel(h_ref, wa_ref, wb_ref, cwa_ref, cwb_ref, ba_ref, bb_ref, o_ref, ubuf, carry,
                   *, tiles_per_seq):
    i = pl.program_id(0)
    j = pl.program_id(1)
    tm = h_ref.shape[0]
    first = i % tiles_per_seq == 0
    halves = []
    for t, (w_ref, cw_ref, b_ref) in enumerate(((wa_ref, cwa_ref, ba_ref), (wb_ref, cwb_ref, bb_ref))):
        u = jnp.dot(h_ref[...], w_ref[...], preferred_element_type=F32)
        prev = jnp.where(first, 0.0, carry[t, j])
        carry[t, j] = u[tm - SUBLANES:tm, :]
        halves.append(_causal_conv(u, prev, ubuf.at[t], cw_ref[...]) + b_ref[...])
    o_ref[...] = (_silu(halves[0]) * halves[1]).astype(o_ref.dtype)


def _ffn_up(h, wa, wb, cwa, cwb, ba, bb, seq):
    m, k = h.shape
    n = wa.shape[1]
    tm, tn = TILES["ffn_tm"], TILES["ffn_tn"]
    wspec = pl.BlockSpec((k, tn), lambda i, j: (0, j))
    cspec = pl.BlockSpec((FFN_CONV, tn), lambda i, j: (0, j))
    bspec = pl.BlockSpec((1, tn), lambda i, j: (0, j))
    return pl.pallas_call(
        functools.partial(_ffn_up_kernel, tiles_per_seq=seq // tm),
        grid=(m // tm, n // tn),
        in_specs=[pl.BlockSpec((tm, k), lambda i, j: (i, 0)), wspec, wspec, cspec, cspec, bspec, bspec],
        out_specs=pl.BlockSpec((tm, tn), lambda i, j: (i, j)),
        out_shape=jax.ShapeDtypeStruct((m, n), BF16),
        scratch_shapes=[
            pltpu.VMEM((2, tm + SUBLANES, tn), F32),
            pltpu.VMEM((2, n // tn, SUBLANES, tn), F32),
        ],
        compiler_params=_cparams(("arbitrary", "arbitrary")),
        name="ffn_up",
    )(h, wa, wb, cwa, cwb, ba, bb)


def _pad_cols(a, n):
    return jnp.pad(a, ((0, 0), (0, n - a.shape[1])))


def _layer(x2d, mem2d, batch, seq, mem_len, p):
    m = x2d.shape[0]
    h = GDN_HEADS
    hp = HEADS_PER_GROUP
    w_in = p["w_in"]
    o_z = QKV_W
    o_b = o_z + V_W
    o_p = o_b + 2 * h
    w_qkvz = w_in[:, :o_b].astype(BF16)
    w_gate = w_in[:, o_p:].astype(BF16)
    w_ba = _pad_cols(w_in[:, o_b:o_p], LANES).astype(BF16)
    alog_l = jnp.zeros((1, LANES), F32).at[0, h:2 * h].set(p["a_log"])
    dtb_l = jnp.zeros((1, LANES), F32).at[0, h:2 * h].set(p["dt_bias"])
    w_up = p["w_up"]
    wa_up = _pad_cols(w_up[:, :D_FF], D_FF_PAD).astype(BF16)
    wb_up = _pad_cols(w_up[:, D_FF:], D_FF_PAD).astype(BF16)
    cwa = _pad_cols(p["ffn_conv_w"][:, :D_FF], D_FF_PAD)
    cwb = _pad_cols(p["ffn_conv_w"][:, D_FF:], D_FF_PAD)
    fba = _pad_cols(p["ffn_conv_b"][None, :D_FF], D_FF_PAD)
    fbb = _pad_cols(p["ffn_conv_b"][None, D_FF:], D_FF_PAD)
    w_down = jnp.pad(p["w_down"], ((0, D_FF_PAD - D_FF), (0, 0))).astype(BF16)

    qkvz, ba, hn = _qkv_proj(x2d, p["mix_pre_norm"], w_qkvz, p["conv_qkv"], w_ba, seq)
    gate = _gate_proj(hn, w_gate)
    gates = _gates(ba, alog_l, dtb_l)
    if HEAD_GROUPS == 1:
        colg4 = gates[None]
    else:
        g4 = gates[:, :3 * h].reshape(m, 3, HEAD_GROUPS, hp).transpose(2, 0, 1, 3)
        colg4 = jnp.pad(g4.reshape(HEAD_GROUPS, m, 3 * hp), ((0, 0), (0, 0), (0, LANES - 3 * hp)))
    rowg4 = gates[:, h:2 * h].reshape(batch, seq, HEAD_GROUPS, hp).transpose(0, 2, 3, 1)
    cdecay = gates[CHUNK - 1::CHUNK, 3 * h:4 * h].reshape(batch, seq // CHUNK, h)
    cdecay = cdecay.transpose(0, 2, 1).reshape(batch * h, seq // CHUNK)
    og = _gdn(qkvz, colg4, rowg4, cdecay, p["gdn_norm"], batch, seq)
    pooled = _pool(gate, p["pool_w"].astype(BF16), p["pool_scale"], batch, seq)
    merged = _merge(og, pooled, p["w_branch_a"].astype(BF16), p["w_branch_b"].astype(BF16), gate)
    x1, q = _mm_norm_res(merged, p["w_mix_out"].astype(BF16), x2d, p["mix_post_norm"],
                         p["xa_pre_norm"], tm=TILES["res_tm"], next_proj=p["w_xq"].astype(BF16))
    kv = _norm_mm(mem2d, p["mem_norm"], p["w_xkv"].astype(BF16), tm=mem2d.shape[0], tn=1024,
                  out_dtype=BF16)
    xo = _xattn(q, kv, batch, seq, mem_len)
    x2, h3 = _mm_norm_res(xo, p["w_xo"].astype(BF16), x1, p["xa_post_norm"], p["ffn_pre_norm"],
                          tm=TILES["res_tm"])
    act = _ffn_up(h3, wa_up, wb_up, cwa, cwb, fba, fbb, seq)
    x3, _ = _mm_norm_res(act, w_down, x2, p["ffn_post_norm"], None, tm=TILES["down_tm"])
    return x3


def kernel(x, mem, mix_pre_norm, w_in, conv_qkv, a_log, dt_bias, gdn_norm, pool_w, pool_scale,
           w_branch_a, w_branch_b, w_mix_out, mix_post_norm, xa_pre_norm, mem_norm, w_xq, w_xkv,
           w_xo, xa_post_norm, ffn_pre_norm, w_up, ffn_conv_w, ffn_conv_b, w_down, ffn_post_norm):
    params = dict(mix_pre_norm=mix_pre_norm, w_in=w_in, conv_qkv=conv_qkv, a_log=a_log,
                  dt_bias=dt_bias, gdn_norm=gdn_norm, pool_w=pool_w, pool_scale=pool_scale,
                  w_branch_a=w_branch_a, w_branch_b=w_branch_b, w_mix_out=w_mix_out,
                  mix_post_norm=mix_post_norm, xa_pre_norm=xa_pre_norm, mem_norm=mem_norm,
                  w_xq=w_xq, w_xkv=w_xkv, w_xo=w_xo, xa_post_norm=xa_post_norm,
                  ffn_pre_norm=ffn_pre_norm, w_up=w_up, ffn_conv_w=ffn_conv_w,
                  ffn_conv_b=ffn_conv_b, w_down=w_down, ffn_post_norm=ffn_post_norm)
    batch, seq, d = x.shape
    mem_len = mem.shape[1]
    x2d = x.reshape(batch * seq, d)
    mem2d = mem.reshape(batch * mem_len, d)
    for l in range(w_in.shape[0]):
        x2d = _layer(x2d, mem2d, batch, seq, mem_len, {k: v[l] for k, v in params.items()})
    return x2d.reshape(batch, seq, d)
```

```python
import functools

import jax
import jax.numpy as jnp
from jax import lax
from jax.experimental import pallas as pl
from jax.experimental.pallas import tpu as pltpu

F32 = jnp.float32
BF16 = jnp.bfloat16

D_MODEL = 2048
CHUNK = 64
GDN_HEADS = 16
GDN_DK = 128
GDN_DV = 128
GDN_CONV = 4
QK_W = GDN_HEADS * GDN_DK
V_W = GDN_HEADS * GDN_DV
QKV_W = 2 * QK_W + V_W
POOL_WINDOWS = (2, 4, 8, 16)
POOL_WIDTH = D_MODEL // 2
POOL_GROUP = POOL_WIDTH // 4
XA_HEADS = 4
XA_HEAD_DIM = D_MODEL // XA_HEADS
D_FF = 5504
FFN_CONV = 3
EPS = 1e-6

LANES = 128
SUBLANES = 8
VMEM_LIMIT = 56 * 1024 * 1024

COL_Z = QKV_W
QKVZ_W = QKV_W + V_W
GCOL_P = 0
GCOL_GA = POOL_WIDTH
GCOL_GB = GCOL_GA + D_MODEL

HEADS_PER_GROUP = 16
HEAD_GROUPS = GDN_HEADS // HEADS_PER_GROUP
GROUP_W = HEADS_PER_GROUP * GDN_DK
GDN_ROWS = 128
CHUNKS_PER_STEP = GDN_ROWS // CHUNK
LOG2_CHUNK = 6
NEG_BIG = -1e30

D_FF_PAD = 5632
PROJ_SUB = 256
PROJ_ROWS = 256
CONV_PHASES = 4

TILES = dict(
    proj_tm=1024, proj_tn=1024,
    gate_rows=512,
    pool_rows=512,
    merge_tm=1024, merge_tn=1024,
    res_tm=512,
    ffn_tm=1024, ffn_tn=512,
    down_tm=512,
)


def _cparams(sem):
    return pltpu.CompilerParams(dimension_semantics=sem, vmem_limit_bytes=VMEM_LIMIT)


def _rms(xf, w):
    ms = jnp.mean(xf * xf, axis=-1, keepdims=True)
    return xf * lax.rsqrt(ms + EPS) * w


def _silu(x):
    hx = 0.5 * x
    return hx + hx * jnp.tanh(hx)


def _bdot(a, b):
    return jnp.dot(a.astype(BF16), b.astype(BF16), preferred_element_type=F32)


def _causal_conv(u, prev, ubuf, cw):
    tm = u.shape[0]
    width = cw.shape[0]
    ubuf[0:SUBLANES, :] = prev
    ubuf[SUBLANES:SUBLANES + tm, :] = u
    y = u * cw[width - 1:width, :]
    for d in range(width - 1):
        off = SUBLANES - (width - 1) + d
        y = y + ubuf[off:off + tm, :] * cw[d:d + 1, :]
    return y


def _qkv_proj_kernel(x_ref, nw_ref, w_ref, cw_ref, wba_ref, o_ref, ba_ref, h_ref, ubuf, stage, carry, *,
                     tiles_per_seq, tiles_per_part):
    i = pl.program_id(0)
    j = pl.program_id(1)
    tm, tn = o_ref.shape

    @pl.when(j == 0)
    def _():
        h_ref[...] = _rms(x_ref[...], nw_ref[...]).astype(BF16)
        ba_ref[...] = jnp.dot(h_ref[...], wba_ref[...], preferred_element_type=F32)

    is_q = j < tiles_per_part
    is_qk = j < 2 * tiles_per_part
    na = jnp.where(is_q, float(GDN_DK), jnp.where(is_qk, 1.0, 0.0))
    nb = jnp.where(is_q, GDN_DK * EPS, jnp.where(is_qk, EPS, 1.0))
    first = i % tiles_per_seq == 0

    @pl.when(j < 3 * tiles_per_part)
    def _():
        nph = CONV_PHASES
        for c in range(tn // PROJ_SUB):
            cs = slice(c * PROJ_SUB, (c + 1) * PROJ_SUB)
            u = jnp.dot(h_ref[...], w_ref[:, cs], preferred_element_type=F32)
            prev = jnp.where(first, 0.0, carry[j, :, cs])
            carry[j, :, cs] = u[tm - SUBLANES:tm, :]
            for s in range(PROJ_SUB // LANES):
                slab = c * (PROJ_SUB // LANES) + s
                ls = slice(s * LANES, (s + 1) * LANES)
                ubuf[slab, 0:SUBLANES, :] = prev[:, ls]
                ubuf[slab, SUBLANES:SUBLANES + tm, :] = u[:, ls]
            for s in range(PROJ_SUB // LANES):
                slab = c * (PROJ_SUB // LANES) + s
                cw = cw_ref[:, slab * LANES:(slab + 1) * LANES]
                for ph in range(nph):
                    y = None
                    for d in range(GDN_CONV):
                        start = SUBLANES - (GDN_CONV - 1) + d + ph
                        term = ubuf[slab, pl.ds(start, tm // nph, stride=nph), :] * cw[d:d + 1, :]
                        y = term if y is None else y + term
                    y = _silu(y)
                    ss = jnp.sum(y * y, axis=-1, keepdims=True)
                    f = lax.rsqrt(ss * na + nb)
                    stage[slab, pl.ds(ph, tm // nph, stride=nph), :] = y * f
                o_ref[:, slab * LANES:(slab + 1) * LANES] = stage[slab].astype(o_ref.dtype)

    @pl.when(j >= 3 * tiles_per_part)
    def _():
        rb = PROJ_ROWS
        for c in range(tn // PROJ_SUB):
            cs = slice(c * PROJ_SUB, (c + 1) * PROJ_SUB)
            for r in range(tm // rb):
                rs = slice(r * rb, (r + 1) * rb)
                u = jnp.dot(h_ref[rs, :], w_ref[:, cs], preferred_element_type=F32)
                o_ref[rs, cs] = _silu(u).astype(o_ref.dtype)


def _qkv_proj(x, nw, w, conv_w, w_ba, seq):
    m, k = x.shape
    n = w.shape[1]
    tm, tn = TILES["proj_tm"], TILES["proj_tn"]
    return pl.pallas_call(
        functools.partial(_qkv_proj_kernel, tiles_per_seq=seq // tm, tiles_per_part=QK_W // tn),
        grid=(m // tm, n // tn),
        in_specs=[
            pl.BlockSpec((tm, k), lambda i, j: (i, 0)),
            pl.BlockSpec((1, k), lambda i, j: (0, 0)),
            pl.BlockSpec((k, tn), lambda i, j: (0, j)),
            pl.BlockSpec((GDN_CONV, tn), lambda i, j: (0, jnp.minimum(j, QKV_W // tn - 1))),
            pl.BlockSpec((k, LANES), lambda i, j: (0, 0)),
        ],
        out_specs=[pl.BlockSpec((tm, tn), lambda i, j: (i, j)),
                   pl.BlockSpec((tm, LANES), lambda i, j: (i, 0)),
                   pl.BlockSpec((tm, k), lambda i, j: (i, 0))],
        out_shape=[jax.ShapeDtypeStruct((m, n), BF16), jax.ShapeDtypeStruct((m, LANES), F32),
                   jax.ShapeDtypeStruct((m, k), BF16)],
        scratch_shapes=[
            pltpu.VMEM((tn // LANES, tm + SUBLANES, LANES), F32),
            pltpu.VMEM((tn // LANES, tm, LANES), F32),
            pltpu.VMEM((n // tn, SUBLANES, tn), F32),
        ],
        compiler_params=_cparams(("arbitrary", "arbitrary")),
        name="qkv_proj",
    )(x, nw.reshape(1, k), w, conv_w, w_ba)


def _gate_proj_kernel(h_ref, w_ref, o_ref, *, pool_tiles):
    j = pl.program_id(1)
    is_pool = j < pool_tiles
    tm, tn = o_ref.shape
    rb = PROJ_ROWS
    for c in range(tn // PROJ_SUB):
        cs = slice(c * PROJ_SUB, (c + 1) * PROJ_SUB)
        for r in range(tm // rb):
            rs = slice(r * rb, (r + 1) * rb)
            u = jnp.dot(h_ref[rs, :], w_ref[:, cs], preferred_element_type=F32)
            out = jnp.where(is_pool, u, jax.nn.sigmoid(u))
            o_ref[rs, cs] = out.astype(o_ref.dtype)


def _gate_proj(h, w):
    m, k = h.shape
    n = w.shape[1]
    tm, tn = TILES["proj_tm"], TILES["proj_tn"]
    return pl.pallas_call(
        functools.partial(_gate_proj_kernel, pool_tiles=POOL_WIDTH // tn),
        grid=(m // tm, n // tn),
        in_specs=[
            pl.BlockSpec((tm, k), lambda i, j: (i, 0)),
            pl.BlockSpec((k, tn), lambda i, j: (0, j)),
        ],
        out_specs=pl.BlockSpec((tm, tn), lambda i, j: (i, j)),
        out_shape=jax.ShapeDtypeStruct((m, n), BF16),
        compiler_params=_cparams(("parallel", "arbitrary")),
        name="gate_proj",
    )(h, w)


def _norm_mm_kernel(x_ref, nw_ref, w_ref, o_ref, h_ref):
    @pl.when(pl.program_id(1) == 0)
    def _():
        h_ref[...] = _rms(x_ref[...], nw_ref[...]).astype(BF16)

    o_ref[...] = jnp.dot(h_ref[...], w_ref[...], preferred_element_type=F32).astype(o_ref.dtype)


def _norm_mm(x, nw, w, *, tm, tn, out_dtype):
    m, k = x.shape
    n = w.shape[1]
    return pl.pallas_call(
        _norm_mm_kernel,
        grid=(m // tm, n // tn),
        in_specs=[
            pl.BlockSpec((tm, k), lambda i, j: (i, 0)),
            pl.BlockSpec((1, k), lambda i, j: (0, 0)),
            pl.BlockSpec((k, tn), lambda i, j: (0, j)),
        ],
        out_specs=pl.BlockSpec((tm, tn), lambda i, j: (i, j)),
        out_shape=jax.ShapeDtypeStruct((m, n), out_dtype),
        scratch_shapes=[pltpu.VMEM((tm, k), BF16)],
        compiler_params=_cparams(("parallel", "arbitrary")),
        name="norm_mm",
    )(x, nw.reshape(1, k), w)


def _mm_norm_res_kernel(a_ref, w_ref, res_ref, pw_ref, *rest, with_next, with_proj):
    rest = list(rest)
    nw_ref = rest.pop(0) if with_next else None
    w2_ref = rest.pop(0) if with_proj else None
    x_ref = rest.pop(0)
    y = jnp.dot(a_ref[...], w_ref[...], preferred_element_type=F32)
    xn = res_ref[...] + _rms(y, pw_ref[...])
    x_ref[...] = xn
    if with_next:
        h = _rms(xn, nw_ref[...]).astype(BF16)
        if with_proj:
            h = jnp.dot(h, w2_ref[...], preferred_element_type=F32).astype(BF16)
        rest[0][...] = h


def _mm_norm_res(a, w, res, post_w, next_w, *, tm, next_proj=None):
    m, k = a.shape
    n = w.shape[1]
    with_next = next_w is not None
    with_proj = next_proj is not None
    row = pl.BlockSpec((tm, n), lambda i: (i, 0))
    vec = pl.BlockSpec((1, n), lambda i: (0, 0))
    wspec = pl.BlockSpec((k, n), lambda i: (0, 0), pipeline_mode=pl.Buffered(1))
    in_specs = [pl.BlockSpec((tm, k), lambda i: (i, 0)), wspec, row, vec]
    args = [a, w, res, post_w.reshape(1, n)]
    out_specs = [row]
    out_shape = [jax.ShapeDtypeStruct((m, n), F32)]
    if with_next:
        in_specs.append(vec)
        args.append(next_w.reshape(1, n))
        if with_proj:
            in_specs.append(pl.BlockSpec(next_proj.shape, lambda i: (0, 0), pipeline_mode=pl.Buffered(1)))
            args.append(next_proj)
        out_specs.append(row)
        out_shape.append(jax.ShapeDtypeStruct((m, n), BF16))
    outs = pl.pallas_call(
        functools.partial(_mm_norm_res_kernel, with_next=with_next, with_proj=with_proj),
        grid=(m // tm,),
        in_specs=in_specs,
        out_specs=out_specs,
        out_shape=out_shape,
        compiler_params=_cparams(("parallel",)),
        name="mm_norm_res",
    )(*args)
    return outs if with_next else (outs[0], None)


def _gates_kernel(ba_ref, alog_ref, dtb_ref, o_ref):
    rows = ba_ref.shape[0]
    ba = ba_ref[...]
    beta = jax.nn.sigmoid(ba)
    xs = ba + dtb_ref[...]
    softplus = jnp.maximum(xs, 0.0) + jnp.log1p(jnp.exp(-jnp.abs(xs)))
    g = -jnp.exp(alog_ref[...]) * softplus
    ri = lax.broadcasted_iota(jnp.int32, (CHUNK, CHUNK), 0)
    ci = lax.broadcasted_iota(jnp.int32, (CHUNK, CHUNK), 1)
    tri = jnp.where(ri >= ci, 1.0, 0.0).astype(F32)
    ones = jnp.ones((CHUNK, CHUNK), F32)
    gcs, gls = [], []
    for c in range(rows // CHUNK):
        gch = g[c * CHUNK:(c + 1) * CHUNK]
        gcs.append(jnp.dot(tri, gch, preferred_element_type=F32, precision=lax.Precision.HIGHEST))
        gls.append(jnp.dot(ones, gch, preferred_element_type=F32, precision=lax.Precision.HIGHEST))
    gc = jnp.concatenate(gcs, axis=0)
    gl = jnp.concatenate(gls, axis=0)
    lane = lax.broadcasted_iota(jnp.int32, ba.shape, 1)
    h = GDN_HEADS
    out = jnp.where(lane < h, beta, gc)
    out = jnp.where(lane < 2 * h, out, pltpu.roll(gl - gc, h, axis=1))
    out = jnp.where(lane < 3 * h, out, pltpu.roll(jnp.exp(gl), 2 * h, axis=1))
    o_ref[...] = out


def _gates(ba, alog_l, dtb_l):
    m = ba.shape[0]
    rows = TILES["gate_rows"]
    vec = pl.BlockSpec((1, LANES), lambda i: (0, 0))
    return pl.pallas_call(
        _gates_kernel,
        grid=(m // rows,),
        in_specs=[pl.BlockSpec((rows, LANES), lambda i: (i, 0)), vec, vec],
        out_specs=pl.BlockSpec((rows, LANES), lambda i: (i, 0)),
        out_shape=jax.ShapeDtypeStruct((m, LANES), F32),
        compiler_params=_cparams(("parallel",)),
        name="gdn_gates",
    )(ba, alog_l, dtb_l)


def _gdn_kernel(cd_ref, q_ref, k_ref, v_ref, z_ref, colg_ref, rowg_ref, gn_ref, o_ref,
                lv_ref, lmask, s_ref, uv_buf, wq_buf, kd_buf, at_buf, ubuf, obuf, *, n_tiles):
    b = pl.program_id(0)
    hg = pl.program_id(1)
    si = pl.program_id(2)
    R = GDN_ROWS
    hp = HEADS_PER_GROUP
    heads = range(hp)
    sls = [slice(hl * GDN_DK, (hl + 1) * GDN_DK) for hl in heads]
    wslot = si % 2
    rslot = 1 - wslot

    @pl.when(si == 0)
    def _():
        s_ref[...] = jnp.zeros_like(s_ref)
        uv_buf[1] = jnp.zeros(uv_buf.shape[1:], uv_buf.dtype)
        wq_buf[1] = jnp.zeros(wq_buf.shape[1:], wq_buf.dtype)
        kd_buf[1] = jnp.zeros(kd_buf.shape[1:], kd_buf.dtype)
        at_buf[1] = jnp.zeros(at_buf.shape[1:], at_buf.dtype)
        ri = lax.broadcasted_iota(jnp.int32, (R, R), 0)
        ci = lax.broadcasted_iota(jnp.int32, (R, R), 1)
        x = ri ^ ci
        lv = jnp.full((R, R), -1, jnp.int32)
        for t in range(LOG2_CHUNK):
            lv = lv + jnp.where(x >= (1 << t), 1, 0)
        ok = ((ri >> LOG2_CHUNK) == (ci >> LOG2_CHUNK)) & (ri >= ci)
        lv = jnp.where(ok, lv, -2)
        lv_ref[...] = lv
        for level in range(1, LOG2_CHUNK):
            lmask[level - 1] = jnp.where(lv == level, 1.0, 0.0).astype(BF16)

    p1 = {}

    def p1_scores():
        p1["qk_kk"] = []
        for hl in heads:
            kb = k_ref[:, sls[hl]]
            p1["qk_kk"].append(lax.dot_general(jnp.concatenate([q_ref[:, sls[hl]], kb], axis=0), kb,
                                               (((1,), (1,)), ((), ())), preferred_element_type=F32))

    def p1_decay():
        lv = lv_ref[...]
        incl = lv >= -1
        colg = colg_ref[...]
        p1["beta"] = [colg[:, hl:hl + 1] for hl in heads]
        p1["exp_g"] = [jnp.exp(colg[:, hp + hl:hp + hl + 1]) for hl in heads]
        p1["low"], p1["xinv"] = [], []
        for hl in heads:
            gap = colg[:, hp + hl:hp + hl + 1] - rowg_ref[hl:hl + 1, :]
            decay = jnp.exp(jnp.where(incl, gap, NEG_BIG))
            at_buf[wslot, hl] = (decay * p1["qk_kk"][hl][0:R]).astype(BF16)
            lo = jnp.where(lv >= 0, p1["beta"][hl] * decay * p1["qk_kk"][hl][R:2 * R], 0.0)
            p1["low"].append(lo.astype(BF16))
            p1["xinv"].append((jnp.where(lv == -1, 1.0, 0.0) - jnp.where(lv == 0, lo, 0.0)).astype(BF16))

    def p1_level_a(level):
        def run():
            p1["ys"] = [jnp.dot(p1["low"][hl] * lmask[level - 1], p1["xinv"][hl],
                                preferred_element_type=F32).astype(BF16) for hl in heads]
        return run

    def p1_level_b():
        p1["xinv"] = [p1["xinv"][hl] - jnp.dot(p1["xinv"][hl], p1["ys"][hl],
                                               preferred_element_type=F32).astype(BF16) for hl in heads]

    def p1_solve():
        p1["sol"] = []
        for hl in heads:
            bk = (p1["beta"][hl] * p1["exp_g"][hl]) * k_ref[:, sls[hl]].astype(F32)
            rhs = jnp.concatenate([p1["beta"][hl] * v_ref[:, sls[hl]].astype(F32), bk], axis=1)
            p1["sol"].append(jnp.dot(p1["xinv"][hl], rhs.astype(BF16), preferred_element_type=F32))

    def p1_store():
        colg = colg_ref[...]
        for hl in heads:
            q_dec = q_ref[:, sls[hl]].astype(F32) * p1["exp_g"][hl]
            uv_buf[wslot, hl] = p1["sol"][hl][:, 0:GDN_DV]
            wq_buf[wslot, hl, 0:R, :] = p1["sol"][hl][:, GDN_DV:].astype(BF16)
            wq_buf[wslot, hl, R:2 * R, :] = q_dec.astype(BF16)
            kd_buf[wslot, hl] = (k_ref[:, sls[hl]].astype(F32)
                                 * jnp.exp(colg[:, 2 * hp + hl:2 * hp + hl + 1])).astype(BF16)

    p1_stages = [p1_scores, p1_decay]
    for level in range(1, LOG2_CHUNK):
        p1_stages += [p1_level_a(level), p1_level_b]
    p1_stages += [p1_solve, p1_store]

    p2 = {}
    tile = jnp.maximum(si - 1, 0)

    def p2_read(c):
        def run():
            p2["states"] = [s_ref[hl] for hl in heads]
            p2["wqs"] = []
            for hl in heads:
                lhs = jnp.concatenate([wq_buf[rslot, hl, c * CHUNK:(c + 1) * CHUNK, :],
                                       wq_buf[rslot, hl, R + c * CHUNK:R + (c + 1) * CHUNK, :]], axis=0)
                p2["wqs"].append(jnp.dot(lhs, p2["states"][hl].astype(BF16),
                                         preferred_element_type=F32))
        return run

    def p2_update(c):
        def run():
            rs = slice(c * CHUNK, (c + 1) * CHUNK)
            for hl in heads:
                u = (uv_buf[rslot, hl, rs, :] - p2["wqs"][hl][0:CHUNK]).astype(BF16)
                ubuf[hl, rs, :] = u
                obuf[hl, rs, :] = p2["wqs"][hl][CHUNK:]
                cd = cd_ref[b * GDN_HEADS + hg * HEADS_PER_GROUP + hl, tile * CHUNKS_PER_STEP + c]
                upd = lax.dot_general(kd_buf[rslot, hl, rs, :], u, (((0,), (0,)), ((), ())),
                                      preferred_element_type=F32)
                s_ref[hl] = cd * p2["states"][hl] + upd
        return run

    def p2_output():
        for hl in heads:
            o = obuf[hl] + jnp.dot(at_buf[rslot, hl], ubuf[hl], preferred_element_type=F32)
            on = _rms(o, gn_ref[...]) * z_ref[:, sls[hl]].astype(F32)
            o_ref[:, sls[hl]] = on.astype(o_ref.dtype)

    p2_stages = []
    for c in range(CHUNKS_PER_STEP):
        p2_stages += [p2_read(c), p2_update(c)]
    p2_stages.append(p2_output)

    n1, n2 = len(p1_stages), len(p2_stages)
    i2 = 0
    for i1, stage in enumerate(p1_stages):
        stage()
        while i2 < n2 and (i2 + 1) * n1 <= (i1 + 1) * n2:
            p2_stages[i2]()
            i2 += 1
    while i2 < n2:
        p2_stages[i2]()
        i2 += 1


def _gdn(qkvz, colg4, rowg4, cdecay, gdn_norm, batch, seq):
    m = qkvz.shape[0]
    ns = seq // GDN_ROWS
    gw = GROUP_W
    R = GDN_ROWS

    def cur(s):
        return jnp.minimum(s, ns - 1)

    def prv(s):
        return jnp.maximum(s - 1, 0)

    def xspec(col0):
        return pl.BlockSpec((R, gw), lambda b, h, s: (b * ns + cur(s), col0 // gw + h))

    hp = HEADS_PER_GROUP
    return pl.pallas_call(
        functools.partial(_gdn_kernel, n_tiles=ns),
        grid=(batch, HEAD_GROUPS, ns + 1),
        in_specs=[
            pl.BlockSpec(memory_space=pltpu.SMEM),
            xspec(0), xspec(QK_W), xspec(2 * QK_W),
            pl.BlockSpec((R, gw), lambda b, h, s: (b * ns + prv(s), COL_Z // gw + h)),
            pl.BlockSpec((None, R, LANES), lambda b, h, s: (h, b * ns + cur(s), 0)),
            pl.BlockSpec((None, None, hp, R), lambda b, h, s: (b, h, 0, cur(s))),
            pl.BlockSpec((1, GDN_DV), lambda b, h, s: (0, 0)),
        ],
        out_specs=pl.BlockSpec((R, gw), lambda b, h, s: (b * ns + prv(s), h)),
        out_shape=jax.ShapeDtypeStruct((m, V_W), BF16),
        scratch_shapes=[
            pltpu.VMEM((R, R), jnp.int32),
            pltpu.VMEM((LOG2_CHUNK - 1, R, R), BF16),
            pltpu.VMEM((hp, GDN_DK, GDN_DV), F32),
            pltpu.VMEM((2, hp, R, GDN_DV), F32),
            pltpu.VMEM((2, hp, 2 * R, GDN_DK), BF16),
            pltpu.VMEM((2, hp, R, GDN_DK), BF16),
            pltpu.VMEM((2, hp, R, R), BF16),
            pltpu.VMEM((hp, R, GDN_DV), BF16),
            pltpu.VMEM((hp, R, GDN_DV), F32),
        ],
        compiler_params=_cparams(("parallel", "parallel", "arbitrary")),
        name="gdn",
    )(cdecay, qkvz, qkvz, qkvz, qkvz, colg4, rowg4, gdn_norm.reshape(1, GDN_DV))


POOL_HALO = 16


def _pool_kernel(p_ref, w_ref, sc_ref, o_ref, buf):
    si = pl.program_id(1)
    R = p_ref.shape[0]

    @pl.when(si == 0)
    def _():
        buf[0:POOL_HALO, :] = jnp.zeros((POOL_HALO, POOL_WIDTH), F32)

    buf[POOL_HALO:POOL_HALO + R, :] = p_ref[...].astype(F32)
    t = si * R + lax.broadcasted_iota(jnp.int32, (R, 1), 0)
    for gi, win in enumerate(POOL_WINDOWS):
        sl = slice(gi * POOL_GROUP, (gi + 1) * POOL_GROUP)
        x = buf[POOL_HALO:POOL_HALO + R, sl]
        acc = x
        for d in range(1, win):
            acc = acc + buf[POOL_HALO - d:POOL_HALO - d + R, sl]
        cnt = jnp.minimum(t + 1, win).astype(F32)
        y = acc / cnt - x
        o = _bdot(y, w_ref[gi]) * sc_ref[:, sl]
        o_ref[:, sl] = o.astype(o_ref.dtype)
    buf[0:POOL_HALO, :] = buf[R:R + POOL_HALO, :]


def _pool(gate, pool_w, pool_scale, batch, seq):
    m = gate.shape[0]
    rows = TILES["pool_rows"]
    ns = seq // rows
    return pl.pallas_call(
        _pool_kernel,
        grid=(batch, ns),
        in_specs=[
            pl.BlockSpec((rows, POOL_WIDTH), lambda b, s: (b * ns + s, GCOL_P // POOL_WIDTH)),
            pl.BlockSpec((len(POOL_WINDOWS), POOL_GROUP, POOL_GROUP), lambda b, s: (0, 0, 0)),
            pl.BlockSpec((1, POOL_WIDTH), lambda b, s: (0, 0)),
        ],
        out_specs=pl.BlockSpec((rows, POOL_WIDTH), lambda b, s: (b * ns + s, 0)),
        out_shape=jax.ShapeDtypeStruct((m, POOL_WIDTH), BF16),
        scratch_shapes=[pltpu.VMEM((rows + POOL_HALO, POOL_WIDTH), F32)],
        compiler_params=_cparams(("parallel", "arbitrary")),
        name="pool",
    )(gate, pool_w, pool_scale.reshape(1, POOL_WIDTH))


def _merge_kernel(a_ref, p_ref, wa_ref, wb_ref, ga_ref, gb_ref, o_ref):
    tm, tn = o_ref.shape
    rb = PROJ_ROWS
    for c in range(tn // PROJ_SUB):
        cs = slice(c * PROJ_SUB, (c + 1) * PROJ_SUB)
        for r in range(tm // rb):
            rs = slice(r * rb, (r + 1) * rb)
            ya = jnp.dot(a_ref[rs, :], wa_ref[:, cs], preferred_element_type=F32)
            yb = jnp.dot(p_ref[rs, :], wb_ref[:, cs], preferred_element_type=F32)
            merged = ga_ref[rs, cs].astype(F32) * ya + gb_ref[rs, cs].astype(F32) * yb
            o_ref[rs, cs] = merged.astype(o_ref.dtype)


def _merge(og, pooled, wa, wb, gate):
    m = og.shape[0]
    n = wa.shape[1]
    tm, tn = TILES["merge_tm"], TILES["merge_tn"]
    return pl.pallas_call(
        _merge_kernel,
        grid=(m // tm, n // tn),
        in_specs=[
            pl.BlockSpec((tm, V_W), lambda i, j: (i, 0)),
            pl.BlockSpec((tm, POOL_WIDTH), lambda i, j: (i, 0)),
            pl.BlockSpec((V_W, tn), lambda i, j: (0, j)),
            pl.BlockSpec((POOL_WIDTH, tn), lambda i, j: (0, j)),
            pl.BlockSpec((tm, tn), lambda i, j: (i, GCOL_GA // tn + j)),
            pl.BlockSpec((tm, tn), lambda i, j: (i, GCOL_GB // tn + j)),
        ],
        out_specs=pl.BlockSpec((tm, tn), lambda i, j: (i, j)),
        out_shape=jax.ShapeDtypeStruct((m, n), BF16),
        compiler_params=_cparams(("parallel", "arbitrary")),
        name="merge",
    )(og, pooled, wa, wb, gate, gate)


def _xattn_out_kernel(q_ref, kv_ref, w_ref, res_ref, pw_ref, nw_ref, x_ref, h_ref):
    heads = []
    for hd in range(XA_HEADS):
        ks = slice(hd * XA_HEAD_DIM, (hd + 1) * XA_HEAD_DIM)
        vs = slice(D_MODEL + hd * XA_HEAD_DIM, D_MODEL + (hd + 1) * XA_HEAD_DIM)
        s = lax.dot_general(q_ref[:, ks], kv_ref[:, ks], (((1,), (1,)), ((), ())),
                            preferred_element_type=F32) * (XA_HEAD_DIM ** -0.5)
        s = s - jnp.max(s, axis=-1, keepdims=True)
        e = jnp.exp(s)
        pr = e / jnp.sum(e, axis=-1, keepdims=True)
        heads.append(jnp.dot(pr.astype(BF16), kv_ref[:, vs], preferred_element_type=F32).astype(BF16))
    o = jnp.concatenate(heads, axis=1)
    y = jnp.dot(o, w_ref[...], preferred_element_type=F32)
    xn = res_ref[...] + _rms(y, pw_ref[...])
    x_ref[...] = xn
    h_ref[...] = _rms(xn, nw_ref[...]).astype(BF16)


def _xattn_out(q, kv, w, res, post_w, next_w, seq, mem_len):
    m, n = q.shape
    tm = TILES["res_tm"]
    assert seq % tm == 0, (seq, tm)
    tiles_per_seq = seq // tm
    row = pl.BlockSpec((tm, n), lambda i: (i, 0))
    vec = pl.BlockSpec((1, n), lambda i: (0, 0))
    return pl.pallas_call(
        _xattn_out_kernel,
        grid=(m // tm,),
        in_specs=[
            row,
            pl.BlockSpec((mem_len, 2 * n), lambda i: (i // tiles_per_seq, 0)),
            pl.BlockSpec((n, n), lambda i: (0, 0), pipeline_mode=pl.Buffered(1)),
            row, vec, vec,
        ],
        out_specs=[row, row],
        out_shape=[jax.ShapeDtypeStruct((m, n), F32), jax.ShapeDtypeStruct((m, n), BF16)],
        compiler_params=_cparams(("parallel",)),
        name="xattn_out",
    )(q, kv, w, res, post_w.reshape(1, n), next_w.reshape(1, n))


def _ffn_up_kernel(h_ref, wa_ref, wb_ref, cwa_ref, cwb_ref, ba_ref, bb_ref, o_ref, ubuf, carry,
                   *, tiles_per_seq):
    i = pl.program_id(0)
    j = pl.program_id(1)
    tm = h_ref.shape[0]
    first = i % tiles_per_seq == 0
    halves = []
    for t, (w_ref, cw_ref, b_ref) in enumerate(((wa_ref, cwa_ref, ba_ref), (wb_ref, cwb_ref, bb_ref))):
        u = jnp.dot(h_ref[...], w_ref[...], preferred_element_type=F32)
        prev = jnp.where(first, 0.0, carry[t, j])
        carry[t, j] = u[tm - SUBLANES:tm, :]
        halves.append(_causal_conv(u, prev, ubuf.at[t], cw_ref[...]) + b_ref[...])
    o_ref[...] = (_silu(halves[0]) * halves[1]).astype(o_ref.dtype)


def _ffn_up(h, wa, wb, cwa, cwb, ba, bb, seq):
    m, k = h.shape
    n = wa.shape[1]
    tm, tn = TILES["ffn_tm"], TILES["ffn_tn"]
    wspec = pl.BlockSpec((k, tn), lambda i, j: (0, j))
    cspec = pl.BlockSpec((FFN_CONV, tn), lambda i, j: (0, j))
    bspec = pl.BlockSpec((1, tn), lambda i, j: (0, j))
    return pl.pallas_call(
        functools.partial(_ffn_up_kernel, tiles_per_seq=seq // tm),
        grid=(m // tm, n // tn),
        in_specs=[pl.BlockSpec((tm, k), lambda i, j: (i, 0)), wspec, wspec, cspec, cspec, bspec, bspec],
        out_specs=pl.BlockSpec((tm, tn), lambda i, j: (i, j)),
        out_shape=jax.ShapeDtypeStruct((m, n), BF16),
        scratch_shapes=[
            pltpu.VMEM((2, tm + SUBLANES, tn), F32),
            pltpu.VMEM((2, n // tn, SUBLANES, tn), F32),
        ],
        compiler_params=_cparams(("arbitrary", "arbitrary")),
        name="ffn_up",
    )(h, wa, wb, cwa, cwb, ba, bb)


def _pad_cols(a, n):
    return jnp.pad(a, ((0, 0), (0, n - a.shape[1])))


def _layer(x2d, mem2d, batch, seq, mem_len, p):
    m = x2d.shape[0]
    h = GDN_HEADS
    hp = HEADS_PER_GROUP
    w_in = p["w_in"]
    o_z = QKV_W
    o_b = o_z + V_W
    o_p = o_b + 2 * h
    w_qkvz = w_in[:, :o_b].astype(BF16)
    w_gate = w_in[:, o_p:].astype(BF16)
    w_ba = _pad_cols(w_in[:, o_b:o_p], LANES).astype(BF16)
    alog_l = jnp.zeros((1, LANES), F32).at[0, h:2 * h].set(p["a_log"])
    dtb_l = jnp.zeros((1, LANES), F32).at[0, h:2 * h].set(p["dt_bias"])
    w_up = p["w_up"]
    wa_up = _pad_cols(w_up[:, :D_FF], D_FF_PAD).astype(BF16)
    wb_up = _pad_cols(w_up[:, D_FF:], D_FF_PAD).astype(BF16)
    cwa = _pad_cols(p["ffn_conv_w"][:, :D_FF], D_FF_PAD)
    cwb = _pad_cols(p["ffn_conv_w"][:, D_FF:], D_FF_PAD)
    fba = _pad_cols(p["ffn_conv_b"][None, :D_FF], D_FF_PAD)
    fbb = _pad_cols(p["ffn_conv_b"][None, D_FF:], D_FF_PAD)
    w_down = jnp.pad(p["w_down"], ((0, D_FF_PAD - D_FF), (0, 0))).astype(BF16)

    qkvz, ba, hn = _qkv_proj(x2d, p["mix_pre_norm"], w_qkvz, p["conv_qkv"], w_ba, seq)
    gate = _gate_proj(hn, w_gate)
    gates = _gates(ba, alog_l, dtb_l)
    if HEAD_GROUPS == 1:
        colg4 = gates[None]
    else:
        g4 = gates[:, :3 * h].reshape(m, 3, HEAD_GROUPS, hp).transpose(2, 0, 1, 3)
        colg4 = jnp.pad(g4.reshape(HEAD_GROUPS, m, 3 * hp), ((0, 0), (0, 0), (0, LANES - 3 * hp)))
    rowg4 = gates[:, h:2 * h].reshape(batch, seq, HEAD_GROUPS, hp).transpose(0, 2, 3, 1)
    cdecay = gates[CHUNK - 1::CHUNK, 3 * h:4 * h].reshape(batch, seq // CHUNK, h)
    cdecay = cdecay.transpose(0, 2, 1).reshape(batch * h, seq // CHUNK)
    og = _gdn(qkvz, colg4, rowg4, cdecay, p["gdn_norm"], batch, seq)
    pooled = _pool(gate, p["pool_w"].astype(BF16), p["pool_scale"], batch, seq)
    merged = _merge(og, pooled, p["w_branch_a"].astype(BF16), p["w_branch_b"].astype(BF16), gate)
    x1, q = _mm_norm_res(merged, p["w_mix_out"].astype(BF16), x2d, p["mix_post_norm"],
                         p["xa_pre_norm"], tm=TILES["res_tm"], next_proj=p["w_xq"].astype(BF16))
    kv = _norm_mm(mem2d, p["mem_norm"], p["w_xkv"].astype(BF16), tm=mem2d.shape[0], tn=1024,
                  out_dtype=BF16)
    x2, h3 = _xattn_out(q, kv, p["w_xo"].astype(BF16), x1, p["xa_post_norm"], p["ffn_pre_norm"],
                        seq, mem_len)
    act = _ffn_up(h3, wa_up, wb_up, cwa, cwb, fba, fbb, seq)
    x3, _ = _mm_norm_res(act, w_down, x2, p["ffn_post_norm"], None, tm=TILES["down_tm"])
    return x3


def kernel(x, mem, mix_pre_norm, w_in, conv_qkv, a_log, dt_bias, gdn_norm, pool_w, pool_scale,
           w_branch_a, w_branch_b, w_mix_out, mix_post_norm, xa_pre_norm, mem_norm, w_xq, w_xkv,
           w_xo, xa_post_norm, ffn_pre_norm, w_up, ffn_conv_w, ffn_conv_b, w_down, ffn_post_norm):
    params = dict(mix_pre_norm=mix_pre_norm, w_in=w_in, conv_qkv=conv_qkv, a_log=a_log,
                  dt_bias=dt_bias, gdn_norm=gdn_norm, pool_w=pool_w, pool_scale=pool_scale,
                  w_branch_a=w_branch_a, w_branch_b=w_branch_b, w_mix_out=w_mix_out,
                  mix_post_norm=mix_post_norm, xa_pre_norm=xa_pre_norm, mem_norm=mem_norm,
                  w_xq=w_xq, w_xkv=w_xkv, w_xo=w_xo, xa_post_norm=xa_post_norm,
                  ffn_pre_norm=ffn_pre_norm, w_up=w_up, ffn_conv_w=ffn_conv_w,
                  ffn_conv_b=ffn_conv_b, w_down=w_down, ffn_post_norm=ffn_post_norm)
    batch, seq, d = x.shape
    mem_len = mem.shape[1]
    x2d = x.reshape(batch * seq, d)
    mem2d = mem.reshape(batch * mem_len, d)
    for l in range(w_in.shape[0]):
        x2d = _layer(x2d, mem2d, batch, seq, mem_len, {k: v[l] for k, v in params.items()})
    return x2d.reshape(batch, seq, d)
```
